```python
import jax, jax.numpy as jnp
from jax import lax
import numpy as np

D_MODEL = 1024
BATCH = 16
SEQ = 2048
DEPTH = 1

D_MIX = D_MODEL
GLA_HEADS = 4
GLA_DK = 64
GLA_DV = 128
GLA_QK = GLA_HEADS * GLA_DK
GLA_WIDTH = GLA_HEADS * GLA_DV
GLA_GATE_RANK = 16
GLA_GATE_NORMALIZER = 16.0
GLA_CHUNK = 64
SG_GROUPS = 4
SG_CH = 128
SG_WIDTH = SG_GROUPS * SG_CH
SG_CHUNK = 128
IN_SPLITS = (GLA_QK, GLA_QK, GLA_WIDTH, GLA_WIDTH, GLA_GATE_RANK, SG_WIDTH, SG_WIDTH)
IN_COLS = sum(IN_SPLITS)
N_EXPERTS = 32
TOP_K = 4
D_EXPERT = D_MODEL
SWIGLU_LIMIT = 7.0
SWIGLU_ALPHA = 1.702
MOE_BLOCK = 128
EPS = 1e-6

kernel_name = 'hymba_gla_sgu_moe_layer'


def rmsnorm(x, w):
    xf = x.astype(jnp.float32)
    y = xf * lax.rsqrt(jnp.mean(xf * xf, axis=-1, keepdims=True) + EPS) * w.astype(jnp.float32)
    return y.astype(x.dtype)


def gla_chunked(q, k, v, log_a):
    B, S, H, K = q.shape
    C = GLA_CHUNK
    N = S // C

    def to_chunks(t):
        return t.reshape(B, N, C, H, t.shape[-1]).transpose(0, 3, 1, 2, 4)

    q, k, v, log_a = to_chunks(q) * (K ** -0.5), to_chunks(k), to_chunks(v), to_chunks(log_a)
    b = jnp.cumsum(log_a, axis=3)
    b_last = b[:, :, :, -1:, :]
    q_dec = q * jnp.exp(b)
    k_inv = k * jnp.exp(-b)
    k_to_end = k * jnp.exp(b_last - b)
    causal = jnp.tril(jnp.ones((C, C), dtype=bool))
    scores = jnp.where(causal, jnp.einsum('bhnck,bhnsk->bhncs', q_dec, k_inv), 0.0)
    o_intra = jnp.einsum('bhncs,bhnsv->bhncv', scores, v)
    chunk_update = jnp.einsum('bhnsk,bhnsv->bhnkv', k_to_end, v)
    chunk_decay = jnp.exp(b_last[:, :, :, 0, :])

    def step(state, inp):
        decay, update = inp
        return decay[..., None] * state + update, state

    state0 = jnp.zeros((B, H, K, v.shape[-1]), jnp.float32)
    _, state_prev = lax.scan(step, state0, (jnp.moveaxis(chunk_decay, 2, 0), jnp.moveaxis(chunk_update, 2, 0)))
    o_inter = jnp.einsum('bhnck,nbhkv->bhncv', q_dec, state_prev)
    o = o_intra + o_inter
    return o.transpose(0, 2, 3, 1, 4).reshape(B, S, H, -1)


def spatial_gating(u, v, ln_w, ln_b, w_s, b_s):
    B, S, _ = u.shape
    N = S // SG_CHUNK
    u = jax.nn.gelu(u)
    vf = jax.nn.gelu(v).astype(jnp.float32)
    mu = jnp.mean(vf, axis=-1, keepdims=True)
    var = jnp.mean(jnp.square(vf - mu), axis=-1, keepdims=True)
    vn = (vf - mu) * lax.rsqrt(var + EPS) * ln_w.astype(jnp.float32) + ln_b.astype(jnp.float32)
    vn = vn.reshape(B, N, SG_CHUNK, SG_GROUPS, SG_CH)
    causal = jnp.tril(jnp.ones((SG_CHUNK, SG_CHUNK), dtype=bool))
    w_causal = jnp.where(causal, w_s.astype(jnp.float32), 0.0)
    mixed = jnp.einsum('gts,bnsgc->bntgc', w_causal, vn) + b_s.astype(jnp.float32).T[:, :, None]
    return u * mixed.reshape(B, S, SG_WIDTH).astype(u.dtype)


def clamped_swiglu(gate, up):
    gate = jnp.minimum(gate, SWIGLU_LIMIT)
    up = jnp.clip(up, -SWIGLU_LIMIT, SWIGLU_LIMIT)
    return (up + 1.0) * (gate * jax.nn.sigmoid(SWIGLU_ALPHA * gate))


def moe(h, w_router, b_router, w_gate_up, b_gate_up, w_down, b_down):
    B, S, D = h.shape
    T = B * S
    xt = h.reshape(T, D)
    logits = (xt @ w_router + b_router).astype(jnp.float32)
    top_val, top_idx = lax.top_k(logits, TOP_K)
    top_w = jax.nn.softmax(top_val, axis=-1)
    flat_e = top_idx.reshape(-1)
    flat_tok = jnp.repeat(jnp.arange(T, dtype=jnp.int32), TOP_K)
    flat_w = top_w.reshape(-1)
    order = jnp.argsort(flat_e)
    sorted_e, sorted_tok, sorted_w = flat_e[order], flat_tok[order], flat_w[order]
    counts = jnp.bincount(flat_e, length=N_EXPERTS)
    starts = jnp.cumsum(counts) - counts
    padded = (counts + MOE_BLOCK - 1) // MOE_BLOCK * MOE_BLOCK
    padded_ends = jnp.cumsum(padded)
    padded_starts = padded_ends - padded
    rank = jnp.arange(T * TOP_K, dtype=jnp.int32) - starts[sorted_e]
    dest = padded_starts[sorted_e] + rank
    n_blocks = -(-(T * TOP_K) // MOE_BLOCK) + N_EXPERTS
    x_buf = jnp.zeros((n_blocks * MOE_BLOCK, D), h.dtype).at[dest].set(xt[sorted_tok])
    block_starts = jnp.arange(n_blocks, dtype=jnp.int32) * MOE_BLOCK
    block_e = jnp.minimum(jnp.searchsorted(padded_ends, block_starts, side='right'), N_EXPERTS - 1)

    def expert_block(args):
        xb, e = args
        gu = xb @ w_gate_up[e] + b_gate_up[e]
        y = clamped_swiglu(gu[:, :D_EXPERT], gu[:, D_EXPERT:])
        return y @ w_down[e] + b_down[e]

    y_buf = lax.map(expert_block, (x_buf.reshape(n_blocks, MOE_BLOCK, D), block_e))
    y_sorted = y_buf.reshape(-1, D)[dest]
    out = jnp.zeros((T, D), h.dtype).at[sorted_tok].add(y_sorted * sorted_w[:, None].astype(h.dtype))
    return out.reshape(B, S, D)


def setup_inputs(seed: int = 0) -> dict:
    key = jax.random.key(seed)
    ks = jax.random.split(key, 20)
    L = DEPTH
    nrm = jax.random.normal
    f32 = jnp.float32
    return {
        'x': nrm(ks[0], (BATCH, SEQ, D_MODEL), f32),
        'norm_mix_w': 1.0 + 0.02 * nrm(ks[1], (L, D_MODEL), f32),
        'w_in': nrm(ks[2], (L, D_MODEL, IN_COLS), f32) * D_MODEL ** -0.5,
        'w_gk_up': nrm(ks[3], (L, GLA_GATE_RANK, GLA_QK), f32) * GLA_GATE_RANK ** -0.5,
        'b_gk': 0.1 * nrm(ks[4], (L, GLA_QK), f32),
        'gla_norm_w': 1.0 + 0.02 * nrm(ks[5], (L, GLA_DV), f32),
        'sg_ln_w': 1.0 + 0.02 * nrm(ks[6], (L, SG_WIDTH), f32),
        'sg_ln_b': 0.02 * nrm(ks[7], (L, SG_WIDTH), f32),
        'w_spatial': nrm(ks[8], (L, SG_GROUPS, SG_CHUNK, SG_CHUNK), f32) * SG_CHUNK ** -0.5,
        'b_spatial': 1.0 + 0.02 * nrm(ks[9], (L, SG_GROUPS, SG_CHUNK), f32),
        'w_out': nrm(ks[10], (L, D_MIX, D_MODEL), f32) * D_MIX ** -0.5,
        'norm_ffn_w': 1.0 + 0.02 * nrm(ks[11], (L, D_MODEL), f32),
        'w_router': nrm(ks[12], (L, D_MODEL, N_EXPERTS), f32) * D_MODEL ** -0.5,
        'b_router': 0.01 * nrm(ks[13], (L, N_EXPERTS), f32),
        'w_gate_up': nrm(ks[14], (L, N_EXPERTS, D_MODEL, 2 * D_EXPERT), f32) * D_MODEL ** -0.5,
        'b_gate_up': 0.01 * nrm(ks[15], (L, N_EXPERTS, 2 * D_EXPERT), f32),
        'w_down': nrm(ks[16], (L, N_EXPERTS, D_EXPERT, D_MODEL), f32) * D_EXPERT ** -0.5,
        'b_down': 0.01 * nrm(ks[17], (L, N_EXPERTS, D_MODEL), f32),
        'norm_final_w': 1.0 + 0.02 * nrm(ks[18], (D_MODEL,), f32),
    }


def reference(x, norm_mix_w, w_in, w_gk_up, b_gk, gla_norm_w, sg_ln_w, sg_ln_b, w_spatial, b_spatial,
              w_out, norm_ffn_w, w_router, b_router, w_gate_up, b_gate_up, w_down, b_down, norm_final_w):
    B, S, _ = x.shape
    split_points = list(np.cumsum(IN_SPLITS)[:-1])
    for l in range(DEPTH):
        h = rmsnorm(x, norm_mix_w[l])
        proj = h @ w_in[l]
        q, k, v, g, gk_low, u_sg, v_sg = jnp.split(proj, split_points, axis=-1)
        log_a = jax.nn.log_sigmoid((gk_low @ w_gk_up[l] + b_gk[l]).astype(jnp.float32)) / GLA_GATE_NORMALIZER
        o = gla_chunked(q.astype(jnp.float32).reshape(B, S, GLA_HEADS, GLA_DK),
                        k.astype(jnp.float32).reshape(B, S, GLA_HEADS, GLA_DK),
                        v.astype(jnp.float32).reshape(B, S, GLA_HEADS, GLA_DV),
                        log_a.reshape(B, S, GLA_HEADS, GLA_DK))
        o = o * lax.rsqrt(jnp.mean(o * o, axis=-1, keepdims=True) + EPS) * gla_norm_w[l].astype(jnp.float32)
        o = o * jax.nn.silu(g.astype(jnp.float32)).reshape(B, S, GLA_HEADS, GLA_DV)
        gla_out = o.reshape(B, S, GLA_WIDTH).astype(x.dtype)
        sg_out = spatial_gating(u_sg, v_sg, sg_ln_w[l], sg_ln_b[l], w_spatial[l], b_spatial[l])
        mixed = jnp.concatenate([gla_out, sg_out], axis=-1)
        x = x + mixed @ w_out[l]
        h = rmsnorm(x, norm_ffn_w[l])
        x = x + moe(h, w_router[l], b_router[l], w_gate_up[l], b_gate_up[l], w_down[l], b_down[l])
    return rmsnorm(x, norm_final_w)
```

```python
import functools

import jax
import jax.numpy as jnp
from jax import lax
from jax.experimental import pallas as pl
from jax.experimental.pallas import tpu as pltpu

D_MODEL = 1024
GLA_HEADS = 4
GLA_DK = 64
GLA_DV = 128
GLA_QK = GLA_HEADS * GLA_DK
GLA_WIDTH = GLA_HEADS * GLA_DV
GLA_GATE_RANK = 16
GLA_GATE_NORMALIZER = 16.0
GLA_CHUNK = 64
SG_GROUPS = 4
SG_CH = 128
SG_WIDTH = SG_GROUPS * SG_CH
SG_CHUNK = 128
N_EXPERTS = 32
TOP_K = 4
D_EXPERT = D_MODEL
SWIGLU_LIMIT = 7.0
SWIGLU_ALPHA = 1.702
EPS = 1e-6

SUBLANES = 8
LANES = 128
ROW_TILES = D_MODEL // LANES
assert ROW_TILES == SUBLANES

SEQ_TILE = 256
EXPERT_ROWS = 256
COMBINE_TOKENS = 128
VMEM_LIMIT_BYTES = 56 * 1024 * 1024

_Q0, _K0, _V0, _G0, _U0, _VS0, _GK0 = 0, 256, 512, 1024, 1536, 2048, 2560
IN_COLS_PACKED = 2688

F32 = jnp.float32
BF16 = jnp.bfloat16
_NT = (((1,), (1,)), ((), ()))
_TN = (((0,), (0,)), ((), ()))


def _dot(a, b):
    return jnp.dot(a, b, preferred_element_type=F32)


def _split_bf16(a):
    hi = a.astype(BF16)
    lo = (a - hi.astype(F32)).astype(BF16)
    return hi, lo


def _gelu_tanh(a):
    return 0.5 * a * (1.0 + jnp.tanh(0.7978845608028654 * (a + 0.044715 * (a * a * a))))


def _rows_from_tiles(ref, lead, base, n_rows):
    parts = [ref[(*lead, pl.ds(base + j, n_rows, stride=ROW_TILES), slice(None))] for j in range(ROW_TILES)]
    return jnp.concatenate(parts, axis=1)


def _rows_to_tiles(ref, val, n_rows):
    for j in range(ROW_TILES):
        ref[pl.ds(j, n_rows, stride=ROW_TILES), :] = val[:, j * LANES:(j + 1) * LANES]


def _mix_kernel(x_ref, nmw_ref, wall_ref, wgk_ref, bgk_ref, glanw_ref, lnw_ref, lnb_ref, w2_ref, bsb_ref,
                wout_ref, nfw_ref, wrh_ref, wrl_ref, br_ref,
                x1_ref, h2_ref, idx_ref, wts_ref, rank_ref, cnt_ref,
                state_ref, carry_ref):
    ts = SEQ_TILE
    n_chunks = ts // GLA_CHUNK
    b_id = pl.program_id(0)
    s_id = pl.program_id(1)

    @pl.when(s_id == 0)
    def _():
        state_ref[...] = jnp.zeros_like(state_ref)

    @pl.when((b_id == 0) & (s_id == 0))
    def _():
        carry_ref[...] = jnp.zeros_like(carry_ref)

    x = x_ref[0]
    ms = jnp.mean(x * x, axis=-1, keepdims=True)
    hb = (x * lax.rsqrt(ms + EPS) * nmw_ref[...]).astype(BF16)

    def proj(lo, hi):
        return _dot(hb, wall_ref[:, lo:hi])

    gkl = proj(_GK0, IN_COLS_PACKED)
    z = _dot(gkl.astype(BF16), wgk_ref[...]) + bgk_ref[...]
    log_a = (jnp.minimum(z, 0.0) - jnp.log1p(jnp.exp(-jnp.abs(z)))) * (1.0 / GLA_GATE_NORMALIZER)
    la_hi, la_lo = _split_bf16(log_a)

    row = lax.broadcasted_iota(jnp.int32, (ts, ts), 0)
    col = lax.broadcasted_iota(jnp.int32, (ts, ts), 1)
    row_base = (row // GLA_CHUNK) * GLA_CHUNK
    in_chunk_le = jnp.where(col <= row, jnp.where(col >= row_base, 1.0, 0.0), 0.0)
    in_chunk_gt = jnp.where(col > row, jnp.where(col < row_base + GLA_CHUNK, 1.0, 0.0), 0.0)
    causal = in_chunk_le > 0.5
    lower = in_chunk_le.astype(BF16)
    upper = in_chunk_gt.astype(BF16)
    b_cum = _dot(lower, la_hi) + _dot(lower, la_lo)
    b_rest = _dot(upper, la_hi) + _dot(upper, la_lo)
    chunk_ind = jnp.where(
        lax.broadcasted_iota(jnp.int32, (ts, n_chunks * LANES), 0) // GLA_CHUNK
        == lax.broadcasted_iota(jnp.int32, (ts, n_chunks * LANES), 1) // LANES, 1.0, 0.0).astype(BF16)
    b_last_t = (lax.dot_general(la_hi, chunk_ind, _TN, preferred_element_type=F32)
                + lax.dot_general(la_lo, chunk_ind, _TN, preferred_element_type=F32))
    decay_t = jnp.exp(b_last_t)

    qk = proj(_Q0, _V0)
    q = qk[:, :GLA_QK]
    k = qk[:, GLA_QK:]
    q_dec = q * (GLA_DK ** -0.5) * jnp.exp(b_cum)
    k_inv = (k * jnp.exp(-b_cum)).astype(BF16)
    k_end = (k * jnp.exp(b_rest)).astype(BF16)
    vb = proj(_V0, _G0).astype(BF16)
    g = proj(_G0, _U0)

    head_of_lane = lax.broadcasted_iota(jnp.int32, (ts, GLA_QK), 1) // GLA_DK
    o_heads = []
    for h in range(GLA_HEADS):
        q_h = jnp.where(head_of_lane == h, q_dec, 0.0).astype(BF16)
        sc = lax.dot_general(q_h, k_inv, _NT, preferred_element_type=F32)
        sc = jnp.where(causal, sc, 0.0).astype(BF16)
        o_heads.append(_dot(sc, vb[:, h * GLA_DV:(h + 1) * GLA_DV]))

    state = state_ref[...]
    head_of_lane_c = lax.broadcasted_iota(jnp.int32, (GLA_CHUNK, GLA_QK), 1) // GLA_DK
    o_inter = []
    for c in range(n_chunks):
        rows = slice(c * GLA_CHUNK, (c + 1) * GLA_CHUNK)
        q_c = q_dec[rows]
        q_stack = jnp.concatenate(
            [jnp.where(head_of_lane_c == h, q_c, 0.0) for h in range(GLA_HEADS)], axis=0).astype(BF16)
        o_inter.append(_dot(q_stack, state.astype(BF16)))
        upd = lax.dot_general(k_end[rows], vb[rows], _TN, preferred_element_type=F32)
        upd = jnp.concatenate(
            [upd[h * GLA_DK:(h + 1) * GLA_DK, h * GLA_DV:(h + 1) * GLA_DV] for h in range(GLA_HEADS)], axis=0)
        state = decay_t[:, c * LANES:(c + 1) * LANES] * state + upd
    state_ref[...] = state

    gla_parts = []
    for h in range(GLA_HEADS):
        inter_h = jnp.concatenate([o_inter[c][h * GLA_CHUNK:(h + 1) * GLA_CHUNK] for c in range(n_chunks)], axis=0)
        o_h = o_heads[h] + inter_h
        ms_h = jnp.mean(o_h * o_h, axis=-1, keepdims=True)
        g_h = g[:, h * GLA_DV:(h + 1) * GLA_DV]
        gla_parts.append(o_h * lax.rsqrt(ms_h + EPS) * glanw_ref[:, h * GLA_DV:(h + 1) * GLA_DV]
                         * (g_h * jax.nn.sigmoid(g_h)))

    u = _gelu_tanh(proj(_U0, _VS0))
    vf = _gelu_tanh(proj(_VS0, _GK0))
    mu = jnp.mean(vf, axis=-1, keepdims=True)
    dv = vf - mu
    var = jnp.mean(dv * dv, axis=-1, keepdims=True)
    vn = (dv * lax.rsqrt(var + EPS) * lnw_ref[...] + lnb_ref[...]).astype(BF16)
    sg_base = (row // SG_CHUNK) * SG_CHUNK
    sg_mask = jnp.where(col <= row, jnp.where(col >= sg_base, 1.0, 0.0), 0.0) > 0.5
    sg_parts = []
    for gi in range(SG_GROUPS):
        w_c = jnp.where(sg_mask, w2_ref[gi], 0.0).astype(BF16)
        mixed_g = _dot(w_c, vn[:, gi * SG_CH:(gi + 1) * SG_CH]) + bsb_ref[gi]
        sg_parts.append(u[:, gi * SG_CH:(gi + 1) * SG_CH] * mixed_g)

    mixed = jnp.concatenate(gla_parts + sg_parts, axis=1).astype(BF16)
    x1 = x + _dot(mixed, wout_ref[...])
    x1_ref[0] = x1

    ms2 = jnp.mean(x1 * x1, axis=-1, keepdims=True)
    h2 = x1 * lax.rsqrt(ms2 + EPS) * nfw_ref[...]
    _rows_to_tiles(h2_ref, h2, ts)
    h2_hi, h2_lo = _split_bf16(h2)
    wr_hi = wrh_ref[...]
    logits = (lax.dot_general(wr_hi, h2_hi, _NT, preferred_element_type=F32)
              + lax.dot_general(wr_hi, h2_lo, _NT, preferred_element_type=F32)
              + lax.dot_general(wrl_ref[...], h2_hi, _NT, preferred_element_type=F32)
              + br_ref[...])
    e_iota = lax.broadcasted_iota(jnp.int32, (N_EXPERTS, ts), 0)
    work = logits
    top_val, top_idx = [], []
    for _ in range(TOP_K):
        m = jnp.max(work, axis=0, keepdims=True)
        i = jnp.min(jnp.where(work == m, e_iota, N_EXPERTS), axis=0, keepdims=True)
        top_val.append(m)
        top_idx.append(i)
        work = jnp.where(e_iota == i, -jnp.inf, work)
    ex = [jnp.exp(v - top_val[0]) for v in top_val]
    den = ex[0] + ex[1] + ex[2] + ex[3]
    top_w = [e / den for e in ex]

    onehot = jnp.zeros((N_EXPERTS, ts), F32)
    for i in top_idx:
        onehot = onehot + jnp.where(e_iota == i, 1.0, 0.0)
    onehot_b = onehot.astype(BF16)
    incl = jnp.where(row <= col, 1.0, 0.0).astype(BF16)
    csum = _dot(onehot_b, incl)
    total = _dot(onehot_b, jnp.ones((ts, ts), BF16))
    carry = carry_ref[...]
    before = carry + csum - onehot
    ranks = [jnp.sum(jnp.where(e_iota == i, before, 0.0), axis=0, keepdims=True).astype(jnp.int32)
             for i in top_idx]
    carry = carry + total
    carry_ref[...] = carry
    cnt_ref[...] = carry[:, :LANES]

    r8 = lax.broadcasted_iota(jnp.int32, (SUBLANES, ts), 0)

    def rows8(vals, fill):
        out = jnp.full((SUBLANES, ts), fill, vals[0].dtype)
        for kk, v in enumerate(vals):
            out = jnp.where(r8 == kk, v, out)
        return out

    idx_ref[...] = rows8(top_idx, 0)
    wts_ref[...] = rows8(top_w, 0.0)
    rank_ref[...] = rows8(ranks, 0)


def _token_mixing(x, nmw, wall, wgk, bgk, glanw, lnw, lnb, w2, bsb, wout, nfw, wrh, wrl, br):
    B, S, D = x.shape
    ts = SEQ_TILE
    ns = S // ts
    T = B * S

    def const(shape):
        return pl.BlockSpec(shape, lambda b, s: (0,) * len(shape))

    tok = lambda b, s: (0, b * ns + s)
    return pl.pallas_call(
        _mix_kernel,
        grid=(B, ns),
        in_specs=[
            pl.BlockSpec((1, ts, D), lambda b, s: (b, s, 0)),
            const((1, D)), const((D, IN_COLS_PACKED)), const((LANES, GLA_QK)), const((1, GLA_QK)),
            const((1, GLA_WIDTH)), const((1, SG_WIDTH)), const((1, SG_WIDTH)),
            const((SG_GROUPS, ts, ts)), const((SG_GROUPS, ts, SG_CH)),
            const((D, D)), const((1, D)), const((N_EXPERTS, D)), const((N_EXPERTS, D)), const((N_EXPERTS, 1)),
        ],
        out_specs=[
            pl.BlockSpec((1, ts, D), lambda b, s: (b, s, 0)),
            pl.BlockSpec((ts * ROW_TILES, LANES), lambda b, s: (b * ns + s, 0)),
            pl.BlockSpec((SUBLANES, ts), tok),
            pl.BlockSpec((SUBLANES, ts), tok),
            pl.BlockSpec((SUBLANES, ts), tok),
            pl.BlockSpec((N_EXPERTS, LANES), lambda b, s: (0, 0)),
        ],
        out_shape=[
            jax.ShapeDtypeStruct((B, S, D), F32),
            jax.ShapeDtypeStruct((T * ROW_TILES, LANES), F32),
            jax.ShapeDtypeStruct((SUBLANES, T), jnp.int32),
            jax.ShapeDtypeStruct((SUBLANES, T), F32),
            jax.ShapeDtypeStruct((SUBLANES, T), jnp.int32),
            jax.ShapeDtypeStruct((N_EXPERTS, LANES), F32),
        ],
        scratch_shapes=[
            pltpu.VMEM((GLA_QK, GLA_DV), F32),
            pltpu.VMEM((N_EXPERTS, ts), F32),
        ],
        compiler_params=pltpu.CompilerParams(
            dimension_semantics=("arbitrary", "arbitrary"), vmem_limit_bytes=VMEM_LIMIT_BYTES),
        name="token_mixing",
    )(x, nmw, wall, wgk, bgk, glanw, lnw, lnb, w2, bsb, wout, nfw, wrh, wrl, br)


def _expert_kernel(be_ref, nu_ref, src_cur_ref, src_nxt_ref, h2_hbm, wgu_ref, bgu_ref, wd_ref, bd_ref,
                   y_ref, xg_ref, wgu_b_ref, wd_b_ref, sem):
    tm = EXPERT_ROWS
    i = pl.program_id(0)
    n_used = nu_ref[0]
    slot = i % 2

    def row_copy(tok, r, slot_):
        return pltpu.make_async_copy(
            h2_hbm.at[pl.ds(pl.multiple_of(tok * ROW_TILES, ROW_TILES), ROW_TILES), :],
            xg_ref.at[slot_, pl.ds(pl.multiple_of(r * ROW_TILES, ROW_TILES), ROW_TILES), :],
            sem.at[slot_])

    def issue(src_ref, slot_):
        def body(r, carry):
            row_copy(src_ref[0, 0, r], r, slot_).start()
            return carry
        lax.fori_loop(0, tm, body, 0)

    @pl.when(i == 0)
    def _():
        issue(src_cur_ref, 0)

    @pl.when(i + 1 < n_used)
    def _():
        issue(src_nxt_ref, 1 - slot)

    @pl.when(i < n_used)
    def _():
        pltpu.make_async_copy(h2_hbm.at[pl.ds(0, tm * ROW_TILES), :], xg_ref.at[slot], sem.at[slot]).wait()

        first_of_expert = jnp.logical_or(i == 0, be_ref[i] != be_ref[jnp.maximum(i - 1, 0)])

        @pl.when(first_of_expert)
        def _():
            step = 128
            for r0 in range(0, D_MODEL, step):
                wgu_b_ref[r0:r0 + step, :] = wgu_ref[r0:r0 + step, :].astype(BF16)
                wd_b_ref[r0:r0 + step, :] = wd_ref[r0:r0 + step, :].astype(BF16)

        xb = _rows_from_tiles(xg_ref, (slot,), 0, tm).astype(BF16)
        gu = _dot(xb, wgu_b_ref[...]) + bgu_ref[...]
        gate = jnp.minimum(gu[:, :D_EXPERT], SWIGLU_LIMIT)
        up = jnp.clip(gu[:, D_EXPERT:], -SWIGLU_LIMIT, SWIGLU_LIMIT)
        act = ((up + 1.0) * (gate * jax.nn.sigmoid(SWIGLU_ALPHA * gate))).astype(BF16)
        y = _dot(act, wd_b_ref[...]) + bd_ref[...]
        _rows_to_tiles(y_ref, y, tm)

    @pl.when(i >= n_used)
    def _():
        y_ref[...] = jnp.zeros_like(y_ref)


def _experts(block_e, n_used, src_tok3, h2_rows, w_gate_up, b_gate_up, w_down, b_down):
    tm = EXPERT_ROWS
    nb = src_tok3.shape[0]
    D = D_MODEL
    grid_spec = pltpu.PrefetchScalarGridSpec(
        num_scalar_prefetch=2,
        grid=(nb,),
        in_specs=[
            pl.BlockSpec((1, 1, tm), lambda i, be, nu: (i, 0, 0), memory_space=pltpu.SMEM),
            pl.BlockSpec((1, 1, tm), lambda i, be, nu: (jnp.minimum(i + 1, nb - 1), 0, 0), memory_space=pltpu.SMEM),
            pl.BlockSpec(memory_space=pl.ANY),
            pl.BlockSpec((None, D, 2 * D_EXPERT), lambda i, be, nu: (be[i], 0, 0)),
            pl.BlockSpec((None, 1, 2 * D_EXPERT), lambda i, be, nu: (be[i], 0, 0)),
            pl.BlockSpec((None, D_EXPERT, D), lambda i, be, nu: (be[i], 0, 0)),
            pl.BlockSpec((None, 1, D), lambda i, be, nu: (be[i], 0, 0)),
        ],
        out_specs=pl.BlockSpec((tm * ROW_TILES, LANES), lambda i, be, nu: (i, 0)),
        scratch_shapes=[
            pltpu.VMEM((2, tm * ROW_TILES, LANES), F32),
            pltpu.VMEM((D, 2 * D_EXPERT), BF16),
            pltpu.VMEM((D_EXPERT, D), BF16),
            pltpu.SemaphoreType.DMA((2,)),
        ],
    )
    return pl.pallas_call(
        _expert_kernel,
        grid_spec=grid_spec,
        out_shape=jax.ShapeDtypeStruct((nb * tm * ROW_TILES, LANES), F32),
        compiler_params=pltpu.CompilerParams(
            dimension_semantics=("arbitrary",), vmem_limit_bytes=VMEM_LIMIT_BYTES),
        name="routed_experts",
    )(block_e, n_used, src_tok3, src_tok3, h2_rows, w_gate_up, b_gate_up, w_down, b_down)


def _combine_kernel(dst_cur_ref, dst_nxt_ref, y_hbm, x1_ref, w_ref, nw_ref, out_ref, yg_ref, sem):
    tt = COMBINE_TOKENS
    n_rows = TOP_K * tt
    i = pl.program_id(0)
    n_steps = pl.num_programs(0)
    slot = i % 2

    def issue(dst_ref, slot_):
        def body(r, carry):
            pltpu.make_async_copy(
                y_hbm.at[pl.ds(pl.multiple_of(dst_ref[0, 0, r] * ROW_TILES, ROW_TILES), ROW_TILES), :],
                yg_ref.at[slot_, pl.ds(pl.multiple_of(r * ROW_TILES, ROW_TILES), ROW_TILES), :],
                sem.at[slot_]).start()
            return carry
        lax.fori_loop(0, n_rows, body, 0)

    @pl.when(i == 0)
    def _():
        issue(dst_cur_ref, 0)

    @pl.when(i + 1 < n_steps)
    def _():
        issue(dst_nxt_ref, 1 - slot)

    pltpu.make_async_copy(y_hbm.at[pl.ds(0, n_rows * ROW_TILES), :], yg_ref.at[slot], sem.at[slot]).wait()

    w = w_ref[...]
    acc = x1_ref[...]
    for kk in range(TOP_K):
        y_k = _rows_from_tiles(yg_ref, (slot,), kk * tt * ROW_TILES, tt)
        acc = acc + w[:, kk:kk + 1] * y_k
    ms = jnp.mean(acc * acc, axis=-1, keepdims=True)
    out_ref[...] = acc * lax.rsqrt(ms + EPS) * nw_ref[...]


def _combine(dest3, y_rows, x1, w_tok, norm_final_w):
    T, D = x1.shape
    tt = COMBINE_TOKENS
    n_steps = T // tt
    return pl.pallas_call(
        _combine_kernel,
        grid=(n_steps,),
        in_specs=[
            pl.BlockSpec((1, 1, TOP_K * tt), lambda i: (i, 0, 0), memory_space=pltpu.SMEM),
            pl.BlockSpec((1, 1, TOP_K * tt), lambda i: (jnp.minimum(i + 1, n_steps - 1), 0, 0),
                         memory_space=pltpu.SMEM),
            pl.BlockSpec(memory_space=pl.ANY),
            pl.BlockSpec((tt, D), lambda i: (i, 0)),
            pl.BlockSpec((tt, TOP_K), lambda i: (i, 0)),
            pl.BlockSpec((1, D), lambda i: (0, 0)),
        ],
        out_specs=pl.BlockSpec((tt, D), lambda i: (i, 0)),
        out_shape=jax.ShapeDtypeStruct((T, D), F32),
        scratch_shapes=[
            pltpu.VMEM((2, TOP_K * tt * ROW_TILES, LANES), F32),
            pltpu.SemaphoreType.DMA((2,)),
        ],
        compiler_params=pltpu.CompilerParams(
            dimension_semantics=("arbitrary",), vmem_limit_bytes=VMEM_LIMIT_BYTES),
        name="combine",
    )(dest3, dest3, y_rows, x1, w_tok, norm_final_w)


def _pack_layer(w_in, w_gk_up, w_spatial, b_spatial, gla_norm_w, w_router, b_router):
    ts = SEQ_TILE
    q, k, v, g, gkl, u, vs = jnp.split(w_in, [256, 512, 1024, 1536, 1552, 2064], axis=1)
    gkl = jnp.pad(gkl, ((0, 0), (0, LANES - GLA_GATE_RANK)))
    wall = jnp.concatenate([q, k, v, g, u, vs, gkl], axis=1).astype(BF16)
    wgk = jnp.pad(w_gk_up, ((0, LANES - GLA_GATE_RANK), (0, 0))).astype(BF16)
    reps = ts // SG_CHUNK
    w2 = jnp.tile(w_spatial, (1, reps, reps)).astype(BF16)
    bsb = jnp.broadcast_to(jnp.tile(b_spatial, (1, reps))[:, :, None], (SG_GROUPS, ts, SG_CH))
    glanw = jnp.tile(gla_norm_w, GLA_HEADS)[None, :]
    wr_t = w_router.T
    wr_hi = wr_t.astype(BF16)
    wr_lo = (wr_t - wr_hi.astype(F32)).astype(BF16)
    return wall, wgk, w2, bsb, glanw, wr_hi, wr_lo, b_router[:, None]


def kernel(x, norm_mix_w, w_in, w_gk_up, b_gk, gla_norm_w, sg_ln_w, sg_ln_b, w_spatial, b_spatial, w_out,
           norm_ffn_w, w_router, b_router, w_gate_up, b_gate_up, w_down, b_down, norm_final_w):
    B, S, D = x.shape
    T = B * S
    assert w_in.shape[0] == 1
    assert D == D_MODEL and S % SEQ_TILE == 0 and T % COMBINE_TOKENS == 0 and (T * TOP_K) % EXPERT_ROWS == 0
    tm = EXPERT_ROWS
    tt = COMBINE_TOKENS
    nb = T * TOP_K // tm + N_EXPERTS
    for l in range(1):
        wall, wgk, w2, bsb, glanw, wr_hi, wr_lo, br = _pack_layer(
            w_in[l], w_gk_up[l], w_spatial[l], b_spatial[l], gla_norm_w[l], w_router[l], b_router[l])
        x1, h2_rows, idx8, wts8, rank8, cnt = _token_mixing(
            x, norm_mix_w[l][None, :], wall, wgk, b_gk[l][None, :], glanw, sg_ln_w[l][None, :],
            sg_ln_b[l][None, :], w2, bsb, w_out[l].astype(BF16), norm_ffn_w[l][None, :], wr_hi, wr_lo, br)

        counts = cnt[:, 0].astype(jnp.int32)
        padded = (counts + tm - 1) // tm * tm
        padded_ends = jnp.cumsum(padded)
        padded_starts = padded_ends - padded
        idx4 = idx8[:TOP_K]
        dest4 = padded_starts[idx4] + rank8[:TOP_K]
        tok4 = jnp.broadcast_to(jnp.arange(T, dtype=jnp.int32)[None, :], (TOP_K, T))
        src_tok = jnp.zeros((nb * tm,), jnp.int32).at[dest4.reshape(-1)].set(tok4.reshape(-1))
        block_e = jnp.minimum(
            jnp.searchsorted(padded_ends, jnp.arange(nb, dtype=jnp.int32) * tm, side='right'),
            N_EXPERTS - 1).astype(jnp.int32)
        n_used = (padded_ends[-1:] // tm).astype(jnp.int32)
        dest3 = dest4.reshape(TOP_K, T // tt, tt).transpose(1, 0, 2).reshape(T // tt, 1, TOP_K * tt)
        w_tok = wts8[:TOP_K].T

        y_rows = _experts(block_e, n_used, src_tok.reshape(nb, 1, tm), h2_rows,
                          w_gate_up[l], b_gate_up[l][:, None, :], w_down[l], b_down[l][:, None, :])
        out = _combine(dest3, y_rows, x1.reshape(T, D), w_tok, norm_final_w[None, :])
        x = out.reshape(B, S, D)
    return x
```

```python
import jax
import jax.numpy as jnp
from jax import lax
from jax.experimental import pallas as pl
from jax.experimental.pallas import tpu as pltpu

D_MODEL = 1024
GLA_HEADS = 4
GLA_DK = 64
GLA_DV = 128
GLA_QK = GLA_HEADS * GLA_DK
GLA_WIDTH = GLA_HEADS * GLA_DV
GLA_GATE_RANK = 16
GLA_GATE_NORMALIZER = 16.0
GLA_CHUNK = 64
SG_GROUPS = 4
SG_CH = 128
SG_WIDTH = SG_GROUPS * SG_CH
SG_CHUNK = 128
N_EXPERTS = 32
TOP_K = 4
D_EXPERT = D_MODEL
SWIGLU_LIMIT = 7.0
SWIGLU_ALPHA = 1.702
EPS = 1e-6

SUBLANES = 8
LANES = 128
ROW_TILES = D_MODEL // LANES
assert ROW_TILES == SUBLANES

SEQ_TILE = 256
EXPERT_ROWS = 256
RUN_CHUNK = 16
LOCAL_ROWS = 1536
assert SEQ_TILE * TOP_K + N_EXPERTS * (RUN_CHUNK - 1) <= LOCAL_ROWS
VMEM_LIMIT_BYTES = 56 * 1024 * 1024

_Q0, _K0, _V0, _G0, _U0, _VS0, _GK0 = 0, 256, 512, 1024, 1536, 2048, 2560
IN_COLS_PACKED = 2688

F32 = jnp.float32
BF16 = jnp.bfloat16
_NT = (((1,), (1,)), ((), ()))
_TN = (((0,), (0,)), ((), ()))


def _dot(a, b):
    return jnp.dot(a, b, preferred_element_type=F32)


def _split_bf16(a):
    hi = a.astype(BF16)
    lo = (a - hi.astype(F32)).astype(BF16)
    return hi, lo


def _gelu_tanh(a):
    return 0.5 * a * (1.0 + jnp.tanh(0.7978845608028654 * (a + 0.044715 * (a * a * a))))


def _rows_from_tiles(ref, lead, base, n_rows):
    parts = [ref[(*lead, pl.ds(base + j, n_rows, stride=ROW_TILES), slice(None))] for j in range(ROW_TILES)]
    return jnp.concatenate(parts, axis=1)


def _rows_to_tiles(ref, val, n_rows):
    for j in range(ROW_TILES):
        ref[pl.ds(j, n_rows, stride=ROW_TILES), :] = val[:, j * LANES:(j + 1) * LANES]


def _mix_kernel(x_ref, nmw_ref, wall_ref, wgk_ref, bgk_ref, glanw_ref, lnw_ref, lnb_ref, w2_ref, bsb_ref,
                wout_ref, nfw_ref, wrh_ref, wrl_ref, br_ref,
                x1_ref, h2_ref, lpos_ref, wts_ref, cnt_ref,
                state_ref):
    ts = SEQ_TILE
    n_chunks = ts // GLA_CHUNK
    s_id = pl.program_id(1)

    @pl.when(s_id == 0)
    def _():
        state_ref[...] = jnp.zeros_like(state_ref)

    x = x_ref[0]
    ms = jnp.mean(x * x, axis=-1, keepdims=True)
    hb = (x * lax.rsqrt(ms + EPS) * nmw_ref[...]).astype(BF16)

    def proj(lo, hi):
        return _dot(hb, wall_ref[:, lo:hi])

    gkl = proj(_GK0, IN_COLS_PACKED)
    z = _dot(gkl.astype(BF16), wgk_ref[...]) + bgk_ref[...]
    log_a = (jnp.minimum(z, 0.0) - jnp.log1p(jnp.exp(-jnp.abs(z)))) * (1.0 / GLA_GATE_NORMALIZER)
    la_hi, la_lo = _split_bf16(log_a)

    row = lax.broadcasted_iota(jnp.int32, (ts, ts), 0)
    col = lax.broadcasted_iota(jnp.int32, (ts, ts), 1)
    row_base = (row // GLA_CHUNK) * GLA_CHUNK
    in_chunk_le = jnp.where(col <= row, jnp.where(col >= row_base, 1.0, 0.0), 0.0)
    in_chunk_gt = jnp.where(col > row, jnp.where(col < row_base + GLA_CHUNK, 1.0, 0.0), 0.0)
    causal = in_chunk_le > 0.5
    lower = in_chunk_le.astype(BF16)
    upper = in_chunk_gt.astype(BF16)
    b_cum = _dot(lower, la_hi) + _dot(lower, la_lo)
    b_rest = _dot(upper, la_hi) + _dot(upper, la_lo)
    chunk_ind = jnp.where(
        lax.broadcasted_iota(jnp.int32, (ts, n_chunks * LANES), 0) // GLA_CHUNK
        == lax.broadcasted_iota(jnp.int32, (ts, n_chunks * LANES), 1) // LANES, 1.0, 0.0).astype(BF16)
    b_last_t = (lax.dot_general(la_hi, chunk_ind, _TN, preferred_element_type=F32)
                + lax.dot_general(la_lo, chunk_ind, _TN, preferred_element_type=F32))
    decay_t = jnp.exp(b_last_t)

    qk = proj(_Q0, _V0)
    q = qk[:, :GLA_QK]
    k = qk[:, GLA_QK:]
    q_dec = q * (GLA_DK ** -0.5) * jnp.exp(b_cum)
    k_inv = (k * jnp.exp(-b_cum)).astype(BF16)
    k_end = (k * jnp.exp(b_rest)).astype(BF16)
    vb = proj(_V0, _G0).astype(BF16)
    g = proj(_G0, _U0)

    head_of_lane = lax.broadcasted_iota(jnp.int32, (ts, GLA_QK), 1) // GLA_DK
    o_heads = []
    for h in range(GLA_HEADS):
        q_h = jnp.where(head_of_lane == h, q_dec, 0.0).astype(BF16)
        sc = lax.dot_general(q_h, k_inv, _NT, preferred_element_type=F32)
        sc = jnp.where(causal, sc, 0.0).astype(BF16)
        o_heads.append(_dot(sc, vb[:, h * GLA_DV:(h + 1) * GLA_DV]))

    state = state_ref[...]
    head_of_lane_c = lax.broadcasted_iota(jnp.int32, (GLA_CHUNK, GLA_QK), 1) // GLA_DK
    o_inter = []
    for c in range(n_chunks):
        rows = slice(c * GLA_CHUNK, (c + 1) * GLA_CHUNK)
        q_c = q_dec[rows]
        q_stack = jnp.concatenate(
            [jnp.where(head_of_lane_c == h, q_c, 0.0) for h in range(GLA_HEADS)], axis=0).astype(BF16)
        o_inter.append(_dot(q_stack, state.astype(BF16)))
        upd = lax.dot_general(k_end[rows], vb[rows], _TN, preferred_element_type=F32)
        upd = jnp.concatenate(
            [upd[h * GLA_DK:(h + 1) * GLA_DK, h * GLA_DV:(h + 1) * GLA_DV] for h in range(GLA_HEADS)], axis=0)
        state = decay_t[:, c * LANES:(c + 1) * LANES] * state + upd
    state_ref[...] = state

    gla_parts = []
    for h in range(GLA_HEADS):
        inter_h = jnp.concatenate([o_inter[c][h * GLA_CHUNK:(h + 1) * GLA_CHUNK] for c in range(n_chunks)], axis=0)
        o_h = o_heads[h] + inter_h
        ms_h = jnp.mean(o_h * o_h, axis=-1, keepdims=True)
        g_h = g[:, h * GLA_DV:(h + 1) * GLA_DV]
        gla_parts.append(o_h * lax.rsqrt(ms_h + EPS) * glanw_ref[:, h * GLA_DV:(h + 1) * GLA_DV]
                         * (g_h * jax.nn.sigmoid(g_h)))

    u = _gelu_tanh(proj(_U0, _VS0))
    vf = _gelu_tanh(proj(_VS0, _GK0))
    mu = jnp.mean(vf, axis=-1, keepdims=True)
    dv = vf - mu
    var = jnp.mean(dv * dv, axis=-1, keepdims=True)
    vn = (dv * lax.rsqrt(var + EPS) * lnw_ref[...] + lnb_ref[...]).astype(BF16)
    sg_base = (row // SG_CHUNK) * SG_CHUNK
    sg_mask = jnp.where(col <= row, jnp.where(col >= sg_base, 1.0, 0.0), 0.0) > 0.5
    sg_parts = []
    for gi in range(SG_GROUPS):
        w_c = jnp.where(sg_mask, w2_ref[gi], 0.0).astype(BF16)
        mixed_g = _dot(w_c, vn[:, gi * SG_CH:(gi + 1) * SG_CH]) + bsb_ref[gi]
        sg_parts.append(u[:, gi * SG_CH:(gi + 1) * SG_CH] * mixed_g)

    mixed = jnp.concatenate(gla_parts + sg_parts, axis=1).astype(BF16)
    x1 = x + _dot(mixed, wout_ref[...])
    x1_ref[0] = x1

    ms2 = jnp.mean(x1 * x1, axis=-1, keepdims=True)
    h2 = x1 * lax.rsqrt(ms2 + EPS) * nfw_ref[...]
    h2_hi, h2_lo = _split_bf16(h2)
    h2_ref[...] = h2_hi
    wr_hi = wrh_ref[...]
    logits = (lax.dot_general(wr_hi, h2_hi, _NT, preferred_element_type=F32)
              + lax.dot_general(wr_hi, h2_lo, _NT, preferred_element_type=F32)
              + lax.dot_general(wrl_ref[...], h2_hi, _NT, preferred_element_type=F32)
              + br_ref[...])
    e_iota = lax.broadcasted_iota(jnp.int32, (N_EXPERTS, ts), 0)
    work = logits
    top_val, top_idx = [], []
    for _ in range(TOP_K):
        m = jnp.max(work, axis=0, keepdims=True)
        i = jnp.min(jnp.where(work == m, e_iota, N_EXPERTS), axis=0, keepdims=True)
        top_val.append(m)
        top_idx.append(i)
        work = jnp.where(e_iota == i, -jnp.inf, work)
    ex = [jnp.exp(v - top_val[0]) for v in top_val]
    den = ex[0] + ex[1] + ex[2] + ex[3]
    top_w = [e / den for e in ex]

    onehot = jnp.zeros((N_EXPERTS, ts), F32)
    for i in top_idx:
        onehot = onehot + jnp.where(e_iota == i, 1.0, 0.0)
    onehot_b = onehot.astype(BF16)
    incl = jnp.where(row <= col, 1.0, 0.0).astype(BF16)
    csum = _dot(onehot_b, incl)
    total = _dot(onehot_b, jnp.ones((ts, ts), BF16))
    n_chunks_e = jnp.floor((total + (RUN_CHUNK - 1)) * (1.0 / RUN_CHUNK))
    e_lower = jnp.where(lax.broadcasted_iota(jnp.int32, (N_EXPERTS, N_EXPERTS), 1)
                        < lax.broadcasted_iota(jnp.int32, (N_EXPERTS, N_EXPERTS), 0), 1.0, 0.0).astype(BF16)
    local_start = RUN_CHUNK * _dot(e_lower, n_chunks_e.astype(BF16))
    local_pos = local_start + csum - onehot
    lpos = [jnp.sum(jnp.where(e_iota == i, local_pos, 0.0), axis=0, keepdims=True).astype(jnp.int32)
            for i in top_idx]
    cnt_ref[0] = total[:, :LANES]

    r8 = lax.broadcasted_iota(jnp.int32, (SUBLANES, ts), 0)

    def rows8(vals, fill):
        out = jnp.full((SUBLANES, ts), fill, vals[0].dtype)
        for kk, v in enumerate(vals):
            out = jnp.where(r8 == kk, v, out)
        return out

    lpos_ref[...] = rows8(lpos, -1)
    wts_ref[...] = rows8(top_w, 0.0)


def _token_mixing(x, nmw, wall, wgk, bgk, glanw, lnw, lnb, w2, bsb, wout, nfw, wrh, wrl, br):
    B, S, D = x.shape
    ts = SEQ_TILE
    ns = S // ts
    T = B * S

    def const(shape):
        return pl.BlockSpec(shape, lambda b, s: (0,) * len(shape))

    tok = lambda b, s: (0, b * ns + s)
    return pl.pallas_call(
        _mix_kernel,
        grid=(B, ns),
        in_specs=[
            pl.BlockSpec((1, ts, D), lambda b, s: (b, s, 0)),
            const((1, D)), const((D, IN_COLS_PACKED)), const((LANES, GLA_QK)), const((1, GLA_QK)),
            const((1, GLA_WIDTH)), const((1, SG_WIDTH)), const((1, SG_WIDTH)),
            const((SG_GROUPS, ts, ts)), const((SG_GROUPS, ts, SG_CH)),
            const((D, D)), const((1, D)), const((N_EXPERTS, D)), const((N_EXPERTS, D)), const((N_EXPERTS, 1)),
        ],
        out_specs=[
            pl.BlockSpec((1, ts, D), lambda b, s: (b, s, 0)),
            pl.BlockSpec((ts, D), lambda b, s: (b * ns + s, 0)),
            pl.BlockSpec((SUBLANES, ts), tok),
            pl.BlockSpec((SUBLANES, ts), tok),
            pl.BlockSpec((1, N_EXPERTS, LANES), lambda b, s: (b * ns + s, 0, 0)),
        ],
        out_shape=[
            jax.ShapeDtypeStruct((B, S, D), F32),
            jax.ShapeDtypeStruct((T, D), BF16),
            jax.ShapeDtypeStruct((SUBLANES, T), jnp.int32),
            jax.ShapeDtypeStruct((SUBLANES, T), F32),
            jax.ShapeDtypeStruct((T // ts, N_EXPERTS, LANES), F32),
        ],
        scratch_shapes=[pltpu.VMEM((GLA_QK, GLA_DV), F32)],
        compiler_params=pltpu.CompilerParams(
            dimension_semantics=("arbitrary", "arbitrary"), vmem_limit_bytes=VMEM_LIMIT_BYTES),
        name="token_mixing",
    )(x, nmw, wall, wgk, bgk, glanw, lnw, lnb, w2, bsb, wout, nfw, wrh, wrl, br)


def _chunk_rows(start_row):
    return pl.ds(pl.multiple_of(start_row * ROW_TILES, ROW_TILES), RUN_CHUNK * ROW_TILES)


def _wait_chunks(hbm_ref, sem, n_chunks):
    @pl.when(n_chunks > 0)
    def _():
        rows = pl.ds(0, n_chunks * (RUN_CHUNK * ROW_TILES))
        pltpu.make_async_copy(hbm_ref.at[rows, :], hbm_ref.at[rows, :], sem).wait()


def _one_hot_positions(lpos_ref, values):
    ts = SEQ_TILE
    p_iota = lax.broadcasted_iota(jnp.int32, (LOCAL_ROWS, ts), 0)
    out = jnp.zeros((LOCAL_ROWS, ts), F32)
    for kk in range(TOP_K):
        out = jnp.where(p_iota == lpos_ref[kk:kk + 1, :], values[kk], out)
    return out.astype(BF16)


def _dispatch_kernel(goff_ref, nch_ref, loff_ref, ntot_ref, zst_ref, znc_ref,
                     h2_ref, lpos_ref, xbuf_hbm, srt_ref, zero_ref, sem, zsem):
    s = pl.program_id(0)
    n_steps = pl.num_programs(0)
    slot = s % 2

    @pl.when(s == 0)
    def _():
        zero_ref[...] = jnp.zeros_like(zero_ref)

        def per_expert(e, total):
            def per_chunk(c, carry):
                pltpu.make_async_copy(zero_ref, xbuf_hbm.at[_chunk_rows(zst_ref[e] + c * RUN_CHUNK), :], zsem).start()
                return carry
            lax.fori_loop(0, znc_ref[e], per_chunk, 0)
            return total + znc_ref[e]
        n_zero = lax.fori_loop(0, N_EXPERTS + 1, per_expert, 0)
        _wait_chunks(xbuf_hbm, zsem, n_zero)

    ones = [1.0] * TOP_K
    perm = _one_hot_positions(lpos_ref, ones)
    local_sorted = _dot(perm, h2_ref[...])
    for j in range(ROW_TILES):
        srt_ref[slot, pl.ds(j, LOCAL_ROWS, stride=ROW_TILES), :] = local_sorted[:, j * LANES:(j + 1) * LANES]

    @pl.when(s > 0)
    def _():
        _wait_chunks(xbuf_hbm, sem.at[1 - slot], ntot_ref[s - 1])

    def per_expert(e, carry):
        g0 = goff_ref[s * N_EXPERTS + e]
        l0 = loff_ref[s * N_EXPERTS + e]

        def per_chunk(c, carry2):
            pltpu.make_async_copy(srt_ref.at[slot, _chunk_rows(l0 + c * RUN_CHUNK), :],
                                  xbuf_hbm.at[_chunk_rows(g0 + c * RUN_CHUNK), :], sem.at[slot]).start()
            return carry2
        lax.fori_loop(0, nch_ref[s * N_EXPERTS + e], per_chunk, 0)
        return carry
    lax.fori_loop(0, N_EXPERTS, per_expert, 0)

    @pl.when(s == n_steps - 1)
    def _():
        _wait_chunks(xbuf_hbm, sem.at[slot], ntot_ref[s])


def _dispatch(tables, h2b, lpos8, n_rows_buf):
    T, D = h2b.shape
    ts = SEQ_TILE
    n_tiles = T // ts
    grid_spec = pltpu.PrefetchScalarGridSpec(
        num_scalar_prefetch=len(tables),
        grid=(n_tiles,),
        in_specs=[
            pl.BlockSpec((ts, D), lambda s, *_: (s, 0)),
            pl.BlockSpec((SUBLANES, ts), lambda s, *_: (0, s)),
        ],
        out_specs=pl.BlockSpec(memory_space=pl.ANY),
        scratch_shapes=[
            pltpu.VMEM((2, LOCAL_ROWS * ROW_TILES, LANES), F32),
            pltpu.VMEM((RUN_CHUNK * ROW_TILES, LANES), F32),
            pltpu.SemaphoreType.DMA((2,)),
            pltpu.SemaphoreType.DMA(()),
        ],
    )
    return pl.pallas_call(
        _dispatch_kernel,
        grid_spec=grid_spec,
        out_shape=jax.ShapeDtypeStruct((n_rows_buf * ROW_TILES, LANES), F32),
        compiler_params=pltpu.CompilerParams(
            dimension_semantics=("arbitrary",), vmem_limit_bytes=VMEM_LIMIT_BYTES),
        name="dispatch",
    )(*tables, h2b, lpos8)


def _expert_kernel(be_ref, nu_ref, x_ref, wgu_ref, bgu_ref, wd_ref, bd_ref, y_ref, wgu_b_ref, wd_b_ref):
    tm = EXPERT_ROWS
    i = pl.program_id(0)
    n_used = nu_ref[0]

    @pl.when(i < n_used)
    def _():
        first_of_expert = jnp.logical_or(i == 0, be_ref[i] != be_ref[jnp.maximum(i - 1, 0)])

        @pl.when(first_of_expert)
        def _():
            step = 128
            for r0 in range(0, D_MODEL, step):
                wgu_b_ref[r0:r0 + step, :] = wgu_ref[r0:r0 + step, :].astype(BF16)
                wd_b_ref[r0:r0 + step, :] = wd_ref[r0:r0 + step, :].astype(BF16)

        xb = _rows_from_tiles(x_ref, (), 0, tm).astype(BF16)
        gu = _dot(xb, wgu_b_ref[...]) + bgu_ref[...]
        gate = jnp.minimum(gu[:, :D_EXPERT], SWIGLU_LIMIT)
        up = jnp.clip(gu[:, D_EXPERT:], -SWIGLU_LIMIT, SWIGLU_LIMIT)
        act = ((up + 1.0) * (gate * jax.nn.sigmoid(SWIGLU_ALPHA * gate))).astype(BF16)
        y = _dot(act, wd_b_ref[...]) + bd_ref[...]
        _rows_to_tiles(y_ref, y, tm)

    @pl.when(i >= n_used)
    def _():
        y_ref[...] = jnp.zeros_like(y_ref)


def _experts(block_e, n_used, nb, x_rows, w_gate_up, b_gate_up, w_down, b_down):
    tm = EXPERT_ROWS
    D = D_MODEL
    grid_spec = pltpu.PrefetchScalarGridSpec(
        num_scalar_prefetch=2,
        grid=(nb,),
        in_specs=[
            pl.BlockSpec((tm * ROW_TILES, LANES), lambda i, be, nu: (jnp.minimum(i, nu[0] - 1), 0)),
            pl.BlockSpec((None, D, 2 * D_EXPERT), lambda i, be, nu: (be[i], 0, 0)),
            pl.BlockSpec((None, 1, 2 * D_EXPERT), lambda i, be, nu: (be[i], 0, 0)),
            pl.BlockSpec((None, D_EXPERT, D), lambda i, be, nu: (be[i], 0, 0)),
            pl.BlockSpec((None, 1, D), lambda i, be, nu: (be[i], 0, 0)),
        ],
        out_specs=pl.BlockSpec((tm * ROW_TILES, LANES), lambda i, be, nu: (i, 0)),
        scratch_shapes=[
            pltpu.VMEM((D, 2 * D_EXPERT), BF16),
            pltpu.VMEM((D_EXPERT, D), BF16),
        ],
    )
    return pl.pallas_call(
        _expert_kernel,
        grid_spec=grid_spec,
        out_shape=jax.ShapeDtypeStruct((nb * tm * ROW_TILES, LANES), F32),
        compiler_params=pltpu.CompilerParams(
            dimension_semantics=("arbitrary",), vmem_limit_bytes=VMEM_LIMIT_BYTES),
        name="routed_experts",
    )(block_e, n_used, x_rows, w_gate_up, b_gate_up, w_down, b_down)


def _combine_kernel(goff_ref, nch_ref, loff_ref, ntot_ref,
                    lpos_ref, wts_ref, y_hbm, x1_ref, nw_ref, out_ref, ysrt_ref, sem):
    s = pl.program_id(0)
    n_steps = pl.num_programs(0)
    slot = s % 2

    def issue(tile, slot_):
        def per_expert(e, carry):
            g0 = goff_ref[tile * N_EXPERTS + e]
            l0 = loff_ref[tile * N_EXPERTS + e]

            def per_chunk(c, carry2):
                pltpu.make_async_copy(y_hbm.at[_chunk_rows(g0 + c * RUN_CHUNK), :],
                                      ysrt_ref.at[slot_, _chunk_rows(l0 + c * RUN_CHUNK), :], sem.at[slot_]).start()
                return carry2
            lax.fori_loop(0, nch_ref[tile * N_EXPERTS + e], per_chunk, 0)
            return carry
        lax.fori_loop(0, N_EXPERTS, per_expert, 0)

    @pl.when(s == 0)
    def _():
        ysrt_ref[...] = jnp.zeros_like(ysrt_ref)
        issue(0, 0)

    @pl.when(s + 1 < n_steps)
    def _():
        issue(s + 1, 1 - slot)

    _wait_chunks(y_hbm, sem.at[slot], ntot_ref[s])

    weights = [wts_ref[kk:kk + 1, :] for kk in range(TOP_K)]
    perm_w = _one_hot_positions(lpos_ref, weights)
    y_local = _rows_from_tiles(ysrt_ref, (slot,), 0, LOCAL_ROWS).astype(BF16)
    acc = x1_ref[...] + lax.dot_general(perm_w, y_local, _TN, preferred_element_type=F32)
    ms = jnp.mean(acc * acc, axis=-1, keepdims=True)
    out_ref[...] = acc * lax.rsqrt(ms + EPS) * nw_ref[...]


def _combine(tables, lpos8, wts8, y_rows, x1, norm_final_w):
    T, D = x1.shape
    ts = SEQ_TILE
    n_tiles = T // ts
    grid_spec = pltpu.PrefetchScalarGridSpec(
        num_scalar_prefetch=len(tables),
        grid=(n_tiles,),
        in_specs=[
            pl.BlockSpec((SUBLANES, ts), lambda s, *_: (0, s)),
            pl.BlockSpec((SUBLANES, ts), lambda s, *_: (0, s)),
            pl.BlockSpec(memory_space=pl.ANY),
            pl.BlockSpec((ts, D), lambda s, *_: (s, 0)),
            pl.BlockSpec((1, D), lambda s, *_: (0, 0)),
        ],
        out_specs=pl.BlockSpec((ts, D), lambda s, *_: (s, 0)),
        scratch_shapes=[
            pltpu.VMEM((2, LOCAL_ROWS * ROW_TILES, LANES), F32),
            pltpu.SemaphoreType.DMA((2,)),
        ],
    )
    return pl.pallas_call(
        _combine_kernel,
        grid_spec=grid_spec,
        out_shape=jax.ShapeDtypeStruct((T, D), F32),
        compiler_params=pltpu.CompilerParams(
            dimension_semantics=("arbitrary",), vmem_limit_bytes=VMEM_LIMIT_BYTES),
        name="combine",
    )(*tables, lpos8, wts8, y_rows, x1, norm_final_w)


def _pack_layer(w_in, w_gk_up, w_spatial, b_spatial, gla_norm_w, w_router, b_router):
    ts = SEQ_TILE
    q, k, v, g, gkl, u, vs = jnp.split(w_in, [256, 512, 1024, 1536, 1552, 2064], axis=1)
    gkl = jnp.pad(gkl, ((0, 0), (0, LANES - GLA_GATE_RANK)))
    wall = jnp.concatenate([q, k, v, g, u, vs, gkl], axis=1).astype(BF16)
    wgk = jnp.pad(w_gk_up, ((0, LANES - GLA_GATE_RANK), (0, 0))).astype(BF16)
    reps = ts // SG_CHUNK
    w2 = jnp.tile(w_spatial, (1, reps, reps)).astype(BF16)
    bsb = jnp.broadcast_to(jnp.tile(b_spatial, (1, reps))[:, :, None], (SG_GROUPS, ts, SG_CH))
    glanw = jnp.tile(gla_norm_w, GLA_HEADS)[None, :]
    wr_t = w_router.T
    wr_hi = wr_t.astype(BF16)
    wr_lo = (wr_t - wr_hi.astype(F32)).astype(BF16)
    return wall, wgk, w2, bsb, glanw, wr_hi, wr_lo, b_router[:, None]


def _routing_tables(tile_counts, nb):
    tm = EXPERT_ROWS
    n = tile_counts
    counts = jnp.sum(n, axis=0)
    padded = (counts + (RUN_CHUNK - 1) + tm - 1) // tm * tm
    padded_ends = jnp.cumsum(padded)
    padded_starts = padded_ends - padded
    goff = padded_starts[None, :] + jnp.cumsum(n, axis=0) - n
    nch = (n + RUN_CHUNK - 1) // RUN_CHUNK
    loff = RUN_CHUNK * (jnp.cumsum(nch, axis=1) - nch)
    ntot = jnp.sum(nch, axis=1)
    zero_start = jnp.concatenate([(padded_starts + counts) // RUN_CHUNK * RUN_CHUNK, padded_ends[-1:]])
    zero_end = jnp.concatenate([padded_ends, jnp.full((1,), nb * tm, padded_ends.dtype)])
    zero_chunks = (zero_end - zero_start) // RUN_CHUNK
    block_start = jnp.arange(nb, dtype=jnp.int32) * tm
    block_e = jnp.minimum(jnp.sum((block_start[:, None] >= padded_ends[None, :]).astype(jnp.int32), axis=1),
                          N_EXPERTS - 1)
    n_used = padded_ends[-1:] // tm
    i32 = lambda a: a.astype(jnp.int32)
    run_tables = (i32(goff.reshape(-1)), i32(nch.reshape(-1)), i32(loff.reshape(-1)), i32(ntot))
    return run_tables, (i32(zero_start), i32(zero_chunks)), i32(block_e), i32(n_used)


def kernel(x, norm_mix_w, w_in, w_gk_up, b_gk, gla_norm_w, sg_ln_w, sg_ln_b, w_spatial, b_spatial, w_out,
           norm_ffn_w, w_router, b_router, w_gate_up, b_gate_up, w_down, b_down, norm_final_w):
    B, S, D = x.shape
    T = B * S
    assert w_in.shape[0] == 1
    assert D == D_MODEL and S % SEQ_TILE == 0 and (T * TOP_K) % EXPERT_ROWS == 0
    tm = EXPERT_ROWS
    spare_blocks = -(-(N_EXPERTS * (RUN_CHUNK - 1)) // tm)
    nb = T * TOP_K // tm + N_EXPERTS + spare_blocks
    l = 0
    wall, wgk, w2, bsb, glanw, wr_hi, wr_lo, br = _pack_layer(
        w_in[l], w_gk_up[l], w_spatial[l], b_spatial[l], gla_norm_w[l], w_router[l], b_router[l])
    x1, h2b, lpos8, wts8, cnt = _token_mixing(
        x, norm_mix_w[l][None, :], wall, wgk, b_gk[l][None, :], glanw, sg_ln_w[l][None, :],
        sg_ln_b[l][None, :], w2, bsb, w_out[l].astype(BF16), norm_ffn_w[l][None, :], wr_hi, wr_lo, br)

    run_tables, zero_tables, block_e, n_used = _routing_tables(cnt[:, :, 0].astype(jnp.int32), nb)
    x_rows = _dispatch(run_tables + zero_tables, h2b, lpos8, nb * tm)
    y_rows = _experts(block_e, n_used, nb, x_rows, w_gate_up[l], b_gate_up[l][:, None, :],
                      w_down[l], b_down[l][:, None, :])
    out = _combine(run_tables, lpos8, wts8, y_rows, x1.reshape(T, D), norm_final_w[None, :])
    return out.reshape(B, S, D)
```

```python
import jax
import jax.numpy as jnp
from jax import lax
from jax.experimental import pallas as pl
from jax.experimental.pallas import tpu as pltpu

D_MODEL = 1024
GLA_HEADS = 4
GLA_DK = 64
GLA_DV = 128
GLA_QK = GLA_HEADS * GLA_DK
GLA_WIDTH = GLA_HEADS * GLA_DV
GLA_GATE_RANK = 16
GLA_GATE_NORMALIZER = 16.0
GLA_CHUNK = 64
SG_GROUPS = 4
SG_CH = 128
SG_WIDTH = SG_GROUPS * SG_CH
SG_CHUNK = 128
N_EXPERTS = 32
TOP_K = 4
D_EXPERT = D_MODEL
SWIGLU_LIMIT = 7.0
SWIGLU_ALPHA = 1.702
EPS = 1e-6

SUBLANES = 8
LANES = 128
ROW_TILES = D_MODEL // LANES
assert ROW_TILES == SUBLANES

SEQ_TILE = 256
EXPERT_ROWS = 512
RUN_CHUNK = 16
LOCAL_ROWS = 1536
assert SEQ_TILE * TOP_K + N_EXPERTS * (RUN_CHUNK - 1) <= LOCAL_ROWS
VMEM_LIMIT_BYTES = 56 * 1024 * 1024

_Q0, _K0, _V0, _G0, _U0, _VS0, _GK0 = 0, 256, 512, 1024, 1536, 2048, 2560
IN_COLS_PACKED = 2688

F32 = jnp.float32
BF16 = jnp.bfloat16
_NT = (((1,), (1,)), ((), ()))
_TN = (((0,), (0,)), ((), ()))


def _dot(a, b):
    return jnp.dot(a, b, preferred_element_type=F32)


def _split_bf16(a):
    hi = a.astype(BF16)
    lo = (a - hi.astype(F32)).astype(BF16)
    return hi, lo


def _gelu_tanh(a):
    return 0.5 * a * (1.0 + jnp.tanh(0.7978845608028654 * (a + 0.044715 * (a * a * a))))


def _rows_from_tiles(ref, lead, base, n_rows):
    parts = [ref[(*lead, pl.ds(base + j, n_rows, stride=ROW_TILES), slice(None))] for j in range(ROW_TILES)]
    return jnp.concatenate(parts, axis=1)


def _rows_to_tiles(ref, val, n_rows):
    for j in range(ROW_TILES):
        ref[pl.ds(j, n_rows, stride=ROW_TILES), :] = val[:, j * LANES:(j + 1) * LANES]


def _mix_kernel(x_ref, nmw_ref, wall_ref, wgk_ref, bgk_ref, glanw_ref, lnw_ref, lnb_ref, w2_ref, bsb_ref,
                wout_ref, nfw_ref, wrh_ref, wrl_ref, br_ref,
                x1_ref, h2_ref, lpos_ref, wts_ref, cnt_ref,
                state_ref):
    ts = SEQ_TILE
    n_chunks = ts // GLA_CHUNK
    s_id = pl.program_id(1)

    @pl.when(s_id == 0)
    def _():
        state_ref[...] = jnp.zeros_like(state_ref)

    x = x_ref[0]
    ms = jnp.mean(x * x, axis=-1, keepdims=True)
    hb = (x * lax.rsqrt(ms + EPS) * nmw_ref[...]).astype(BF16)

    def proj(lo, hi):
        return _dot(hb, wall_ref[:, lo:hi])

    gkl = proj(_GK0, IN_COLS_PACKED)
    z = _dot(gkl.astype(BF16), wgk_ref[...]) + bgk_ref[...]
    log_a = (jnp.minimum(z, 0.0) - jnp.log1p(jnp.exp(-jnp.abs(z)))) * (1.0 / GLA_GATE_NORMALIZER)
    la_hi, la_lo = _split_bf16(log_a)

    row = lax.broadcasted_iota(jnp.int32, (ts, ts), 0)
    col = lax.broadcasted_iota(jnp.int32, (ts, ts), 1)
    row_base = (row // GLA_CHUNK) * GLA_CHUNK
    in_chunk_le = jnp.where(col <= row, jnp.where(col >= row_base, 1.0, 0.0), 0.0)
    in_chunk_gt = jnp.where(col > row, jnp.where(col < row_base + GLA_CHUNK, 1.0, 0.0), 0.0)
    causal = in_chunk_le > 0.5
    lower = in_chunk_le.astype(BF16)
    upper = in_chunk_gt.astype(BF16)
    b_cum = _dot(lower, la_hi) + _dot(lower, la_lo)
    b_rest = _dot(upper, la_hi) + _dot(upper, la_lo)
    chunk_ind = jnp.where(
        lax.broadcasted_iota(jnp.int32, (ts, n_chunks * LANES), 0) // GLA_CHUNK
        == lax.broadcasted_iota(jnp.int32, (ts, n_chunks * LANES), 1) // LANES, 1.0, 0.0).astype(BF16)
    b_last_t = (lax.dot_general(la_hi, chunk_ind, _TN, preferred_element_type=F32)
                + lax.dot_general(la_lo, chunk_ind, _TN, preferred_element_type=F32))
    decay_t = jnp.exp(b_last_t)

    qk = proj(_Q0, _V0)
    q = qk[:, :GLA_QK]
    k = qk[:, GLA_QK:]
    q_dec = q * (GLA_DK ** -0.5) * jnp.exp(b_cum)
    k_inv = (k * jnp.exp(-b_cum)).astype(BF16)
    k_end = (k * jnp.exp(b_rest)).astype(BF16)
    vb = proj(_V0, _G0).astype(BF16)
    g = proj(_G0, _U0)

    head_of_lane = lax.broadcasted_iota(jnp.int32, (ts, GLA_QK), 1) // GLA_DK
    o_heads = []
    for h in range(GLA_HEADS):
        q_h = jnp.where(head_of_lane == h, q_dec, 0.0).astype(BF16)
        sc = lax.dot_general(q_h, k_inv, _NT, preferred_element_type=F32)
        sc = jnp.where(causal, sc, 0.0).astype(BF16)
        o_heads.append(_dot(sc, vb[:, h * GLA_DV:(h + 1) * GLA_DV]))

    state = state_ref[...]
    head_of_lane_c = lax.broadcasted_iota(jnp.int32, (GLA_CHUNK, GLA_QK), 1) // GLA_DK
    o_inter = []
    for c in range(n_chunks):
        rows = slice(c * GLA_CHUNK, (c + 1) * GLA_CHUNK)
        q_c = q_dec[rows]
        q_stack = jnp.concatenate(
            [jnp.where(head_of_lane_c == h, q_c, 0.0) for h in range(GLA_HEADS)], axis=0).astype(BF16)
        o_inter.append(_dot(q_stack, state.astype(BF16)))
        upd = lax.dot_general(k_end[rows], vb[rows], _TN, preferred_element_type=F32)
        upd = jnp.concatenate(
            [upd[h * GLA_DK:(h + 1) * GLA_DK, h * GLA_DV:(h + 1) * GLA_DV] for h in range(GLA_HEADS)], axis=0)
        state = decay_t[:, c * LANES:(c + 1) * LANES] * state + upd
    state_ref[...] = state

    gla_parts = []
    for h in range(GLA_HEADS):
        inter_h = jnp.concatenate([o_inter[c][h * GLA_CHUNK:(h + 1) * GLA_CHUNK] for c in range(n_chunks)], axis=0)
        o_h = o_heads[h] + inter_h
        ms_h = jnp.mean(o_h * o_h, axis=-1, keepdims=True)
        g_h = g[:, h * GLA_DV:(h + 1) * GLA_DV]
        gla_parts.append(o_h * lax.rsqrt(ms_h + EPS) * glanw_ref[:, h * GLA_DV:(h + 1) * GLA_DV]
                         * (g_h * jax.nn.sigmoid(g_h)))

    u = _gelu_tanh(proj(_U0, _VS0))
    vf = _gelu_tanh(proj(_VS0, _GK0))
    mu = jnp.mean(vf, axis=-1, keepdims=True)
    dv = vf - mu
    var = jnp.mean(dv * dv, axis=-1, keepdims=True)
    vn = (dv * lax.rsqrt(var + EPS) * lnw_ref[...] + lnb_ref[...]).astype(BF16)
    sg_base = (row // SG_CHUNK) * SG_CHUNK
    sg_mask = jnp.where(col <= row, jnp.where(col >= sg_base, 1.0, 0.0), 0.0) > 0.5
    sg_parts = []
    for gi in range(SG_GROUPS):
        w_c = jnp.where(sg_mask, w2_ref[gi], 0.0).astype(BF16)
        mixed_g = _dot(w_c, vn[:, gi * SG_CH:(gi + 1) * SG_CH]) + bsb_ref[gi]
        sg_parts.append(u[:, gi * SG_CH:(gi + 1) * SG_CH] * mixed_g)

    mixed = jnp.concatenate(gla_parts + sg_parts, axis=1).astype(BF16)
    x1 = x + _dot(mixed, wout_ref[...])
    x1_ref[0] = x1

    ms2 = jnp.mean(x1 * x1, axis=-1, keepdims=True)
    h2 = x1 * lax.rsqrt(ms2 + EPS) * nfw_ref[...]
    h2_hi, h2_lo = _split_bf16(h2)
    h2_ref[...] = h2_hi
    wr_hi = wrh_ref[...]
    logits = (lax.dot_general(wr_hi, h2_hi, _NT, preferred_element_type=F32)
              + lax.dot_general(wr_hi, h2_lo, _NT, preferred_element_type=F32)
              + lax.dot_general(wrl_ref[...], h2_hi, _NT, preferred_element_type=F32)
              + br_ref[...])
    e_iota = lax.broadcasted_iota(jnp.int32, (N_EXPERTS, ts), 0)
    work = logits
    top_val, top_idx = [], []
    for _ in range(TOP_K):
        m = jnp.max(work, axis=0, keepdims=True)
        i = jnp.min(jnp.where(work == m, e_iota, N_EXPERTS), axis=0, keepdims=True)
        top_val.append(m)
        top_idx.append(i)
        work = jnp.where(e_iota == i, -jnp.inf, work)
    ex = [jnp.exp(v - top_val[0]) for v in top_val]
    den = ex[0] + ex[1] + ex[2] + ex[3]
    top_w = [e / den for e in ex]

    onehot = jnp.zeros((N_EXPERTS, ts), F32)
    for i in top_idx:
        onehot = onehot + jnp.where(e_iota == i, 1.0, 0.0)
    onehot_b = onehot.astype(BF16)
    incl = jnp.where(row <= col, 1.0, 0.0).astype(BF16)
    csum = _dot(onehot_b, incl)
    total = _dot(onehot_b, jnp.ones((ts, ts), BF16))
    n_chunks_e = jnp.floor((total + (RUN_CHUNK - 1)) * (1.0 / RUN_CHUNK))
    e_lower = jnp.where(lax.broadcasted_iota(jnp.int32, (N_EXPERTS, N_EXPERTS), 1)
                        < lax.broadcasted_iota(jnp.int32, (N_EXPERTS, N_EXPERTS), 0), 1.0, 0.0).astype(BF16)
    local_start = RUN_CHUNK * _dot(e_lower, n_chunks_e.astype(BF16))
    local_pos = local_start + csum - onehot
    lpos = [jnp.sum(jnp.where(e_iota == i, local_pos, 0.0), axis=0, keepdims=True).astype(jnp.int32)
            for i in top_idx]
    cnt_ref[0] = total[:, :LANES]

    r8 = lax.broadcasted_iota(jnp.int32, (SUBLANES, ts), 0)

    def rows8(vals, fill):
        out = jnp.full((SUBLANES, ts), fill, vals[0].dtype)
        for kk, v in enumerate(vals):
            out = jnp.where(r8 == kk, v, out)
        return out

    lpos_ref[...] = rows8(lpos, -1)
    wts_ref[...] = rows8(top_w, 0.0)


def _token_mixing(x, nmw, wall, wgk, bgk, glanw, lnw, lnb, w2, bsb, wout, nfw, wrh, wrl, br):
    B, S, D = x.shape
    ts = SEQ_TILE
    ns = S // ts
    T = B * S

    def const(shape):
        return pl.BlockSpec(shape, lambda b, s: (0,) * len(shape))

    tok = lambda b, s: (0, b * ns + s)
    return pl.pallas_call(
        _mix_kernel,
        grid=(B, ns),
        in_specs=[
            pl.BlockSpec((1, ts, D), lambda b, s: (b, s, 0)),
            const((1, D)), const((D, IN_COLS_PACKED)), const((LANES, GLA_QK)), const((1, GLA_QK)),
            const((1, GLA_WIDTH)), const((1, SG_WIDTH)), const((1, SG_WIDTH)),
            const((SG_GROUPS, ts, ts)), const((SG_GROUPS, ts, SG_CH)),
            const((D, D)), const((1, D)), const((N_EXPERTS, D)), const((N_EXPERTS, D)), const((N_EXPERTS, 1)),
        ],
        out_specs=[
            pl.BlockSpec((1, ts, D), lambda b, s: (b, s, 0)),
            pl.BlockSpec((ts, D), lambda b, s: (b * ns + s, 0)),
            pl.BlockSpec((SUBLANES, ts), tok),
            pl.BlockSpec((SUBLANES, ts), tok),
            pl.BlockSpec((1, N_EXPERTS, LANES), lambda b, s: (b * ns + s, 0, 0)),
        ],
        out_shape=[
            jax.ShapeDtypeStruct((B, S, D), F32),
            jax.ShapeDtypeStruct((T, D), BF16),
            jax.ShapeDtypeStruct((SUBLANES, T), jnp.int32),
            jax.ShapeDtypeStruct((SUBLANES, T), F32),
            jax.ShapeDtypeStruct((T // ts, N_EXPERTS, LANES), F32),
        ],
        scratch_shapes=[pltpu.VMEM((GLA_QK, GLA_DV), F32)],
        compiler_params=pltpu.CompilerParams(
            dimension_semantics=("arbitrary", "arbitrary"), vmem_limit_bytes=VMEM_LIMIT_BYTES),
        name="token_mixing",
    )(x, nmw, wall, wgk, bgk, glanw, lnw, lnb, w2, bsb, wout, nfw, wrh, wrl, br)


def _chunk_rows(start_row):
    return pl.ds(pl.multiple_of(start_row * ROW_TILES, ROW_TILES), RUN_CHUNK * ROW_TILES)


def _wait_chunks(hbm_ref, sem, n_chunks):
    @pl.when(n_chunks > 0)
    def _():
        rows = pl.ds(0, n_chunks * (RUN_CHUNK * ROW_TILES))
        pltpu.make_async_copy(hbm_ref.at[rows, :], hbm_ref.at[rows, :], sem).wait()


def _one_hot_positions(lpos_ref, values):
    ts = SEQ_TILE
    p_iota = lax.broadcasted_iota(jnp.int32, (LOCAL_ROWS, ts), 0)
    out = jnp.zeros((LOCAL_ROWS, ts), F32)
    for kk in range(TOP_K):
        out = jnp.where(p_iota == lpos_ref[kk:kk + 1, :], values[kk], out)
    return out.astype(BF16)


def _dispatch_kernel(goff_ref, nch_ref, loff_ref, ntot_ref, zst_ref, znc_ref,
                     h2_ref, lpos_ref, xbuf_hbm, srt_ref, zero_ref, sem, zsem):
    s = pl.program_id(0)
    n_steps = pl.num_programs(0)
    slot = s % 2

    @pl.when(s == 0)
    def _():
        zero_ref[...] = jnp.zeros_like(zero_ref)

        def per_expert(e, total):
            def per_chunk(c, carry):
                pltpu.make_async_copy(zero_ref, xbuf_hbm.at[_chunk_rows(zst_ref[e] + c * RUN_CHUNK), :], zsem).start()
                return carry
            lax.fori_loop(0, znc_ref[e], per_chunk, 0)
            return total + znc_ref[e]
        n_zero = lax.fori_loop(0, N_EXPERTS + 1, per_expert, 0)
        _wait_chunks(xbuf_hbm, zsem, n_zero)

    ones = [1.0] * TOP_K
    perm = _one_hot_positions(lpos_ref, ones)
    local_sorted = _dot(perm, h2_ref[...])
    for j in range(ROW_TILES):
        srt_ref[slot, pl.ds(j, LOCAL_ROWS, stride=ROW_TILES), :] = local_sorted[:, j * LANES:(j + 1) * LANES]

    @pl.when(s > 0)
    def _():
        _wait_chunks(xbuf_hbm, sem.at[1 - slot], ntot_ref[s - 1])

    def per_expert(e, carry):
        g0 = goff_ref[s * N_EXPERTS + e]
        l0 = loff_ref[s * N_EXPERTS + e]

        def per_chunk(c, carry2):
            pltpu.make_async_copy(srt_ref.at[slot, _chunk_rows(l0 + c * RUN_CHUNK), :],
                                  xbuf_hbm.at[_chunk_rows(g0 + c * RUN_CHUNK), :], sem.at[slot]).start()
            return carry2
        lax.fori_loop(0, nch_ref[s * N_EXPERTS + e], per_chunk, 0)
        return carry
    lax.fori_loop(0, N_EXPERTS, per_expert, 0)

    @pl.when(s == n_steps - 1)
    def _():
        _wait_chunks(xbuf_hbm, sem.at[slot], ntot_ref[s])


def _dispatch(tables, h2b, lpos8, n_rows_buf):
    T, D = h2b.shape
    ts = SEQ_TILE
    n_tiles = T // ts
    grid_spec = pltpu.PrefetchScalarGridSpec(
        num_scalar_prefetch=len(tables),
        grid=(n_tiles,),
        in_specs=[
            pl.BlockSpec((ts, D), lambda s, *_: (s, 0)),
            pl.BlockSpec((SUBLANES, ts), lambda s, *_: (0, s)),
        ],
        out_specs=pl.BlockSpec(memory_space=pl.ANY),
        scratch_shapes=[
            pltpu.VMEM((2, LOCAL_ROWS * ROW_TILES, LANES), F32),
            pltpu.VMEM((RUN_CHUNK * ROW_TILES, LANES), F32),
            pltpu.SemaphoreType.DMA((2,)),
            pltpu.SemaphoreType.DMA(()),
        ],
    )
    return pl.pallas_call(
        _dispatch_kernel,
        grid_spec=grid_spec,
        out_shape=jax.ShapeDtypeStruct((n_rows_buf * ROW_TILES, LANES), F32),
        compiler_params=pltpu.CompilerParams(
            dimension_semantics=("arbitrary",), vmem_limit_bytes=VMEM_LIMIT_BYTES),
        name="dispatch",
    )(*tables, h2b, lpos8)


def _expert_kernel(be_ref, nu_ref, x_ref, wgu_hbm, bgu_ref, wd_hbm, bd_ref, y_ref,
                   wgu_f_ref, wd_f_ref, wgu_b_ref, wd_b_ref, wsem):
    tm = EXPERT_ROWS
    i = pl.program_id(0)
    n_used = nu_ref[0]
    e = be_ref[i]

    def weight_copies(expert, slot_):
        return (pltpu.make_async_copy(wgu_hbm.at[expert], wgu_f_ref.at[slot_], wsem.at[0, slot_]),
                pltpu.make_async_copy(wd_hbm.at[expert], wd_f_ref.at[slot_], wsem.at[1, slot_]))

    @pl.when(i == 0)
    def _():
        for cp in weight_copies(e, e % 2):
            cp.start()

    @pl.when(i < n_used)
    def _():
        first_of_expert = jnp.logical_or(i == 0, e != be_ref[jnp.maximum(i - 1, 0)])

        @pl.when(first_of_expert)
        def _():
            slot = e % 2
            for cp in weight_copies(e, slot):
                cp.wait()

            @pl.when(e + 1 < N_EXPERTS)
            def _():
                for cp in weight_copies(e + 1, 1 - slot):
                    cp.start()

            step = 128
            for r0 in range(0, D_MODEL, step):
                wgu_b_ref[r0:r0 + step, :] = wgu_f_ref[slot, r0:r0 + step, :].astype(BF16)
                wd_b_ref[r0:r0 + step, :] = wd_f_ref[slot, r0:r0 + step, :].astype(BF16)

        xb = _rows_from_tiles(x_ref, (), 0, tm).astype(BF16)
        gu = _dot(xb, wgu_b_ref[...]) + bgu_ref[...]
        gate = jnp.minimum(gu[:, :D_EXPERT], SWIGLU_LIMIT)
        up = jnp.clip(gu[:, D_EXPERT:], -SWIGLU_LIMIT, SWIGLU_LIMIT)
        act = ((up + 1.0) * (gate * jax.nn.sigmoid(SWIGLU_ALPHA * gate))).astype(BF16)
        y = _dot(act, wd_b_ref[...]) + bd_ref[...]
        _rows_to_tiles(y_ref, y, tm)

    @pl.when(i >= n_used)
    def _():
        y_ref[...] = jnp.zeros_like(y_ref)


def _experts(block_e, n_used, nb, x_rows, w_gate_up, b_gate_up, w_down, b_down):
    tm = EXPERT_ROWS
    D = D_MODEL
    grid_spec = pltpu.PrefetchScalarGridSpec(
        num_scalar_prefetch=2,
        grid=(nb,),
        in_specs=[
            pl.BlockSpec((tm * ROW_TILES, LANES), lambda i, be, nu: (jnp.minimum(i, nu[0] - 1), 0)),
            pl.BlockSpec(memory_space=pl.ANY),
            pl.BlockSpec((None, 1, 2 * D_EXPERT), lambda i, be, nu: (be[i], 0, 0)),
            pl.BlockSpec(memory_space=pl.ANY),
            pl.BlockSpec((None, 1, D), lambda i, be, nu: (be[i], 0, 0)),
        ],
        out_specs=pl.BlockSpec((tm * ROW_TILES, LANES), lambda i, be, nu: (i, 0)),
        scratch_shapes=[
            pltpu.VMEM((2, D, 2 * D_EXPERT), F32),
            pltpu.VMEM((2, D_EXPERT, D), F32),
            pltpu.VMEM((D, 2 * D_EXPERT), BF16),
            pltpu.VMEM((D_EXPERT, D), BF16),
            pltpu.SemaphoreType.DMA((2, 2)),
        ],
    )
    return pl.pallas_call(
        _expert_kernel,
        grid_spec=grid_spec,
        out_shape=jax.ShapeDtypeStruct((nb * tm * ROW_TILES, LANES), F32),
        compiler_params=pltpu.CompilerParams(
            dimension_semantics=("arbitrary",), vmem_limit_bytes=VMEM_LIMIT_BYTES),
        name="routed_experts",
    )(block_e, n_used, x_rows, w_gate_up, b_gate_up, w_down, b_down)


def _combine_kernel(goff_ref, nch_ref, loff_ref, ntot_ref,
                    lpos_ref, wts_ref, y_hbm, x1_ref, nw_ref, out_ref, ysrt_ref, sem):
    s = pl.program_id(0)
    n_steps = pl.num_programs(0)
    slot = s % 2

    def issue(tile, slot_):
        def per_expert(e, carry):
            g0 = goff_ref[tile * N_EXPERTS + e]
            l0 = loff_ref[tile * N_EXPERTS + e]

            def per_chunk(c, carry2):
                pltpu.make_async_copy(y_hbm.at[_chunk_rows(g0 + c * RUN_CHUNK), :],
                                      ysrt_ref.at[slot_, _chunk_rows(l0 + c * RUN_CHUNK), :], sem.at[slot_]).start()
                return carry2
            lax.fori_loop(0, nch_ref[tile * N_EXPERTS + e], per_chunk, 0)
            return carry
        lax.fori_loop(0, N_EXPERTS, per_expert, 0)

    @pl.when(s == 0)
    def _():
        ysrt_ref[...] = jnp.zeros_like(ysrt_ref)
        issue(0, 0)

    @pl.when(s + 1 < n_steps)
    def _():
        issue(s + 1, 1 - slot)

    _wait_chunks(y_hbm, sem.at[slot], ntot_ref[s])

    weights = [wts_ref[kk:kk + 1, :] for kk in range(TOP_K)]
    perm_w = _one_hot_positions(lpos_ref, weights)
    y_local = _rows_from_tiles(ysrt_ref, (slot,), 0, LOCAL_ROWS).astype(BF16)
    acc = x1_ref[...] + lax.dot_general(perm_w, y_local, _TN, preferred_element_type=F32)
    ms = jnp.mean(acc * acc, axis=-1, keepdims=True)
    out_ref[...] = acc * lax.rsqrt(ms + EPS) * nw_ref[...]


def _combine(tables, lpos8, wts8, y_rows, x1, norm_final_w):
    T, D = x1.shape
    ts = SEQ_TILE
    n_tiles = T // ts
    grid_spec = pltpu.PrefetchScalarGridSpec(
        num_scalar_prefetch=len(tables),
        grid=(n_tiles,),
        in_specs=[
            pl.BlockSpec((SUBLANES, ts), lambda s, *_: (0, s)),
            pl.BlockSpec((SUBLANES, ts), lambda s, *_: (0, s)),
            pl.BlockSpec(memory_space=pl.ANY),
            pl.BlockSpec((ts, D), lambda s, *_: (s, 0)),
            pl.BlockSpec((1, D), lambda s, *_: (0, 0)),
        ],
        out_specs=pl.BlockSpec((ts, D), lambda s, *_: (s, 0)),
        scratch_shapes=[
            pltpu.VMEM((2, LOCAL_ROWS * ROW_TILES, LANES), F32),
            pltpu.SemaphoreType.DMA((2,)),
        ],
    )
    return pl.pallas_call(
        _combine_kernel,
        grid_spec=grid_spec,
        out_shape=jax.ShapeDtypeStruct((T, D), F32),
        compiler_params=pltpu.CompilerParams(
            dimension_semantics=("arbitrary",), vmem_limit_bytes=VMEM_LIMIT_BYTES),
        name="combine",
    )(*tables, lpos8, wts8, y_rows, x1, norm_final_w)


def _pack_layer(w_in, w_gk_up, w_spatial, b_spatial, gla_norm_w, w_router, b_router):
    ts = SEQ_TILE
    q, k, v, g, gkl, u, vs = jnp.split(w_in, [256, 512, 1024, 1536, 1552, 2064], axis=1)
    gkl = jnp.pad(gkl, ((0, 0), (0, LANES - GLA_GATE_RANK)))
    wall = jnp.concatenate([q, k, v, g, u, vs, gkl], axis=1).astype(BF16)
    wgk = jnp.pad(w_gk_up, ((0, LANES - GLA_GATE_RANK), (0, 0))).astype(BF16)
    reps = ts // SG_CHUNK
    w2 = jnp.tile(w_spatial, (1, reps, reps)).astype(BF16)
    bsb = jnp.broadcast_to(jnp.tile(b_spatial, (1, reps))[:, :, None], (SG_GROUPS, ts, SG_CH))
    glanw = jnp.tile(gla_norm_w, GLA_HEADS)[None, :]
    wr_t = w_router.T
    wr_hi = wr_t.astype(BF16)
    wr_lo = (wr_t - wr_hi.astype(F32)).astype(BF16)
    return wall, wgk, w2, bsb, glanw, wr_hi, wr_lo, b_router[:, None]


def _routing_tables(tile_counts, nb):
    tm = EXPERT_ROWS
    n = tile_counts
    counts = jnp.sum(n, axis=0)
    padded = (counts + (RUN_CHUNK - 1) + tm - 1) // tm * tm
    padded_ends = jnp.cumsum(padded)
    padded_starts = padded_ends - padded
    goff = padded_starts[None, :] + jnp.cumsum(n, axis=0) - n
    nch = (n + RUN_CHUNK - 1) // RUN_CHUNK
    loff = RUN_CHUNK * (jnp.cumsum(nch, axis=1) - nch)
    ntot = jnp.sum(nch, axis=1)
    zero_start = jnp.concatenate([(padded_starts + counts) // RUN_CHUNK * RUN_CHUNK, padded_ends[-1:]])
    zero_end = jnp.concatenate([padded_ends, jnp.full((1,), nb * tm, padded_ends.dtype)])
    zero_chunks = (zero_end - zero_start) // RUN_CHUNK
    block_start = jnp.arange(nb, dtype=jnp.int32) * tm
    block_e = jnp.minimum(jnp.sum((block_start[:, None] >= padded_ends[None, :]).astype(jnp.int32), axis=1),
                          N_EXPERTS - 1)
    n_used = padded_ends[-1:] // tm
    i32 = lambda a: a.astype(jnp.int32)
    run_tables = (i32(goff.reshape(-1)), i32(nch.reshape(-1)), i32(loff.reshape(-1)), i32(ntot))
    return run_tables, (i32(zero_start), i32(zero_chunks)), i32(block_e), i32(n_used)


def kernel(x, norm_mix_w, w_in, w_gk_up, b_gk, gla_norm_w, sg_ln_w, sg_ln_b, w_spatial, b_spatial, w_out,
           norm_ffn_w, w_router, b_router, w_gate_up, b_gate_up, w_down, b_down, norm_final_w):
    B, S, D = x.shape
    T = B * S
    assert w_in.shape[0] == 1
    assert D == D_MODEL and S % SEQ_TILE == 0 and (T * TOP_K) % EXPERT_ROWS == 0
    tm = EXPERT_ROWS
    spare_blocks = -(-(N_EXPERTS * (RUN_CHUNK - 1)) // tm)
    nb = T * TOP_K // tm + N_EXPERTS + spare_blocks
    l = 0
    wall, wgk, w2, bsb, glanw, wr_hi, wr_lo, br = _pack_layer(
        w_in[l], w_gk_up[l], w_spatial[l], b_spatial[l], gla_norm_w[l], w_router[l], b_router[l])
    x1, h2b, lpos8, wts8, cnt = _token_mixing(
        x, norm_mix_w[l][None, :], wall, wgk, b_gk[l][None, :], glanw, sg_ln_w[l][None, :],
        sg_ln_b[l][None, :], w2, bsb, w_out[l].astype(BF16), norm_ffn_w[l][None, :], wr_hi, wr_lo, br)

    run_tables, zero_tables, block_e, n_used = _routing_tables(cnt[:, :, 0].astype(jnp.int32), nb)
    x_rows = _dispatch(run_tables + zero_tables, h2b, lpos8, nb * tm)
    y_rows = _experts(block_e, n_used, nb, x_rows, w_gate_up[l], b_gate_up[l][:, None, :],
                      w_down[l], b_down[l][:, None, :])
    out = _combine(run_tables, lpos8, wts8, y_rows, x1.reshape(T, D), norm_final_w[None, :])
    return out.reshape(B, S, D)
```

```python
import jax
import jax.numpy as jnp
from jax import lax
from jax.experimental import pallas as pl
from jax.experimental.pallas import tpu as pltpu

D_MODEL = 1024
GLA_HEADS = 4
GLA_DK = 64
GLA_DV = 128
GLA_QK = GLA_HEADS * GLA_DK
GLA_WIDTH = GLA_HEADS * GLA_DV
GLA_GATE_RANK = 16
GLA_GATE_NORMALIZER = 16.0
GLA_CHUNK = 64
SG_GROUPS = 4
SG_CH = 128
SG_WIDTH = SG_GROUPS * SG_CH
SG_CHUNK = 128
N_EXPERTS = 32
TOP_K = 4
D_EXPERT = D_MODEL
SWIGLU_LIMIT = 7.0
SWIGLU_ALPHA = 1.702
EPS = 1e-6

SUBLANES = 8
LANES = 128
ROW_TILES = D_MODEL // LANES
assert ROW_TILES == SUBLANES

SEQ_TILE = 256
MIX_BATCH_ROWS = 2
EXPERT_ROWS = 512
RUN_CHUNK = 16
LOCAL_ROWS = 1536
assert SEQ_TILE * TOP_K + N_EXPERTS * (RUN_CHUNK - 1) <= LOCAL_ROWS
VMEM_LIMIT_BYTES = 56 * 1024 * 1024

_Q0, _K0, _V0, _G0, _U0, _VS0, _GK0 = 0, 256, 512, 1024, 1536, 2048, 2560
IN_COLS_PACKED = 2688

F32 = jnp.float32
BF16 = jnp.bfloat16
_NT = (((1,), (1,)), ((), ()))
_TN = (((0,), (0,)), ((), ()))


def _dot(a, b):
    return jnp.dot(a, b, preferred_element_type=F32)


def _split_bf16(a):
    hi = a.astype(BF16)
    lo = (a - hi.astype(F32)).astype(BF16)
    return hi, lo


def _gelu_tanh(a):
    return 0.5 * a * (1.0 + jnp.tanh(0.7978845608028654 * (a + 0.044715 * (a * a * a))))


def _rows_from_tiles(ref, lead, n_rows):
    parts = [ref[(*lead, pl.ds(j, n_rows, stride=ROW_TILES), slice(None))] for j in range(ROW_TILES)]
    return jnp.concatenate(parts, axis=1)


def _rows_to_tiles(ref, lead, val, n_rows):
    for j in range(ROW_TILES):
        ref[(*lead, pl.ds(j, n_rows, stride=ROW_TILES), slice(None))] = val[:, j * LANES:(j + 1) * LANES]


_DONE = object()


def _mix_kernel(*refs):
    state_ref = refs[-1]

    @pl.when(pl.program_id(1) == 0)
    def _():
        state_ref[...] = jnp.zeros_like(state_ref)

    tiles = [_mix_tile(ti, *refs) for ti in range(MIX_BATCH_ROWS)]
    while tiles:
        tiles = [t for t in tiles if next(t, _DONE) is not _DONE]


def _mix_tile(ti, x_ref, nmw_ref, wall_ref, wgk_ref, bgk_ref, glanw_ref, lnw_ref, lnb_ref, w2_ref, bsb_ref,
              wout_ref, nfw_ref, wrh_ref, wrl_ref, br_ref,
              x1_ref, h2_ref, lpos_ref, wts_ref, cnt_ref,
              state_ref):
    ts = SEQ_TILE
    n_chunks = ts // GLA_CHUNK

    x = x_ref[ti]
    ms = jnp.mean(x * x, axis=-1, keepdims=True)
    hb = (x * lax.rsqrt(ms + EPS) * nmw_ref[...]).astype(BF16)
    yield

    def proj(lo, hi):
        return _dot(hb, wall_ref[:, lo:hi])

    gkl = proj(_GK0, IN_COLS_PACKED)
    z = _dot(gkl.astype(BF16), wgk_ref[...]) + bgk_ref[...]
    log_a = (jnp.minimum(z, 0.0) - jnp.log1p(jnp.exp(-jnp.abs(z)))) * (1.0 / GLA_GATE_NORMALIZER)
    la_hi, la_lo = _split_bf16(log_a)
    yield

    row = lax.broadcasted_iota(jnp.int32, (ts, ts), 0)
    col = lax.broadcasted_iota(jnp.int32, (ts, ts), 1)
    row_base = (row // GLA_CHUNK) * GLA_CHUNK
    in_chunk_le = jnp.where(col <= row, jnp.where(col >= row_base, 1.0, 0.0), 0.0)
    in_chunk_gt = jnp.where(col > row, jnp.where(col < row_base + GLA_CHUNK, 1.0, 0.0), 0.0)
    causal = in_chunk_le > 0.5
    lower = in_chunk_le.astype(BF16)
    upper = in_chunk_gt.astype(BF16)
    b_cum = _dot(lower, la_hi) + _dot(lower, la_lo)
    b_rest = _dot(upper, la_hi) + _dot(upper, la_lo)
    yield
    chunk_ind = jnp.where(
        lax.broadcasted_iota(jnp.int32, (ts, n_chunks * LANES), 0) // GLA_CHUNK
        == lax.broadcasted_iota(jnp.int32, (ts, n_chunks * LANES), 1) // LANES, 1.0, 0.0).astype(BF16)
    b_last_t = (lax.dot_general(la_hi, chunk_ind, _TN, preferred_element_type=F32)
                + lax.dot_general(la_lo, chunk_ind, _TN, preferred_element_type=F32))
    decay_t = jnp.exp(b_last_t)
    yield

    qk = proj(_Q0, _V0)
    q = qk[:, :GLA_QK]
    k = qk[:, GLA_QK:]
    q_dec = q * (GLA_DK ** -0.5) * jnp.exp(b_cum)
    k_inv = (k * jnp.exp(-b_cum)).astype(BF16)
    k_end = (k * jnp.exp(b_rest)).astype(BF16)
    yield
    vb = proj(_V0, _G0).astype(BF16)
    g = proj(_G0, _U0)
    yield

    head_of_lane = lax.broadcasted_iota(jnp.int32, (ts, GLA_QK), 1) // GLA_DK
    o_heads = []
    for h in range(GLA_HEADS):
        q_h = jnp.where(head_of_lane == h, q_dec, 0.0).astype(BF16)
        sc = lax.dot_general(q_h, k_inv, _NT, preferred_element_type=F32)
        sc = jnp.where(causal, sc, 0.0).astype(BF16)
        o_heads.append(_dot(sc, vb[:, h * GLA_DV:(h + 1) * GLA_DV]))
        yield

    state = state_ref[ti]
    head_of_lane_c = lax.broadcasted_iota(jnp.int32, (GLA_CHUNK, GLA_QK), 1) // GLA_DK
    o_inter = []
    for c in range(n_chunks):
        rows = slice(c * GLA_CHUNK, (c + 1) * GLA_CHUNK)
        q_c = q_dec[rows]
        q_stack = jnp.concatenate(
            [jnp.where(head_of_lane_c == h, q_c, 0.0) for h in range(GLA_HEADS)], axis=0).astype(BF16)
        o_inter.append(_dot(q_stack, state.astype(BF16)))
        upd = lax.dot_general(k_end[rows], vb[rows], _TN, preferred_element_type=F32)
        upd = jnp.concatenate(
            [upd[h * GLA_DK:(h + 1) * GLA_DK, h * GLA_DV:(h + 1) * GLA_DV] for h in range(GLA_HEADS)], axis=0)
        state = decay_t[:, c * LANES:(c + 1) * LANES] * state + upd
        yield
    state_ref[ti] = state

    gla_parts = []
    for h in range(GLA_HEADS):
        inter_h = jnp.concatenate([o_inter[c][h * GLA_CHUNK:(h + 1) * GLA_CHUNK] for c in range(n_chunks)], axis=0)
        o_h = o_heads[h] + inter_h
        ms_h = jnp.mean(o_h * o_h, axis=-1, keepdims=True)
        g_h = g[:, h * GLA_DV:(h + 1) * GLA_DV]
        gla_parts.append(o_h * lax.rsqrt(ms_h + EPS) * glanw_ref[:, h * GLA_DV:(h + 1) * GLA_DV]
                         * (g_h * jax.nn.sigmoid(g_h)))
        yield

    u = _gelu_tanh(proj(_U0, _VS0))
    vf = _gelu_tanh(proj(_VS0, _GK0))
    mu = jnp.mean(vf, axis=-1, keepdims=True)
    dv = vf - mu
    var = jnp.mean(dv * dv, axis=-1, keepdims=True)
    vn = (dv * lax.rsqrt(var + EPS) * lnw_ref[...] + lnb_ref[...]).astype(BF16)
    yield
    sg_base = (row // SG_CHUNK) * SG_CHUNK
    sg_mask = jnp.where(col <= row, jnp.where(col >= sg_base, 1.0, 0.0), 0.0) > 0.5
    sg_parts = []
    for gi in range(SG_GROUPS):
        w_c = jnp.where(sg_mask, w2_ref[gi], 0.0).astype(BF16)
        mixed_g = _dot(w_c, vn[:, gi * SG_CH:(gi + 1) * SG_CH]) + bsb_ref[gi]
        sg_parts.append(u[:, gi * SG_CH:(gi + 1) * SG_CH] * mixed_g)
        yield

    mixed = jnp.concatenate(gla_parts + sg_parts, axis=1).astype(BF16)
    x1 = x + _dot(mixed, wout_ref[...])
    x1_ref[ti] = x1
    yield

    ms2 = jnp.mean(x1 * x1, axis=-1, keepdims=True)
    h2 = x1 * lax.rsqrt(ms2 + EPS) * nfw_ref[...]
    h2_hi, h2_lo = _split_bf16(h2)
    h2_ref[ti] = h2_hi
    yield
    wr_hi = wrh_ref[...]
    logits = (lax.dot_general(wr_hi, h2_hi, _NT, preferred_element_type=F32)
              + lax.dot_general(wr_hi, h2_lo, _NT, preferred_element_type=F32)
              + lax.dot_general(wrl_ref[...], h2_hi, _NT, preferred_element_type=F32)
              + br_ref[...])
    yield
    e_iota = lax.broadcasted_iota(jnp.int32, (N_EXPERTS, ts), 0)
    work = logits
    top_val, top_idx = [], []
    for _ in range(TOP_K):
        m = jnp.max(work, axis=0, keepdims=True)
        i = jnp.min(jnp.where(work == m, e_iota, N_EXPERTS), axis=0, keepdims=True)
        top_val.append(m)
        top_idx.append(i)
        work = jnp.where(e_iota == i, -jnp.inf, work)
        yield
    ex = [jnp.exp(v - top_val[0]) for v in top_val]
    den = ex[0] + ex[1] + ex[2] + ex[3]
    top_w = [e / den for e in ex]
    yield

    onehot = jnp.zeros((N_EXPERTS, ts), F32)
    for i in top_idx:
        onehot = onehot + jnp.where(e_iota == i, 1.0, 0.0)
    onehot_b = onehot.astype(BF16)
    incl = jnp.where(row <= col, 1.0, 0.0).astype(BF16)
    csum = _dot(onehot_b, incl)
    total = _dot(onehot_b, jnp.ones((ts, ts), BF16))
    yield
    n_chunks_e = jnp.floor((total + (RUN_CHUNK - 1)) * (1.0 / RUN_CHUNK))
    e_lower = jnp.where(lax.broadcasted_iota(jnp.int32, (N_EXPERTS, N_EXPERTS), 1)
                        < lax.broadcasted_iota(jnp.int32, (N_EXPERTS, N_EXPERTS), 0), 1.0, 0.0).astype(BF16)
    local_start = RUN_CHUNK * _dot(e_lower, n_chunks_e.astype(BF16))
    local_pos = local_start + csum - onehot
    yield
    lpos = [jnp.sum(jnp.where(e_iota == i, local_pos, 0.0), axis=0, keepdims=True).astype(jnp.int32)
            for i in top_idx]
    cnt_ref[ti, 0] = total[:, :LANES]

    r8 = lax.broadcasted_iota(jnp.int32, (SUBLANES, ts), 0)

    def rows8(vals, fill):
        out = jnp.full((SUBLANES, ts), fill, vals[0].dtype)
        for kk, v in enumerate(vals):
            out = jnp.where(r8 == kk, v, out)
        return out

    lpos_ref[ti] = rows8(lpos, -1)
    wts_ref[ti] = rows8(top_w, 0.0)


def _token_mixing(x, nmw, wall, wgk, bgk, glanw, lnw, lnb, w2, bsb, wout, nfw, wrh, wrl, br):
    B, S, D = x.shape
    ts = SEQ_TILE
    ns = S // ts
    T = B * S

    def const(shape):
        return pl.BlockSpec(shape, lambda b, s: (0,) * len(shape))

    nbr = MIX_BATCH_ROWS
    return pl.pallas_call(
        _mix_kernel,
        grid=(B // nbr, ns),
        in_specs=[
            pl.BlockSpec((nbr, ts, D), lambda b, s: (b, s, 0)),
            const((1, D)), const((D, IN_COLS_PACKED)), const((LANES, GLA_QK)), const((1, GLA_QK)),
            const((1, GLA_WIDTH)), const((1, SG_WIDTH)), const((1, SG_WIDTH)),
            const((SG_GROUPS, ts, ts)), const((SG_GROUPS, ts, SG_CH)),
            const((D, D)), const((1, D)), const((N_EXPERTS, D)), const((N_EXPERTS, D)), const((N_EXPERTS, 1)),
        ],
        out_specs=[
            pl.BlockSpec((nbr, ts, D), lambda b, s: (b, s, 0)),
            pl.BlockSpec((nbr, ts, D), lambda b, s: (b, s, 0)),
            pl.BlockSpec((nbr, SUBLANES, ts), lambda b, s: (b, 0, s)),
            pl.BlockSpec((nbr, SUBLANES, ts), lambda b, s: (b, 0, s)),
            pl.BlockSpec((nbr, 1, N_EXPERTS, LANES), lambda b, s: (b, s, 0, 0)),
        ],
        out_shape=[
            jax.ShapeDtypeStruct((B, S, D), F32),
            jax.ShapeDtypeStruct((B, S, D), BF16),
            jax.ShapeDtypeStruct((B, SUBLANES, S), jnp.int32),
            jax.ShapeDtypeStruct((B, SUBLANES, S), F32),
            jax.ShapeDtypeStruct((B, ns, N_EXPERTS, LANES), F32),
        ],
        scratch_shapes=[pltpu.VMEM((nbr, GLA_QK, GLA_DV), F32)],
        compiler_params=pltpu.CompilerParams(
            dimension_semantics=("arbitrary", "arbitrary"), vmem_limit_bytes=VMEM_LIMIT_BYTES),
        name="token_mixing",
    )(x, nmw, wall, wgk, bgk, glanw, lnw, lnb, w2, bsb, wout, nfw, wrh, wrl, br)


def _chunk_rows(start_row):
    return pl.ds(pl.multiple_of(start_row * ROW_TILES, ROW_TILES), RUN_CHUNK * ROW_TILES)


def _wait_chunks(hbm_ref, sem, n_chunks):
    @pl.when(n_chunks > 0)
    def _():
        rows = pl.ds(0, n_chunks * (RUN_CHUNK * ROW_TILES))
        pltpu.make_async_copy(hbm_ref.at[rows, :], hbm_ref.at[rows, :], sem).wait()


def _one_hot_positions(lpos_ref, values):
    ts = SEQ_TILE
    p_iota = lax.broadcasted_iota(jnp.int32, (LOCAL_ROWS, ts), 0)
    out = jnp.zeros((LOCAL_ROWS, ts), F32)
    for kk in range(TOP_K):
        out = jnp.where(p_iota == lpos_ref[kk:kk + 1, :], values[kk], out)
    return out.astype(BF16)


def _dispatch_kernel(goff_ref, nch_ref, loff_ref, ntot_ref, zst_ref, znc_ref,
                     h2_ref, lpos_ref, xbuf_hbm, srt_ref, zero_ref, sem, zsem):
    s = pl.program_id(0)
    n_steps = pl.num_programs(0)
    slot = s % 2

    @pl.when(s == 0)
    def _():
        zero_ref[...] = jnp.zeros_like(zero_ref)

        def per_expert(e, total):
            def per_chunk(c, carry):
                pltpu.make_async_copy(zero_ref, xbuf_hbm.at[_chunk_rows(zst_ref[e] + c * RUN_CHUNK), :], zsem).start()
                return carry
            lax.fori_loop(0, znc_ref[e], per_chunk, 0)
            return total + znc_ref[e]
        n_zero = lax.fori_loop(0, N_EXPERTS + 1, per_expert, 0)
        _wait_chunks(xbuf_hbm, zsem, n_zero)

    ones = [1.0] * TOP_K
    perm = _one_hot_positions(lpos_ref, ones)
    local_sorted = _dot(perm, h2_ref[...])
    _rows_to_tiles(srt_ref, (slot,), local_sorted, LOCAL_ROWS)

    @pl.when(s > 0)
    def _():
        _wait_chunks(xbuf_hbm, sem.at[1 - slot], ntot_ref[s - 1])

    def per_expert(e, carry):
        g0 = goff_ref[s * N_EXPERTS + e]
        l0 = loff_ref[s * N_EXPERTS + e]

        def per_chunk(c, carry2):
            pltpu.make_async_copy(srt_ref.at[slot, _chunk_rows(l0 + c * RUN_CHUNK), :],
                                  xbuf_hbm.at[_chunk_rows(g0 + c * RUN_CHUNK), :], sem.at[slot]).start()
            return carry2
        lax.fori_loop(0, nch_ref[s * N_EXPERTS + e], per_chunk, 0)
        return carry
    lax.fori_loop(0, N_EXPERTS, per_expert, 0)

    @pl.when(s == n_steps - 1)
    def _():
        _wait_chunks(xbuf_hbm, sem.at[slot], ntot_ref[s])


def _dispatch(tables, h2b, lpos8, n_rows_buf):
    B, S, D = h2b.shape
    ts = SEQ_TILE
    ns = S // ts
    n_tiles = B * ns
    grid_spec = pltpu.PrefetchScalarGridSpec(
        num_scalar_prefetch=len(tables),
        grid=(n_tiles,),
        in_specs=[
            pl.BlockSpec((None, ts, D), lambda t, *_: (t // ns, t % ns, 0)),
            pl.BlockSpec((None, SUBLANES, ts), lambda t, *_: (t // ns, 0, t % ns)),
        ],
        out_specs=pl.BlockSpec(memory_space=pl.ANY),
        scratch_shapes=[
            pltpu.VMEM((2, LOCAL_ROWS * ROW_TILES, LANES), F32),
            pltpu.VMEM((RUN_CHUNK * ROW_TILES, LANES), F32),
            pltpu.SemaphoreType.DMA((2,)),
            pltpu.SemaphoreType.DMA(()),
        ],
    )
    return pl.pallas_call(
        _dispatch_kernel,
        grid_spec=grid_spec,
        out_shape=jax.ShapeDtypeStruct((n_rows_buf * ROW_TILES, LANES), F32),
        compiler_params=pltpu.CompilerParams(
            dimension_semantics=("arbitrary",), vmem_limit_bytes=VMEM_LIMIT_BYTES),
        name="dispatch",
    )(*tables, h2b, lpos8)


def _expert_kernel(be_ref, nu_ref, x_ref, wgu_hbm, bgu_ref, wd_hbm, bd_ref, y_ref,
                   wgu_f_ref, wd_f_ref, wgu_b_ref, wd_b_ref, wsem):
    tm = EXPERT_ROWS
    i = pl.program_id(0)
    n_used = nu_ref[0]
    e = be_ref[i]

    def weight_copies(expert, slot_):
        return (pltpu.make_async_copy(wgu_hbm.at[expert], wgu_f_ref.at[slot_], wsem.at[0, slot_]),
                pltpu.make_async_copy(wd_hbm.at[expert], wd_f_ref.at[slot_], wsem.at[1, slot_]))

    @pl.when(i == 0)
    def _():
        for cp in weight_copies(e, e % 2):
            cp.start()

    @pl.when(i < n_used)
    def _():
        first_of_expert = jnp.logical_or(i == 0, e != be_ref[jnp.maximum(i - 1, 0)])

        @pl.when(first_of_expert)
        def _():
            slot = e % 2
            for cp in weight_copies(e, slot):
                cp.wait()

            @pl.when(e + 1 < N_EXPERTS)
            def _():
                for cp in weight_copies(e + 1, 1 - slot):
                    cp.start()

            step = 128
            for r0 in range(0, D_MODEL, step):
                wgu_b_ref[r0:r0 + step, :] = wgu_f_ref[slot, r0:r0 + step, :].astype(BF16)
                wd_b_ref[r0:r0 + step, :] = wd_f_ref[slot, r0:r0 + step, :].astype(BF16)

        xb = _rows_from_tiles(x_ref, (), tm).astype(BF16)
        gu = _dot(xb, wgu_b_ref[...]) + bgu_ref[...]
        gate = jnp.minimum(gu[:, :D_EXPERT], SWIGLU_LIMIT)
        up = jnp.clip(gu[:, D_EXPERT:], -SWIGLU_LIMIT, SWIGLU_LIMIT)
        act = ((up + 1.0) * (gate * jax.nn.sigmoid(SWIGLU_ALPHA * gate))).astype(BF16)
        y = _dot(act, wd_b_ref[...]) + bd_ref[...]
        _rows_to_tiles(y_ref, (), y, tm)

    @pl.when(i >= n_used)
    def _():
        y_ref[...] = jnp.zeros_like(y_ref)


def _experts(block_e, n_used, nb, x_rows, w_gate_up, b_gate_up, w_down, b_down):
    tm = EXPERT_ROWS
    D = D_MODEL
    grid_spec = pltpu.PrefetchScalarGridSpec(
        num_scalar_prefetch=2,
        grid=(nb,),
        in_specs=[
            pl.BlockSpec((tm * ROW_TILES, LANES), lambda i, be, nu: (jnp.minimum(i, nu[0] - 1), 0)),
            pl.BlockSpec(memory_space=pl.ANY),
            pl.BlockSpec((None, 1, 2 * D_EXPERT), lambda i, be, nu: (be[i], 0, 0)),
            pl.BlockSpec(memory_space=pl.ANY),
            pl.BlockSpec((None, 1, D), lambda i, be, nu: (be[i], 0, 0)),
        ],
        out_specs=pl.BlockSpec((tm * ROW_TILES, LANES), lambda i, be, nu: (i, 0)),
        scratch_shapes=[
            pltpu.VMEM((2, D, 2 * D_EXPERT), F32),
            pltpu.VMEM((2, D_EXPERT, D), F32),
            pltpu.VMEM((D, 2 * D_EXPERT), BF16),
            pltpu.VMEM((D_EXPERT, D), BF16),
            pltpu.SemaphoreType.DMA((2, 2)),
        ],
    )
    return pl.pallas_call(
        _expert_kernel,
        grid_spec=grid_spec,
        out_shape=jax.ShapeDtypeStruct((nb * tm * ROW_TILES, LANES), F32),
        compiler_params=pltpu.CompilerParams(
            dimension_semantics=("arbitrary",), vmem_limit_bytes=VMEM_LIMIT_BYTES),
        name="routed_experts",
    )(block_e, n_used, x_rows, w_gate_up, b_gate_up, w_down, b_down)


def _combine_kernel(goff_ref, nch_ref, loff_ref, ntot_ref,
                    lpos_ref, wts_ref, y_hbm, x1_ref, nw_ref, out_ref, ysrt_ref, sem):
    s = pl.program_id(0)
    n_steps = pl.num_programs(0)
    slot = s % 2

    def issue(tile, slot_):
        def per_expert(e, carry):
            g0 = goff_ref[tile * N_EXPERTS + e]
            l0 = loff_ref[tile * N_EXPERTS + e]

            def per_chunk(c, carry2):
                pltpu.make_async_copy(y_hbm.at[_chunk_rows(g0 + c * RUN_CHUNK), :],
                                      ysrt_ref.at[slot_, _chunk_rows(l0 + c * RUN_CHUNK), :], sem.at[slot_]).start()
                return carry2
            lax.fori_loop(0, nch_ref[tile * N_EXPERTS + e], per_chunk, 0)
            return carry
        lax.fori_loop(0, N_EXPERTS, per_expert, 0)

    @pl.when(s == 0)
    def _():
        ysrt_ref[...] = jnp.zeros_like(ysrt_ref)
        issue(0, 0)

    @pl.when(s + 1 < n_steps)
    def _():
        issue(s + 1, 1 - slot)

    _wait_chunks(y_hbm, sem.at[slot], ntot_ref[s])

    weights = [wts_ref[kk:kk + 1, :] for kk in range(TOP_K)]
    perm_w = _one_hot_positions(lpos_ref, weights)
    y_local = _rows_from_tiles(ysrt_ref, (slot,), LOCAL_ROWS).astype(BF16)
    acc = x1_ref[...] + lax.dot_general(perm_w, y_local, _TN, preferred_element_type=F32)
    ms = jnp.mean(acc * acc, axis=-1, keepdims=True)
    out_ref[...] = acc * lax.rsqrt(ms + EPS) * nw_ref[...]


def _combine(tables, lpos8, wts8, y_rows, x1, norm_final_w):
    B, S, D = x1.shape
    ts = SEQ_TILE
    ns = S // ts
    n_tiles = B * ns
    meta = pl.BlockSpec((None, SUBLANES, ts), lambda t, *_: (t // ns, 0, t % ns))
    rows = pl.BlockSpec((None, ts, D), lambda t, *_: (t // ns, t % ns, 0))
    grid_spec = pltpu.PrefetchScalarGridSpec(
        num_scalar_prefetch=len(tables),
        grid=(n_tiles,),
        in_specs=[
            meta,
            meta,
            pl.BlockSpec(memory_space=pl.ANY),
            rows,
            pl.BlockSpec((1, D), lambda t, *_: (0, 0)),
        ],
        out_specs=rows,
        scratch_shapes=[
            pltpu.VMEM((2, LOCAL_ROWS * ROW_TILES, LANES), F32),
            pltpu.SemaphoreType.DMA((2,)),
        ],
    )
    return pl.pallas_call(
        _combine_kernel,
        grid_spec=grid_spec,
        out_shape=jax.ShapeDtypeStruct((B, S, D), F32),
        compiler_params=pltpu.CompilerParams(
            dimension_semantics=("arbitrary",), vmem_limit_bytes=VMEM_LIMIT_BYTES),
        name="combine",
    )(*tables, lpos8, wts8, y_rows, x1, norm_final_w)


def _pack_layer(w_in, w_gk_up, w_spatial, b_spatial, gla_norm_w, w_router, b_router):
    ts = SEQ_TILE
    q, k, v, g, gkl, u, vs = jnp.split(w_in, [256, 512, 1024, 1536, 1552, 2064], axis=1)
    gkl = jnp.pad(gkl, ((0, 0), (0, LANES - GLA_GATE_RANK)))
    wall = jnp.concatenate([q, k, v, g, u, vs, gkl], axis=1).astype(BF16)
    wgk = jnp.pad(w_gk_up, ((0, LANES - GLA_GATE_RANK), (0, 0))).astype(BF16)
    reps = ts // SG_CHUNK
    w2 = jnp.tile(w_spatial, (1, reps, reps)).astype(BF16)
    bsb = jnp.broadcast_to(jnp.tile(b_spatial, (1, reps))[:, :, None], (SG_GROUPS, ts, SG_CH))
    glanw = jnp.tile(gla_norm_w, GLA_HEADS)[None, :]
    wr_t = w_router.T
    wr_hi = wr_t.astype(BF16)
    wr_lo = (wr_t - wr_hi.astype(F32)).astype(BF16)
    return wall, wgk, w2, bsb, glanw, wr_hi, wr_lo, b_router[:, None]


def _routing_tables(tile_counts, nb):
    tm = EXPERT_ROWS
    n = tile_counts
    counts = jnp.sum(n, axis=0)
    padded = (counts + (RUN_CHUNK - 1) + tm - 1) // tm * tm
    padded_ends = jnp.cumsum(padded)
    padded_starts = padded_ends - padded
    goff = padded_starts[None, :] + jnp.cumsum(n, axis=0) - n
    nch = (n + RUN_CHUNK - 1) // RUN_CHUNK
    loff = RUN_CHUNK * (jnp.cumsum(nch, axis=1) - nch)
    ntot = jnp.sum(nch, axis=1)
    zero_start = jnp.concatenate([(padded_starts + counts) // RUN_CHUNK * RUN_CHUNK, padded_ends[-1:]])
    zero_end = jnp.concatenate([padded_ends, jnp.full((1,), nb * tm, padded_ends.dtype)])
    zero_chunks = (zero_end - zero_start) // RUN_CHUNK
    block_start = jnp.arange(nb, dtype=jnp.int32) * tm
    block_e = jnp.minimum(jnp.sum((block_start[:, None] >= padded_ends[None, :]).astype(jnp.int32), axis=1),
                          N_EXPERTS - 1)
    n_used = padded_ends[-1:] // tm
    i32 = lambda a: a.astype(jnp.int32)
    run_tables = (i32(goff.reshape(-1)), i32(nch.reshape(-1)), i32(loff.reshape(-1)), i32(ntot))
    return run_tables, (i32(zero_start), i32(zero_chunks)), i32(block_e), i32(n_used)


def kernel(x, norm_mix_w, w_in, w_gk_up, b_gk, gla_norm_w, sg_ln_w, sg_ln_b, w_spatial, b_spatial, w_out,
           norm_ffn_w, w_router, b_router, w_gate_up, b_gate_up, w_down, b_down, norm_final_w):
    B, S, D = x.shape
    T = B * S
    assert w_in.shape[0] == 1
    assert D == D_MODEL and S % SEQ_TILE == 0 and B % MIX_BATCH_ROWS == 0 and (T * TOP_K) % EXPERT_ROWS == 0
    tm = EXPERT_ROWS
    spare_blocks = -(-(N_EXPERTS * (RUN_CHUNK - 1)) // tm)
    nb = T * TOP_K // tm + N_EXPERTS + spare_blocks
    l = 0
    wall, wgk, w2, bsb, glanw, wr_hi, wr_lo, br = _pack_layer(
        w_in[l], w_gk_up[l], w_spatial[l], b_spatial[l], gla_norm_w[l], w_router[l], b_router[l])
    x1, h2b, lpos8, wts8, cnt = _token_mixing(
        x, norm_mix_w[l][None, :], wall, wgk, b_gk[l][None, :], glanw, sg_ln_w[l][None, :],
        sg_ln_b[l][None, :], w2, bsb, w_out[l].astype(BF16), norm_ffn_w[l][None, :], wr_hi, wr_lo, br)

    tile_counts = cnt[:, :, :, 0].reshape(T // SEQ_TILE, N_EXPERTS).astype(jnp.int32)
    run_tables, zero_tables, block_e, n_used = _routing_tables(tile_counts, nb)
    x_rows = _dispatch(run_tables + zero_tables, h2b, lpos8, nb * tm)
    y_rows = _experts(block_e, n_used, nb, x_rows, w_gate_up[l], b_gate_up[l][:, None, :],
                      w_down[l], b_down[l][:, None, :])
    return _combine(run_tables, lpos8, wts8, y_rows, x1, norm_final_w[None, :])
```

```python
import jax
import jax.numpy as jnp
from jax import lax
from jax.experimental import pallas as pl
from jax.experimental.pallas import tpu as pltpu

D_MODEL = 1024
GLA_HEADS = 4
GLA_DK = 64
GLA_DV = 128
GLA_QK = GLA_HEADS * GLA_DK
GLA_WIDTH = GLA_HEADS * GLA_DV
GLA_GATE_RANK = 16
GLA_GATE_NORMALIZER = 16.0
GLA_CHUNK = 64
SG_GROUPS = 4
SG_CH = 128
SG_WIDTH = SG_GROUPS * SG_CH
SG_CHUNK = 128
N_EXPERTS = 32
TOP_K = 4
D_EXPERT = D_MODEL
SWIGLU_LIMIT = 7.0
SWIGLU_ALPHA = 1.702
EPS = 1e-6

SUBLANES = 8
LANES = 128
ROW_TILES = D_MODEL // LANES
assert ROW_TILES == SUBLANES

SEQ_TILE = 256
MIX_BATCH_ROWS = 2
EXPERT_ROWS = 512
RUN_CHUNK = 16
LOCAL_ROWS = 1536
assert SEQ_TILE * TOP_K + N_EXPERTS * (RUN_CHUNK - 1) <= LOCAL_ROWS
VMEM_LIMIT_BYTES = 56 * 1024 * 1024

_Q0, _K0, _V0, _G0, _U0, _VS0, _GK0 = 0, 256, 512, 1024, 1536, 2048, 2560
IN_COLS_PACKED = 2688

F32 = jnp.float32
BF16 = jnp.bfloat16
_NT = (((1,), (1,)), ((), ()))
_TN = (((0,), (0,)), ((), ()))


def _dot(a, b):
    return jnp.dot(a, b, preferred_element_type=F32)


def _split_bf16(a):
    hi = a.astype(BF16)
    lo = (a - hi.astype(F32)).astype(BF16)
    return hi, lo


def _gelu_tanh(a):
    return 0.5 * a * (1.0 + jnp.tanh(0.7978845608028654 * (a + 0.044715 * (a * a * a))))


def _rows_from_tiles(ref, lead, n_rows, base=0):
    parts = [ref[(*lead, pl.ds(base + j, n_rows, stride=ROW_TILES), slice(None))] for j in range(ROW_TILES)]
    return jnp.concatenate(parts, axis=1)


def _rows_to_tiles(ref, lead, val, n_rows, base=0):
    for j in range(ROW_TILES):
        ref[(*lead, pl.ds(base + j, n_rows, stride=ROW_TILES), slice(None))] = val[:, j * LANES:(j + 1) * LANES]


_DONE = object()


def _mix_kernel(*refs):
    state_ref = refs[-1]

    @pl.when(pl.program_id(1) == 0)
    def _():
        state_ref[...] = jnp.zeros_like(state_ref)

    tiles = [_mix_tile(ti, *refs) for ti in range(MIX_BATCH_ROWS)]
    while tiles:
        tiles = [t for t in tiles if next(t, _DONE) is not _DONE]


def _mix_tile(ti, x_ref, nmw_ref, wall_ref, wgk_ref, bgk_ref, glanw_ref, lnw_ref, lnb_ref, w2_ref, bsb_ref,
              wout_ref, nfw_ref, wrh_ref, wrl_ref, br_ref,
              x1_ref, h2_ref, lpos_ref, wts_ref, cnt_ref,
              state_ref):
    ts = SEQ_TILE
    n_chunks = ts // GLA_CHUNK

    x = x_ref[ti]
    ms = jnp.mean(x * x, axis=-1, keepdims=True)
    hb = (x * lax.rsqrt(ms + EPS) * nmw_ref[...]).astype(BF16)
    yield

    def proj(lo, hi):
        return _dot(hb, wall_ref[:, lo:hi])

    gkl = proj(_GK0, IN_COLS_PACKED)
    z = _dot(gkl.astype(BF16), wgk_ref[...]) + bgk_ref[...]
    log_a = (jnp.minimum(z, 0.0) - jnp.log1p(jnp.exp(-jnp.abs(z)))) * (1.0 / GLA_GATE_NORMALIZER)
    la_hi, la_lo = _split_bf16(log_a)
    yield

    row = lax.broadcasted_iota(jnp.int32, (ts, ts), 0)
    col = lax.broadcasted_iota(jnp.int32, (ts, ts), 1)
    row_base = (row // GLA_CHUNK) * GLA_CHUNK
    in_chunk_le = jnp.where(col <= row, jnp.where(col >= row_base, 1.0, 0.0), 0.0)
    in_chunk_gt = jnp.where(col > row, jnp.where(col < row_base + GLA_CHUNK, 1.0, 0.0), 0.0)
    causal = in_chunk_le > 0.5
    lower = in_chunk_le.astype(BF16)
    upper = in_chunk_gt.astype(BF16)
    b_cum = _dot(lower, la_hi) + _dot(lower, la_lo)
    b_rest = _dot(upper, la_hi) + _dot(upper, la_lo)
    yield
    chunk_ind = jnp.where(
        lax.broadcasted_iota(jnp.int32, (ts, n_chunks * LANES), 0) // GLA_CHUNK
        == lax.broadcasted_iota(jnp.int32, (ts, n_chunks * LANES), 1) // LANES, 1.0, 0.0).astype(BF16)
    b_last_t = (lax.dot_general(la_hi, chunk_ind, _TN, preferred_element_type=F32)
                + lax.dot_general(la_lo, chunk_ind, _TN, preferred_element_type=F32))
    decay_t = jnp.exp(b_last_t)
    yield

    qk = proj(_Q0, _V0)
    q = qk[:, :GLA_QK]
    k = qk[:, GLA_QK:]
    q_dec = q * (GLA_DK ** -0.5) * jnp.exp(b_cum)
    k_inv = (k * jnp.exp(-b_cum)).astype(BF16)
    k_end = (k * jnp.exp(b_rest)).astype(BF16)
    yield
    vb = proj(_V0, _G0).astype(BF16)
    g = proj(_G0, _U0)
    yield

    head_of_lane = lax.broadcasted_iota(jnp.int32, (ts, GLA_QK), 1) // GLA_DK
    o_heads = []
    for h in range(GLA_HEADS):
        q_h = jnp.where(head_of_lane == h, q_dec, 0.0).astype(BF16)
        sc = lax.dot_general(q_h, k_inv, _NT, preferred_element_type=F32)
        sc = jnp.where(causal, sc, 0.0).astype(BF16)
        o_heads.append(_dot(sc, vb[:, h * GLA_DV:(h + 1) * GLA_DV]))
        yield

    state = state_ref[ti]
    head_of_lane_c = lax.broadcasted_iota(jnp.int32, (GLA_CHUNK, GLA_QK), 1) // GLA_DK
    o_inter = []
    for c in range(n_chunks):
        rows = slice(c * GLA_CHUNK, (c + 1) * GLA_CHUNK)
        q_c = q_dec[rows]
        q_stack = jnp.concatenate(
            [jnp.where(head_of_lane_c == h, q_c, 0.0) for h in range(GLA_HEADS)], axis=0).astype(BF16)
        o_inter.append(_dot(q_stack, state.astype(BF16)))
        upd = lax.dot_general(k_end[rows], vb[rows], _TN, preferred_element_type=F32)
        upd = jnp.concatenate(
            [upd[h * GLA_DK:(h + 1) * GLA_DK, h * GLA_DV:(h + 1) * GLA_DV] for h in range(GLA_HEADS)], axis=0)
        state = decay_t[:, c * LANES:(c + 1) * LANES] * state + upd
        yield
    state_ref[ti] = state

    gla_parts = []
    for h in range(GLA_HEADS):
        inter_h = jnp.concatenate([o_inter[c][h * GLA_CHUNK:(h + 1) * GLA_CHUNK] for c in range(n_chunks)], axis=0)
        o_h = o_heads[h] + inter_h
        ms_h = jnp.mean(o_h * o_h, axis=-1, keepdims=True)
        g_h = g[:, h * GLA_DV:(h + 1) * GLA_DV]
        gla_parts.append(o_h * lax.rsqrt(ms_h + EPS) * glanw_ref[:, h * GLA_DV:(h + 1) * GLA_DV]
                         * (g_h * jax.nn.sigmoid(g_h)))
        yield

    u = _gelu_tanh(proj(_U0, _VS0))
    vf = _gelu_tanh(proj(_VS0, _GK0))
    mu = jnp.mean(vf, axis=-1, keepdims=True)
    dv = vf - mu
    var = jnp.mean(dv * dv, axis=-1, keepdims=True)
    vn = (dv * lax.rsqrt(var + EPS) * lnw_ref[...] + lnb_ref[...]).astype(BF16)
    yield
    sg_base = (row // SG_CHUNK) * SG_CHUNK
    sg_mask = jnp.where(col <= row, jnp.where(col >= sg_base, 1.0, 0.0), 0.0) > 0.5
    sg_parts = []
    for gi in range(SG_GROUPS):
        w_c = jnp.where(sg_mask, w2_ref[gi], 0.0).astype(BF16)
        mixed_g = _dot(w_c, vn[:, gi * SG_CH:(gi + 1) * SG_CH]) + bsb_ref[gi]
        sg_parts.append(u[:, gi * SG_CH:(gi + 1) * SG_CH] * mixed_g)
        yield

    mixed = jnp.concatenate(gla_parts + sg_parts, axis=1).astype(BF16)
    x1 = x + _dot(mixed, wout_ref[...])
    x1_ref[ti] = x1
    yield

    ms2 = jnp.mean(x1 * x1, axis=-1, keepdims=True)
    h2 = x1 * lax.rsqrt(ms2 + EPS) * nfw_ref[...]
    h2_hi, h2_lo = _split_bf16(h2)
    h2_ref[ti] = h2_hi
    yield
    wr_hi = wrh_ref[...]
    logits = (lax.dot_general(wr_hi, h2_hi, _NT, preferred_element_type=F32)
              + lax.dot_general(wr_hi, h2_lo, _NT, preferred_element_type=F32)
              + lax.dot_general(wrl_ref[...], h2_hi, _NT, preferred_element_type=F32)
              + br_ref[...])
    yield
    e_iota = lax.broadcasted_iota(jnp.int32, (N_EXPERTS, ts), 0)
    work = logits
    top_val, top_idx = [], []
    for _ in range(TOP_K):
        m = jnp.max(work, axis=0, keepdims=True)
        i = jnp.min(jnp.where(work == m, e_iota, N_EXPERTS), axis=0, keepdims=True)
        top_val.append(m)
        top_idx.append(i)
        work = jnp.where(e_iota == i, -jnp.inf, work)
        yield
    ex = [jnp.exp(v - top_val[0]) for v in top_val]
    den = ex[0] + ex[1] + ex[2] + ex[3]
    top_w = [e / den for e in ex]
    yield

    onehot = jnp.zeros((N_EXPERTS, ts), F32)
    for i in top_idx:
        onehot = onehot + jnp.where(e_iota == i, 1.0, 0.0)
    onehot_b = onehot.astype(BF16)
    incl = jnp.where(row <= col, 1.0, 0.0).astype(BF16)
    csum = _dot(onehot_b, incl)
    total = _dot(onehot_b, jnp.ones((ts, ts), BF16))
    yield
    n_chunks_e = jnp.floor((total + (RUN_CHUNK - 1)) * (1.0 / RUN_CHUNK))
    e_lower = jnp.where(lax.broadcasted_iota(jnp.int32, (N_EXPERTS, N_EXPERTS), 1)
                        < lax.broadcasted_iota(jnp.int32, (N_EXPERTS, N_EXPERTS), 0), 1.0, 0.0).astype(BF16)
    local_start = RUN_CHUNK * _dot(e_lower, n_chunks_e.astype(BF16))
    local_pos = local_start + csum - onehot
    yield
    lpos = [jnp.sum(jnp.where(e_iota == i, local_pos, 0.0), axis=0, keepdims=True).astype(jnp.int32)
            for i in top_idx]
    cnt_ref[ti, 0] = total[:, :LANES]

    r8 = lax.broadcasted_iota(jnp.int32, (SUBLANES, ts), 0)

    def rows8(vals, fill):
        out = jnp.full((SUBLANES, ts), fill, vals[0].dtype)
        for kk, v in enumerate(vals):
            out = jnp.where(r8 == kk, v, out)
        return out

    lpos_ref[ti] = rows8(lpos, -1)
    wts_ref[ti] = rows8(top_w, 0.0)


def _token_mixing(x, nmw, wall, wgk, bgk, glanw, lnw, lnb, w2, bsb, wout, nfw, wrh, wrl, br):
    B, S, D = x.shape
    ts = SEQ_TILE
    ns = S // ts
    T = B * S

    def const(shape):
        return pl.BlockSpec(shape, lambda b, s: (0,) * len(shape))

    nbr = MIX_BATCH_ROWS
    return pl.pallas_call(
        _mix_kernel,
        grid=(B // nbr, ns),
        in_specs=[
            pl.BlockSpec((nbr, ts, D), lambda b, s: (b, s, 0)),
            const((1, D)), const((D, IN_COLS_PACKED)), const((LANES, GLA_QK)), const((1, GLA_QK)),
            const((1, GLA_WIDTH)), const((1, SG_WIDTH)), const((1, SG_WIDTH)),
            const((SG_GROUPS, ts, ts)), const((SG_GROUPS, ts, SG_CH)),
            const((D, D)), const((1, D)), const((N_EXPERTS, D)), const((N_EXPERTS, D)), const((N_EXPERTS, 1)),
        ],
        out_specs=[
            pl.BlockSpec((nbr, ts, D), lambda b, s: (b, s, 0)),
            pl.BlockSpec((nbr, ts, D), lambda b, s: (b, s, 0)),
            pl.BlockSpec((nbr, SUBLANES, ts), lambda b, s: (b, 0, s)),
            pl.BlockSpec((nbr, SUBLANES, ts), lambda b, s: (b, 0, s)),
            pl.BlockSpec((nbr, 1, N_EXPERTS, LANES), lambda b, s: (b, s, 0, 0)),
        ],
        out_shape=[
            jax.ShapeDtypeStruct((B, S, D), F32),
            jax.ShapeDtypeStruct((B, S, D), BF16),
            jax.ShapeDtypeStruct((B, SUBLANES, S), jnp.int32),
            jax.ShapeDtypeStruct((B, SUBLANES, S), F32),
            jax.ShapeDtypeStruct((B, ns, N_EXPERTS, LANES), F32),
        ],
        scratch_shapes=[pltpu.VMEM((nbr, GLA_QK, GLA_DV), F32)],
        compiler_params=pltpu.CompilerParams(
            dimension_semantics=("arbitrary", "arbitrary"), vmem_limit_bytes=VMEM_LIMIT_BYTES),
        name="token_mixing",
    )(x, nmw, wall, wgk, bgk, glanw, lnw, lnb, w2, bsb, wout, nfw, wrh, wrl, br)


def _chunk_rows(start_row, n_chunks=1):
    return pl.ds(pl.multiple_of(start_row * ROW_TILES, ROW_TILES), n_chunks * (RUN_CHUNK * ROW_TILES))


def _wait_chunks(hbm_ref, sem, n_chunks):
    @pl.when(n_chunks > 0)
    def _():
        rows = pl.ds(0, n_chunks * (RUN_CHUNK * ROW_TILES))
        pltpu.make_async_copy(hbm_ref.at[rows, :], hbm_ref.at[rows, :], sem).wait()


def _one_hot_positions(lpos_ref, values):
    ts = SEQ_TILE
    p_iota = lax.broadcasted_iota(jnp.int32, (LOCAL_ROWS, ts), 0)
    out = jnp.zeros((LOCAL_ROWS, ts), F32)
    for kk in range(TOP_K):
        out = jnp.where(p_iota == lpos_ref[kk:kk + 1, :], values[kk], out)
    return out.astype(BF16)


def _dispatch_kernel(goff_ref, nch_ref, loff_ref, ntot_ref, zst_ref, znc_ref,
                     h2_ref, lpos_ref, xbuf_hbm, srt_ref, zero_ref, sem, zsem):
    s = pl.program_id(0)
    n_steps = pl.num_programs(0)
    slot = s % 2

    @pl.when(s == 0)
    def _():
        zero_ref[...] = jnp.zeros_like(zero_ref)

        def per_expert(e, total):
            def per_chunk(c, carry):
                pltpu.make_async_copy(zero_ref, xbuf_hbm.at[_chunk_rows(zst_ref[e] + c * RUN_CHUNK), :], zsem).start()
                return carry
            lax.fori_loop(0, znc_ref[e], per_chunk, 0)
            return total + znc_ref[e]
        n_zero = lax.fori_loop(0, N_EXPERTS + 1, per_expert, 0)
        _wait_chunks(xbuf_hbm, zsem, n_zero)

    ones = [1.0] * TOP_K
    perm = _one_hot_positions(lpos_ref, ones)
    local_sorted = _dot(perm, h2_ref[...])
    _rows_to_tiles(srt_ref, (slot,), local_sorted, LOCAL_ROWS)

    @pl.when(s > 0)
    def _():
        _wait_chunks(xbuf_hbm, sem.at[1 - slot], ntot_ref[s - 1])

    def per_expert(e, carry):
        g0 = goff_ref[s * N_EXPERTS + e]
        l0 = loff_ref[s * N_EXPERTS + e]

        n = nch_ref[s * N_EXPERTS + e]

        @pl.when(n > 0)
        def _():
            pltpu.make_async_copy(srt_ref.at[slot, _chunk_rows(l0, n), :],
                                  xbuf_hbm.at[_chunk_rows(g0, n), :], sem.at[slot]).start()
        return carry
    lax.fori_loop(0, N_EXPERTS, per_expert, 0)

    @pl.when(s == n_steps - 1)
    def _():
        _wait_chunks(xbuf_hbm, sem.at[slot], ntot_ref[s])


def _dispatch(tables, h2b, lpos8, n_rows_buf):
    B, S, D = h2b.shape
    ts = SEQ_TILE
    ns = S // ts
    n_tiles = B * ns
    grid_spec = pltpu.PrefetchScalarGridSpec(
        num_scalar_prefetch=len(tables),
        grid=(n_tiles,),
        in_specs=[
            pl.BlockSpec((None, ts, D), lambda t, *_: (t // ns, t % ns, 0)),
            pl.BlockSpec((None, SUBLANES, ts), lambda t, *_: (t // ns, 0, t % ns)),
        ],
        out_specs=pl.BlockSpec(memory_space=pl.ANY),
        scratch_shapes=[
            pltpu.VMEM((2, LOCAL_ROWS * ROW_TILES, LANES), F32),
            pltpu.VMEM((RUN_CHUNK * ROW_TILES, LANES), F32),
            pltpu.SemaphoreType.DMA((2,)),
            pltpu.SemaphoreType.DMA(()),
        ],
    )
    return pl.pallas_call(
        _dispatch_kernel,
        grid_spec=grid_spec,
        out_shape=jax.ShapeDtypeStruct((n_rows_buf * ROW_TILES, LANES), F32),
        compiler_params=pltpu.CompilerParams(
            dimension_semantics=("arbitrary",), vmem_limit_bytes=VMEM_LIMIT_BYTES),
        name="dispatch",
    )(*tables, h2b, lpos8)


def _expert_kernel(be_ref, nu_ref, nv_ref, x_ref, wgu_hbm, bgu_ref, wd_hbm, bd_ref, y_ref,
                   wgu_f_ref, wd_f_ref, wgu_b_ref, wd_b_ref, wsem):
    tm = EXPERT_ROWS
    i = pl.program_id(0)
    n_used = nu_ref[0]
    e = be_ref[i]

    def weight_copies(expert, slot_):
        return (pltpu.make_async_copy(wgu_hbm.at[expert], wgu_f_ref.at[slot_], wsem.at[0, slot_]),
                pltpu.make_async_copy(wd_hbm.at[expert], wd_f_ref.at[slot_], wsem.at[1, slot_]))

    @pl.when(i == 0)
    def _():
        for cp in weight_copies(e, e % 2):
            cp.start()

    @pl.when(i < n_used)
    def _():
        first_of_expert = jnp.logical_or(i == 0, e != be_ref[jnp.maximum(i - 1, 0)])

        @pl.when(first_of_expert)
        def _():
            slot = e % 2
            for cp in weight_copies(e, slot):
                cp.wait()

            @pl.when(e + 1 < N_EXPERTS)
            def _():
                for cp in weight_copies(e + 1, 1 - slot):
                    cp.start()

            step = 128
            for r0 in range(0, D_MODEL, step):
                wgu_b_ref[r0:r0 + step, :] = wgu_f_ref[slot, r0:r0 + step, :].astype(BF16)
                wd_b_ref[r0:r0 + step, :] = wd_f_ref[slot, r0:r0 + step, :].astype(BF16)

        th = tm // 2

        def gate_up(half):
            xb = _rows_from_tiles(x_ref, (), th, base=half * th * ROW_TILES).astype(BF16)
            return _dot(xb, wgu_b_ref[...]) + bgu_ref[...]

        def swiglu(gu):
            gate = jnp.minimum(gu[:, :D_EXPERT], SWIGLU_LIMIT)
            up = jnp.clip(gu[:, D_EXPERT:], -SWIGLU_LIMIT, SWIGLU_LIMIT)
            return ((up + 1.0) * (gate * jax.nn.sigmoid(SWIGLU_ALPHA * gate))).astype(BF16)

        def down(half, act):
            y = _dot(act, wd_b_ref[...]) + bd_ref[...]
            _rows_to_tiles(y_ref, (), y, th, base=half * th * ROW_TILES)

        down(0, swiglu(gate_up(0)))

        @pl.when(nv_ref[i] > th)
        def _():
            down(1, swiglu(gate_up(1)))

        @pl.when(nv_ref[i] <= th)
        def _():
            y_ref[th * ROW_TILES:, :] = jnp.zeros((th * ROW_TILES, LANES), F32)

    @pl.when(i >= n_used)
    def _():
        y_ref[...] = jnp.zeros_like(y_ref)


def _experts(block_tables, nb, x_rows, w_gate_up, b_gate_up, w_down, b_down):
    tm = EXPERT_ROWS
    D = D_MODEL
    grid_spec = pltpu.PrefetchScalarGridSpec(
        num_scalar_prefetch=len(block_tables),
        grid=(nb,),
        in_specs=[
            pl.BlockSpec((tm * ROW_TILES, LANES), lambda i, be, nu, nv:(jnp.minimum(i, nu[0] - 1), 0)),
            pl.BlockSpec(memory_space=pl.ANY),
            pl.BlockSpec((None, 1, 2 * D_EXPERT), lambda i, be, nu, nv:(be[i], 0, 0)),
            pl.BlockSpec(memory_space=pl.ANY),
            pl.BlockSpec((None, 1, D), lambda i, be, nu, nv:(be[i], 0, 0)),
        ],
        out_specs=pl.BlockSpec((tm * ROW_TILES, LANES), lambda i, be, nu, nv:(i, 0)),
        scratch_shapes=[
            pltpu.VMEM((2, D, 2 * D_EXPERT), F32),
            pltpu.VMEM((2, D_EXPERT, D), F32),
            pltpu.VMEM((D, 2 * D_EXPERT), BF16),
            pltpu.VMEM((D_EXPERT, D), BF16),
            pltpu.SemaphoreType.DMA((2, 2)),
        ],
    )
    return pl.pallas_call(
        _expert_kernel,
        grid_spec=grid_spec,
        out_shape=jax.ShapeDtypeStruct((nb * tm * ROW_TILES, LANES), F32),
        compiler_params=pltpu.CompilerParams(
            dimension_semantics=("arbitrary",), vmem_limit_bytes=VMEM_LIMIT_BYTES),
        name="routed_experts",
    )(*block_tables, x_rows, w_gate_up, b_gate_up, w_down, b_down)


def _combine_kernel(goff_ref, nch_ref, loff_ref, ntot_ref,
                    lpos_ref, wts_ref, y_hbm, x1_ref, nw_ref, out_ref, ysrt_ref, sem):
    s = pl.program_id(0)
    n_steps = pl.num_programs(0)
    slot = s % 2

    def issue(tile, slot_):
        def per_expert(e, carry):
            g0 = goff_ref[tile * N_EXPERTS + e]
            l0 = loff_ref[tile * N_EXPERTS + e]

            n = nch_ref[tile * N_EXPERTS + e]

            @pl.when(n > 0)
            def _():
                pltpu.make_async_copy(y_hbm.at[_chunk_rows(g0, n), :],
                                      ysrt_ref.at[slot_, _chunk_rows(l0, n), :], sem.at[slot_]).start()
            return carry
        lax.fori_loop(0, N_EXPERTS, per_expert, 0)

    @pl.when(s == 0)
    def _():
        ysrt_ref[...] = jnp.zeros_like(ysrt_ref)
        issue(0, 0)

    @pl.when(s + 1 < n_steps)
    def _():
        issue(s + 1, 1 - slot)

    _wait_chunks(y_hbm, sem.at[slot], ntot_ref[s])

    weights = [wts_ref[kk:kk + 1, :] for kk in range(TOP_K)]
    perm_w = _one_hot_positions(lpos_ref, weights)
    y_local = _rows_from_tiles(ysrt_ref, (slot,), LOCAL_ROWS).astype(BF16)
    acc = x1_ref[...] + lax.dot_general(perm_w, y_local, _TN, preferred_element_type=F32)
    ms = jnp.mean(acc * acc, axis=-1, keepdims=True)
    out_ref[...] = acc * lax.rsqrt(ms + EPS) * nw_ref[...]


def _combine(tables, lpos8, wts8, y_rows, x1, norm_final_w):
    B, S, D = x1.shape
    ts = SEQ_TILE
    ns = S // ts
    n_tiles = B * ns
    meta = pl.BlockSpec((None, SUBLANES, ts), lambda t, *_: (t // ns, 0, t % ns))
    rows = pl.BlockSpec((None, ts, D), lambda t, *_: (t // ns, t % ns, 0))
    grid_spec = pltpu.PrefetchScalarGridSpec(
        num_scalar_prefetch=len(tables),
        grid=(n_tiles,),
        in_specs=[
            meta,
            meta,
            pl.BlockSpec(memory_space=pl.ANY),
            rows,
            pl.BlockSpec((1, D), lambda t, *_: (0, 0)),
        ],
        out_specs=rows,
        scratch_shapes=[
            pltpu.VMEM((2, LOCAL_ROWS * ROW_TILES, LANES), F32),
            pltpu.SemaphoreType.DMA((2,)),
        ],
    )
    return pl.pallas_call(
        _combine_kernel,
        grid_spec=grid_spec,
        out_shape=jax.ShapeDtypeStruct((B, S, D), F32),
        compiler_params=pltpu.CompilerParams(
            dimension_semantics=("arbitrary",), vmem_limit_bytes=VMEM_LIMIT_BYTES),
        name="combine",
    )(*tables, lpos8, wts8, y_rows, x1, norm_final_w)


def _pack_layer(w_in, w_gk_up, w_spatial, b_spatial, gla_norm_w, w_router, b_router):
    ts = SEQ_TILE
    q, k, v, g, gkl, u, vs = jnp.split(w_in, [256, 512, 1024, 1536, 1552, 2064], axis=1)
    gkl = jnp.pad(gkl, ((0, 0), (0, LANES - GLA_GATE_RANK)))
    wall = jnp.concatenate([q, k, v, g, u, vs, gkl], axis=1).astype(BF16)
    wgk = jnp.pad(w_gk_up, ((0, LANES - GLA_GATE_RANK), (0, 0))).astype(BF16)
    reps = ts // SG_CHUNK
    w2 = jnp.tile(w_spatial, (1, reps, reps)).astype(BF16)
    bsb = jnp.broadcast_to(jnp.tile(b_spatial, (1, reps))[:, :, None], (SG_GROUPS, ts, SG_CH))
    glanw = jnp.tile(gla_norm_w, GLA_HEADS)[None, :]
    wr_t = w_router.T
    wr_hi = wr_t.astype(BF16)
    wr_lo = (wr_t - wr_hi.astype(F32)).astype(BF16)
    return wall, wgk, w2, bsb, glanw, wr_hi, wr_lo, b_router[:, None]


def _routing_tables(tile_counts, nb):
    tm = EXPERT_ROWS
    n = tile_counts
    counts = jnp.sum(n, axis=0)
    padded = (counts + (RUN_CHUNK - 1) + tm - 1) // tm * tm
    padded_ends = jnp.cumsum(padded)
    padded_starts = padded_ends - padded
    goff = padded_starts[None, :] + jnp.cumsum(n, axis=0) - n
    nch = (n + RUN_CHUNK - 1) // RUN_CHUNK
    loff = RUN_CHUNK * (jnp.cumsum(nch, axis=1) - nch)
    ntot = jnp.sum(nch, axis=1)
    zero_start = jnp.concatenate([(padded_starts + counts) // RUN_CHUNK * RUN_CHUNK, padded_ends[-1:]])
    zero_end = jnp.concatenate([padded_ends, jnp.full((1,), nb * tm, padded_ends.dtype)])
    zero_chunks = (zero_end - zero_start) // RUN_CHUNK
    block_start = jnp.arange(nb, dtype=jnp.int32) * tm
    block_e = jnp.minimum(jnp.sum((block_start[:, None] >= padded_ends[None, :]).astype(jnp.int32), axis=1),
                          N_EXPERTS - 1)
    n_used = padded_ends[-1:] // tm
    real_end = padded_starts + counts
    is_e = block_e[:, None] == jnp.arange(N_EXPERTS, dtype=jnp.int32)[None, :]
    block_rows = jnp.clip(jnp.sum(jnp.where(is_e, real_end[None, :], 0), axis=1) - block_start, 0, tm)
    i32 = lambda a: a.astype(jnp.int32)
    run_tables = (i32(goff.reshape(-1)), i32(nch.reshape(-1)), i32(loff.reshape(-1)), i32(ntot))
    return run_tables, (i32(zero_start), i32(zero_chunks)), (i32(block_e), i32(n_used), i32(block_rows))


def kernel(x, norm_mix_w, w_in, w_gk_up, b_gk, gla_norm_w, sg_ln_w, sg_ln_b, w_spatial, b_spatial, w_out,
           norm_ffn_w, w_router, b_router, w_gate_up, b_gate_up, w_down, b_down, norm_final_w):
    B, S, D = x.shape
    T = B * S
    assert w_in.shape[0] == 1
    assert D == D_MODEL and S % SEQ_TILE == 0 and B % MIX_BATCH_ROWS == 0 and (T * TOP_K) % EXPERT_ROWS == 0
    tm = EXPERT_ROWS
    spare_blocks = -(-(N_EXPERTS * (RUN_CHUNK - 1)) // tm)
    nb = T * TOP_K // tm + N_EXPERTS + spare_blocks
    l = 0
    wall, wgk, w2, bsb, glanw, wr_hi, wr_lo, br = _pack_layer(
        w_in[l], w_gk_up[l], w_spatial[l], b_spatial[l], gla_norm_w[l], w_router[l], b_router[l])
    x1, h2b, lpos8, wts8, cnt = _token_mixing(
        x, norm_mix_w[l][None, :], wall, wgk, b_gk[l][None, :], glanw, sg_ln_w[l][None, :],
        sg_ln_b[l][None, :], w2, bsb, w_out[l].astype(BF16), norm_ffn_w[l][None, :], wr_hi, wr_lo, br)

    tile_counts = cnt[:, :, :, 0].reshape(T // SEQ_TILE, N_EXPERTS).astype(jnp.int32)
    run_tables, zero_tables, block_tables = _routing_tables(tile_counts, nb)
    x_rows = _dispatch(run_tables + zero_tables, h2b, lpos8, nb * tm)
    y_rows = _experts(block_tables, nb, x_rows, w_gate_up[l], b_gate_up[l][:, None, :],
                      w_down[l], b_down[l][:, None, :])
    return _combine(run_tables, lpos8, wts8, y_rows, x1, norm_final_w[None, :])
```

```python
import jax
import jax.numpy as jnp
from jax import lax
from jax.experimental import pallas as pl
from jax.experimental.pallas import tpu as pltpu

D_MODEL = 1024
GLA_HEADS = 4
GLA_DK = 64
GLA_DV = 128
GLA_QK = GLA_HEADS * GLA_DK
GLA_WIDTH = GLA_HEADS * GLA_DV
GLA_GATE_RANK = 16
GLA_GATE_NORMALIZER = 16.0
GLA_CHUNK = 64
SG_GROUPS = 4
SG_CH = 128
SG_WIDTH = SG_GROUPS * SG_CH
SG_CHUNK = 128
N_EXPERTS = 32
TOP_K = 4
D_EXPERT = D_MODEL
SWIGLU_LIMIT = 7.0
SWIGLU_ALPHA = 1.702
EPS = 1e-6

SUBLANES = 8
LANES = 128
ROW_TILES = D_MODEL // LANES
assert ROW_TILES == SUBLANES

SEQ_TILE = 256
MIX_BATCH_ROWS = 4
EXPERT_ROWS = 512
RUN_CHUNK = 16
LOCAL_ROWS = 1536
assert SEQ_TILE * TOP_K + N_EXPERTS * (RUN_CHUNK - 1) <= LOCAL_ROWS
VMEM_LIMIT_BYTES = 56 * 1024 * 1024

_Q0, _K0, _V0, _G0, _U0, _VS0, _GK0 = 0, 256, 512, 1024, 1536, 2048, 2560
IN_COLS_PACKED = 2688

F32 = jnp.float32
BF16 = jnp.bfloat16
_NT = (((1,), (1,)), ((), ()))
_TN = (((0,), (0,)), ((), ()))


def _dot(a, b):
    return jnp.dot(a, b, preferred_element_type=F32)


def _split_bf16(a):
    hi = a.astype(BF16)
    lo = (a - hi.astype(F32)).astype(BF16)
    return hi, lo


def _gelu_tanh(a):
    return 0.5 * a * (1.0 + jnp.tanh(0.7978845608028654 * (a + 0.044715 * (a * a * a))))


def _rows_from_tiles(ref, lead, n_rows, base=0):
    parts = [ref[(*lead, pl.ds(base + j, n_rows, stride=ROW_TILES), slice(None))] for j in range(ROW_TILES)]
    return jnp.concatenate(parts, axis=1)


def _rows_to_tiles(ref, lead, val, n_rows, base=0):
    for j in range(ROW_TILES):
        ref[(*lead, pl.ds(base + j, n_rows, stride=ROW_TILES), slice(None))] = val[:, j * LANES:(j + 1) * LANES]


_DONE = object()


def _mix_kernel(*refs):
    state_ref = refs[-1]

    @pl.when(pl.program_id(1) == 0)
    def _():
        state_ref[...] = jnp.zeros_like(state_ref)

    tiles = [_mix_tile(ti, *refs) for ti in range(MIX_BATCH_ROWS)]
    while tiles:
        tiles = [t for t in tiles if next(t, _DONE) is not _DONE]


def _mix_tile(ti, x_ref, nmw_ref, wall_ref, wgk_ref, bgk_ref, glanw_ref, lnw_ref, lnb_ref, w2_ref, bsb_ref,
              wout_ref, nfw_ref, wrh_ref, wrl_ref, br_ref,
              x1_ref, h2_ref, lpos_ref, wts_ref, cnt_ref,
              state_ref):
    ts = SEQ_TILE
    n_chunks = ts // GLA_CHUNK

    x = x_ref[ti]
    ms = jnp.mean(x * x, axis=-1, keepdims=True)
    hb = (x * lax.rsqrt(ms + EPS) * nmw_ref[...]).astype(BF16)
    yield

    def proj(lo, hi):
        return _dot(hb, wall_ref[:, lo:hi])

    gkl = proj(_GK0, IN_COLS_PACKED)
    z = _dot(gkl.astype(BF16), wgk_ref[...]) + bgk_ref[...]
    log_a = (jnp.minimum(z, 0.0) - jnp.log1p(jnp.exp(-jnp.abs(z)))) * (1.0 / GLA_GATE_NORMALIZER)
    la_hi, la_lo = _split_bf16(log_a)
    yield

    row = lax.broadcasted_iota(jnp.int32, (ts, ts), 0)
    col = lax.broadcasted_iota(jnp.int32, (ts, ts), 1)
    row_base = (row // GLA_CHUNK) * GLA_CHUNK
    in_chunk_le = jnp.where(col <= row, jnp.where(col >= row_base, 1.0, 0.0), 0.0)
    in_chunk_gt = jnp.where(col > row, jnp.where(col < row_base + GLA_CHUNK, 1.0, 0.0), 0.0)
    causal = in_chunk_le > 0.5
    lower = in_chunk_le.astype(BF16)
    upper = in_chunk_gt.astype(BF16)
    b_cum = _dot(lower, la_hi) + _dot(lower, la_lo)
    b_rest = _dot(upper, la_hi) + _dot(upper, la_lo)
    yield
    chunk_ind = jnp.where(
        lax.broadcasted_iota(jnp.int32, (ts, n_chunks * LANES), 0) // GLA_CHUNK
        == lax.broadcasted_iota(jnp.int32, (ts, n_chunks * LANES), 1) // LANES, 1.0, 0.0).astype(BF16)
    b_last_t = (lax.dot_general(la_hi, chunk_ind, _TN, preferred_element_type=F32)
                + lax.dot_general(la_lo, chunk_ind, _TN, preferred_element_type=F32))
    decay_t = jnp.exp(b_last_t)
    yield

    qk = proj(_Q0, _V0)
    q = qk[:, :GLA_QK]
    k = qk[:, GLA_QK:]
    q_dec = q * (GLA_DK ** -0.5) * jnp.exp(b_cum)
    k_inv = (k * jnp.exp(-b_cum)).astype(BF16)
    k_end = (k * jnp.exp(b_rest)).astype(BF16)
    yield
    vb = proj(_V0, _G0).astype(BF16)
    g = proj(_G0, _U0)
    yield

    head_of_lane = lax.broadcasted_iota(jnp.int32, (ts, GLA_QK), 1) // GLA_DK
    o_heads = []
    for h in range(GLA_HEADS):
        q_h = jnp.where(head_of_lane == h, q_dec, 0.0).astype(BF16)
        sc = lax.dot_general(q_h, k_inv, _NT, preferred_element_type=F32)
        sc = jnp.where(causal, sc, 0.0).astype(BF16)
        o_heads.append(_dot(sc, vb[:, h * GLA_DV:(h + 1) * GLA_DV]))
        yield

    state = state_ref[ti]
    head_of_lane_c = lax.broadcasted_iota(jnp.int32, (GLA_CHUNK, GLA_QK), 1) // GLA_DK
    o_inter = []
    for c in range(n_chunks):
        rows = slice(c * GLA_CHUNK, (c + 1) * GLA_CHUNK)
        q_c = q_dec[rows]
        q_stack = jnp.concatenate(
            [jnp.where(head_of_lane_c == h, q_c, 0.0) for h in range(GLA_HEADS)], axis=0).astype(BF16)
        o_inter.append(_dot(q_stack, state.astype(BF16)))
        upd = lax.dot_general(k_end[rows], vb[rows], _TN, preferred_element_type=F32)
        upd = jnp.concatenate(
            [upd[h * GLA_DK:(h + 1) * GLA_DK, h * GLA_DV:(h + 1) * GLA_DV] for h in range(GLA_HEADS)], axis=0)
        state = decay_t[:, c * LANES:(c + 1) * LANES] * state + upd
        yield
    state_ref[ti] = state

    gla_parts = []
    for h in range(GLA_HEADS):
        inter_h = jnp.concatenate([o_inter[c][h * GLA_CHUNK:(h + 1) * GLA_CHUNK] for c in range(n_chunks)], axis=0)
        o_h = o_heads[h] + inter_h
        ms_h = jnp.mean(o_h * o_h, axis=-1, keepdims=True)
        g_h = g[:, h * GLA_DV:(h + 1) * GLA_DV]
        gla_parts.append(o_h * lax.rsqrt(ms_h + EPS) * glanw_ref[:, h * GLA_DV:(h + 1) * GLA_DV]
                         * (g_h * jax.nn.sigmoid(g_h)))
        yield

    u = _gelu_tanh(proj(_U0, _VS0))
    vf = _gelu_tanh(proj(_VS0, _GK0))
    mu = jnp.mean(vf, axis=-1, keepdims=True)
    dv = vf - mu
    var = jnp.mean(dv * dv, axis=-1, keepdims=True)
    vn = (dv * lax.rsqrt(var + EPS) * lnw_ref[...] + lnb_ref[...]).astype(BF16)
    yield
    sg_base = (row // SG_CHUNK) * SG_CHUNK
    sg_mask = jnp.where(col <= row, jnp.where(col >= sg_base, 1.0, 0.0), 0.0) > 0.5
    sg_parts = []
    for gi in range(SG_GROUPS):
        w_c = jnp.where(sg_mask, w2_ref[gi], 0.0).astype(BF16)
        mixed_g = _dot(w_c, vn[:, gi * SG_CH:(gi + 1) * SG_CH]) + bsb_ref[gi]
        sg_parts.append(u[:, gi * SG_CH:(gi + 1) * SG_CH] * mixed_g)
        yield

    mixed = jnp.concatenate(gla_parts + sg_parts, axis=1).astype(BF16)
    x1 = x + _dot(mixed, wout_ref[...])
    x1_ref[ti] = x1
    yield

    ms2 = jnp.mean(x1 * x1, axis=-1, keepdims=True)
    h2 = x1 * lax.rsqrt(ms2 + EPS) * nfw_ref[...]
    h2_hi, h2_lo = _split_bf16(h2)
    h2_ref[ti] = h2_hi
    yield
    wr_hi = wrh_ref[...]
    logits = (lax.dot_general(wr_hi, h2_hi, _NT, preferred_element_type=F32)
              + lax.dot_general(wr_hi, h2_lo, _NT, preferred_element_type=F32)
              + lax.dot_general(wrl_ref[...], h2_hi, _NT, preferred_element_type=F32)
              + br_ref[...])
    yield
    e_iota = lax.broadcasted_iota(jnp.int32, (N_EXPERTS, ts), 0)
    work = logits
    top_val, top_idx = [], []
    for _ in range(TOP_K):
        m = jnp.max(work, axis=0, keepdims=True)
        i = jnp.min(jnp.where(work == m, e_iota, N_EXPERTS), axis=0, keepdims=True)
        top_val.append(m)
        top_idx.append(i)
        work = jnp.where(e_iota == i, -jnp.inf, work)
        yield
    ex = [jnp.exp(v - top_val[0]) for v in top_val]
    den = ex[0] + ex[1] + ex[2] + ex[3]
    top_w = [e / den for e in ex]
    yield

    onehot = jnp.zeros((N_EXPERTS, ts), F32)
    for i in top_idx:
        onehot = onehot + jnp.where(e_iota == i, 1.0, 0.0)
    onehot_b = onehot.astype(BF16)
    incl = jnp.where(row <= col, 1.0, 0.0).astype(BF16)
    csum = _dot(onehot_b, incl)
    total = _dot(onehot_b, jnp.ones((ts, ts), BF16))
    yield
    n_chunks_e = jnp.floor((total + (RUN_CHUNK - 1)) * (1.0 / RUN_CHUNK))
    e_lower = jnp.where(lax.broadcasted_iota(jnp.int32, (N_EXPERTS, N_EXPERTS), 1)
                        < lax.broadcasted_iota(jnp.int32, (N_EXPERTS, N_EXPERTS), 0), 1.0, 0.0).astype(BF16)
    local_start = RUN_CHUNK * _dot(e_lower, n_chunks_e.astype(BF16))
    local_pos = local_start + csum - onehot
    yield
    lpos = [jnp.sum(jnp.where(e_iota == i, local_pos, 0.0), axis=0, keepdims=True).astype(jnp.int32)
            for i in top_idx]
    cnt_ref[ti, 0] = total[:, :LANES]

    r8 = lax.broadcasted_iota(jnp.int32, (SUBLANES, ts), 0)

    def rows8(vals, fill):
        out = jnp.full((SUBLANES, ts), fill, vals[0].dtype)
        for kk, v in enumerate(vals):
            out = jnp.where(r8 == kk, v, out)
        return out

    lpos_ref[ti] = rows8(lpos, -1)
    wts_ref[ti] = rows8(top_w, 0.0)


def _token_mixing(x, nmw, wall, wgk, bgk, glanw, lnw, lnb, w2, bsb, wout, nfw, wrh, wrl, br):
    B, S, D = x.shape
    ts = SEQ_TILE
    ns = S // ts
    T = B * S

    def const(shape):
        return pl.BlockSpec(shape, lambda b, s: (0,) * len(shape))

    nbr = MIX_BATCH_ROWS
    return pl.pallas_call(
        _mix_kernel,
        grid=(B // nbr, ns),
        in_specs=[
            pl.BlockSpec((nbr, ts, D), lambda b, s: (b, s, 0)),
            const((1, D)), const((D, IN_COLS_PACKED)), const((LANES, GLA_QK)), const((1, GLA_QK)),
            const((1, GLA_WIDTH)), const((1, SG_WIDTH)), const((1, SG_WIDTH)),
            const((SG_GROUPS, ts, ts)), const((SG_GROUPS, ts, SG_CH)),
            const((D, D)), const((1, D)), const((N_EXPERTS, D)), const((N_EXPERTS, D)), const((N_EXPERTS, 1)),
        ],
        out_specs=[
            pl.BlockSpec((nbr, ts, D), lambda b, s: (b, s, 0)),
            pl.BlockSpec((nbr, ts, D), lambda b, s: (b, s, 0)),
            pl.BlockSpec((nbr, SUBLANES, ts), lambda b, s: (b, 0, s)),
            pl.BlockSpec((nbr, SUBLANES, ts), lambda b, s: (b, 0, s)),
            pl.BlockSpec((nbr, 1, N_EXPERTS, LANES), lambda b, s: (b, s, 0, 0)),
        ],
        out_shape=[
            jax.ShapeDtypeStruct((B, S, D), F32),
            jax.ShapeDtypeStruct((B, S, D), BF16),
            jax.ShapeDtypeStruct((B, SUBLANES, S), jnp.int32),
            jax.ShapeDtypeStruct((B, SUBLANES, S), F32),
            jax.ShapeDtypeStruct((B, ns, N_EXPERTS, LANES), F32),
        ],
        scratch_shapes=[pltpu.VMEM((nbr, GLA_QK, GLA_DV), F32)],
        compiler_params=pltpu.CompilerParams(
            dimension_semantics=("arbitrary", "arbitrary"), vmem_limit_bytes=VMEM_LIMIT_BYTES),
        name="token_mixing",
    )(x, nmw, wall, wgk, bgk, glanw, lnw, lnb, w2, bsb, wout, nfw, wrh, wrl, br)


def _chunk_rows(start_row, n_chunks=1):
    return pl.ds(pl.multiple_of(start_row * ROW_TILES, ROW_TILES), n_chunks * (RUN_CHUNK * ROW_TILES))


def _wait_chunks(hbm_ref, sem, n_chunks):
    @pl.when(n_chunks > 0)
    def _():
        rows = pl.ds(0, n_chunks * (RUN_CHUNK * ROW_TILES))
        pltpu.make_async_copy(hbm_ref.at[rows, :], hbm_ref.at[rows, :], sem).wait()


def _one_hot_positions(lpos_ref, values):
    ts = SEQ_TILE
    p_iota = lax.broadcasted_iota(jnp.int32, (LOCAL_ROWS, ts), 0)
    out = jnp.zeros((LOCAL_ROWS, ts), F32)
    for kk in range(TOP_K):
        out = jnp.where(p_iota == lpos_ref[kk:kk + 1, :], values[kk], out)
    return out.astype(BF16)


def _dispatch_kernel(goff_ref, nch_ref, loff_ref, ntot_ref, zst_ref, znc_ref,
                     h2_ref, lpos_ref, xbuf_hbm, srt_ref, zero_ref, sem, zsem):
    s = pl.program_id(0)
    n_steps = pl.num_programs(0)
    slot = s % 2

    @pl.when(s == 0)
    def _():
        zero_ref[...] = jnp.zeros_like(zero_ref)

        def per_expert(e, total):
            def per_chunk(c, carry):
                pltpu.make_async_copy(zero_ref, xbuf_hbm.at[_chunk_rows(zst_ref[e] + c * RUN_CHUNK), :], zsem).start()
                return carry
            lax.fori_loop(0, znc_ref[e], per_chunk, 0)
            return total + znc_ref[e]
        n_zero = lax.fori_loop(0, N_EXPERTS + 1, per_expert, 0)
        _wait_chunks(xbuf_hbm, zsem, n_zero)

    ones = [1.0] * TOP_K
    perm = _one_hot_positions(lpos_ref, ones)
    local_sorted = _dot(perm, h2_ref[...])
    _rows_to_tiles(srt_ref, (slot,), local_sorted, LOCAL_ROWS)

    @pl.when(s > 0)
    def _():
        _wait_chunks(xbuf_hbm, sem.at[1 - slot], ntot_ref[s - 1])

    def per_expert(e, carry):
        g0 = goff_ref[s * N_EXPERTS + e]
        l0 = loff_ref[s * N_EXPERTS + e]

        n = nch_ref[s * N_EXPERTS + e]

        @pl.when(n > 0)
        def _():
            pltpu.make_async_copy(srt_ref.at[slot, _chunk_rows(l0, n), :],
                                  xbuf_hbm.at[_chunk_rows(g0, n), :], sem.at[slot]).start()
        return carry
    lax.fori_loop(0, N_EXPERTS, per_expert, 0)

    @pl.when(s == n_steps - 1)
    def _():
        _wait_chunks(xbuf_hbm, sem.at[slot], ntot_ref[s])


def _dispatch(tables, h2b, lpos8, n_rows_buf):
    B, S, D = h2b.shape
    ts = SEQ_TILE
    ns = S // ts
    n_tiles = B * ns
    grid_spec = pltpu.PrefetchScalarGridSpec(
        num_scalar_prefetch=len(tables),
        grid=(n_tiles,),
        in_specs=[
            pl.BlockSpec((None, ts, D), lambda t, *_: (t // ns, t % ns, 0)),
            pl.BlockSpec((None, SUBLANES, ts), lambda t, *_: (t // ns, 0, t % ns)),
        ],
        out_specs=pl.BlockSpec(memory_space=pl.ANY),
        scratch_shapes=[
            pltpu.VMEM((2, LOCAL_ROWS * ROW_TILES, LANES), F32),
            pltpu.VMEM((RUN_CHUNK * ROW_TILES, LANES), F32),
            pltpu.SemaphoreType.DMA((2,)),
            pltpu.SemaphoreType.DMA(()),
        ],
    )
    return pl.pallas_call(
        _dispatch_kernel,
        grid_spec=grid_spec,
        out_shape=jax.ShapeDtypeStruct((n_rows_buf * ROW_TILES, LANES), F32),
        compiler_params=pltpu.CompilerParams(
            dimension_semantics=("arbitrary",), vmem_limit_bytes=VMEM_LIMIT_BYTES),
        name="dispatch",
    )(*tables, h2b, lpos8)


def _expert_kernel(be_ref, nu_ref, x_ref, wgu_hbm, bgu_ref, wd_hbm, bd_ref, y_ref,
                   wgu_f_ref, wd_f_ref, wgu_b_ref, wd_b_ref, wsem):
    tm = EXPERT_ROWS
    i = pl.program_id(0)
    n_used = nu_ref[0]
    e = be_ref[i]

    def weight_copies(expert, slot_):
        return (pltpu.make_async_copy(wgu_hbm.at[expert], wgu_f_ref.at[slot_], wsem.at[0, slot_]),
                pltpu.make_async_copy(wd_hbm.at[expert], wd_f_ref.at[slot_], wsem.at[1, slot_]))

    @pl.when(i == 0)
    def _():
        for cp in weight_copies(e, e % 2):
            cp.start()

    @pl.when(i < n_used)
    def _():
        first_of_expert = jnp.logical_or(i == 0, e != be_ref[jnp.maximum(i - 1, 0)])

        @pl.when(first_of_expert)
        def _():
            slot = e % 2
            for cp in weight_copies(e, slot):
                cp.wait()

            @pl.when(e + 1 < N_EXPERTS)
            def _():
                for cp in weight_copies(e + 1, 1 - slot):
                    cp.start()

            step = 128
            for r0 in range(0, D_MODEL, step):
                wgu_b_ref[r0:r0 + step, :] = wgu_f_ref[slot, r0:r0 + step, :].astype(BF16)
                wd_b_ref[r0:r0 + step, :] = wd_f_ref[slot, r0:r0 + step, :].astype(BF16)

        xb = _rows_from_tiles(x_ref, (), tm).astype(BF16)
        gu = _dot(xb, wgu_b_ref[...]) + bgu_ref[...]
        gate = jnp.minimum(gu[:, :D_EXPERT], SWIGLU_LIMIT)
        up = jnp.clip(gu[:, D_EXPERT:], -SWIGLU_LIMIT, SWIGLU_LIMIT)
        act = ((up + 1.0) * (gate * jax.nn.sigmoid(SWIGLU_ALPHA * gate))).astype(BF16)
        y = _dot(act, wd_b_ref[...]) + bd_ref[...]
        _rows_to_tiles(y_ref, (), y, tm)

    @pl.when(i >= n_used)
    def _():
        y_ref[...] = jnp.zeros_like(y_ref)


def _experts(block_tables, nb, x_rows, w_gate_up, b_gate_up, w_down, b_down):
    tm = EXPERT_ROWS
    D = D_MODEL
    grid_spec = pltpu.PrefetchScalarGridSpec(
        num_scalar_prefetch=len(block_tables),
        grid=(nb,),
        in_specs=[
            pl.BlockSpec((tm * ROW_TILES, LANES), lambda i, be, nu:(jnp.minimum(i, nu[0] - 1), 0)),
            pl.BlockSpec(memory_space=pl.ANY),
            pl.BlockSpec((None, 1, 2 * D_EXPERT), lambda i, be, nu:(be[i], 0, 0)),
            pl.BlockSpec(memory_space=pl.ANY),
            pl.BlockSpec((None, 1, D), lambda i, be, nu:(be[i], 0, 0)),
        ],
        out_specs=pl.BlockSpec((tm * ROW_TILES, LANES), lambda i, be, nu:(i, 0)),
        scratch_shapes=[
            pltpu.VMEM((2, D, 2 * D_EXPERT), F32),
            pltpu.VMEM((2, D_EXPERT, D), F32),
            pltpu.VMEM((D, 2 * D_EXPERT), BF16),
            pltpu.VMEM((D_EXPERT, D), BF16),
            pltpu.SemaphoreType.DMA((2, 2)),
        ],
    )
    return pl.pallas_call(
        _expert_kernel,
        grid_spec=grid_spec,
        out_shape=jax.ShapeDtypeStruct((nb * tm * ROW_TILES, LANES), F32),
        compiler_params=pltpu.CompilerParams(
            dimension_semantics=("arbitrary",), vmem_limit_bytes=VMEM_LIMIT_BYTES),
        name="routed_experts",
    )(*block_tables, x_rows, w_gate_up, b_gate_up, w_down, b_down)


def _combine_kernel(goff_ref, nch_ref, loff_ref, ntot_ref,
                    lpos_ref, wts_ref, y_hbm, x1_ref, nw_ref, out_ref, ysrt_ref, sem):
    s = pl.program_id(0)
    n_steps = pl.num_programs(0)
    slot = s % 2

    def issue(tile, slot_):
        def per_expert(e, carry):
            g0 = goff_ref[tile * N_EXPERTS + e]
            l0 = loff_ref[tile * N_EXPERTS + e]

            n = nch_ref[tile * N_EXPERTS + e]

            @pl.when(n > 0)
            def _():
                pltpu.make_async_copy(y_hbm.at[_chunk_rows(g0, n), :],
                                      ysrt_ref.at[slot_, _chunk_rows(l0, n), :], sem.at[slot_]).start()
            return carry
        lax.fori_loop(0, N_EXPERTS, per_expert, 0)

    @pl.when(s == 0)
    def _():
        ysrt_ref[...] = jnp.zeros_like(ysrt_ref)
        issue(0, 0)

    @pl.when(s + 1 < n_steps)
    def _():
        issue(s + 1, 1 - slot)

    _wait_chunks(y_hbm, sem.at[slot], ntot_ref[s])

    weights = [wts_ref[kk:kk + 1, :] for kk in range(TOP_K)]
    perm_w = _one_hot_positions(lpos_ref, weights)
    y_local = _rows_from_tiles(ysrt_ref, (slot,), LOCAL_ROWS).astype(BF16)
    acc = x1_ref[...] + lax.dot_general(perm_w, y_local, _TN, preferred_element_type=F32)
    ms = jnp.mean(acc * acc, axis=-1, keepdims=True)
    out_ref[...] = acc * lax.rsqrt(ms + EPS) * nw_ref[...]


def _combine(tables, lpos8, wts8, y_rows, x1, norm_final_w):
    B, S, D = x1.shape
    ts = SEQ_TILE
    ns = S // ts
    n_tiles = B * ns
    meta = pl.BlockSpec((None, SUBLANES, ts), lambda t, *_: (t // ns, 0, t % ns))
    rows = pl.BlockSpec((None, ts, D), lambda t, *_: (t // ns, t % ns, 0))
    grid_spec = pltpu.PrefetchScalarGridSpec(
        num_scalar_prefetch=len(tables),
        grid=(n_tiles,),
        in_specs=[
            meta,
            meta,
            pl.BlockSpec(memory_space=pl.ANY),
            rows,
            pl.BlockSpec((1, D), lambda t, *_: (0, 0)),
        ],
        out_specs=rows,
        scratch_shapes=[
            pltpu.VMEM((2, LOCAL_ROWS * ROW_TILES, LANES), F32),
            pltpu.SemaphoreType.DMA((2,)),
        ],
    )
    return pl.pallas_call(
        _combine_kernel,
        grid_spec=grid_spec,
        out_shape=jax.ShapeDtypeStruct((B, S, D), F32),
        compiler_params=pltpu.CompilerParams(
            dimension_semantics=("arbitrary",), vmem_limit_bytes=VMEM_LIMIT_BYTES),
        name="combine",
    )(*tables, lpos8, wts8, y_rows, x1, norm_final_w)


def _pack_layer(w_in, w_gk_up, w_spatial, b_spatial, gla_norm_w, w_router, b_router):
    ts = SEQ_TILE
    q, k, v, g, gkl, u, vs = jnp.split(w_in, [256, 512, 1024, 1536, 1552, 2064], axis=1)
    gkl = jnp.pad(gkl, ((0, 0), (0, LANES - GLA_GATE_RANK)))
    wall = jnp.concatenate([q, k, v, g, u, vs, gkl], axis=1).astype(BF16)
    wgk = jnp.pad(w_gk_up, ((0, LANES - GLA_GATE_RANK), (0, 0))).astype(BF16)
    reps = ts // SG_CHUNK
    w2 = jnp.tile(w_spatial, (1, reps, reps)).astype(BF16)
    bsb = jnp.broadcast_to(jnp.tile(b_spatial, (1, reps))[:, :, None], (SG_GROUPS, ts, SG_CH))
    glanw = jnp.tile(gla_norm_w, GLA_HEADS)[None, :]
    wr_t = w_router.T
    wr_hi = wr_t.astype(BF16)
    wr_lo = (wr_t - wr_hi.astype(F32)).astype(BF16)
    return wall, wgk, w2, bsb, glanw, wr_hi, wr_lo, b_router[:, None]


def _routing_tables(tile_counts, nb):
    tm = EXPERT_ROWS
    n = tile_counts
    counts = jnp.sum(n, axis=0)
    padded = (counts + (RUN_CHUNK - 1) + tm - 1) // tm * tm
    padded_ends = jnp.cumsum(padded)
    padded_starts = padded_ends - padded
    goff = padded_starts[None, :] + jnp.cumsum(n, axis=0) - n
    nch = (n + RUN_CHUNK - 1) // RUN_CHUNK
    loff = RUN_CHUNK * (jnp.cumsum(nch, axis=1) - nch)
    ntot = jnp.sum(nch, axis=1)
    zero_start = jnp.concatenate([(padded_starts + counts) // RUN_CHUNK * RUN_CHUNK, padded_ends[-1:]])
    zero_end = jnp.concatenate([padded_ends, jnp.full((1,), nb * tm, padded_ends.dtype)])
    zero_chunks = (zero_end - zero_start) // RUN_CHUNK
    block_start = jnp.arange(nb, dtype=jnp.int32) * tm
    block_e = jnp.minimum(jnp.sum((block_start[:, None] >= padded_ends[None, :]).astype(jnp.int32), axis=1),
                          N_EXPERTS - 1)
    n_used = padded_ends[-1:] // tm
    i32 = lambda a: a.astype(jnp.int32)
    run_tables = (i32(goff.reshape(-1)), i32(nch.reshape(-1)), i32(loff.reshape(-1)), i32(ntot))
    return run_tables, (i32(zero_start), i32(zero_chunks)), (i32(block_e), i32(n_used))


def kernel(x, norm_mix_w, w_in, w_gk_up, b_gk, gla_norm_w, sg_ln_w, sg_ln_b, w_spatial, b_spatial, w_out,
           norm_ffn_w, w_router, b_router, w_gate_up, b_gate_up, w_down, b_down, norm_final_w):
    B, S, D = x.shape
    T = B * S
    assert w_in.shape[0] == 1
    assert D == D_MODEL and S % SEQ_TILE == 0 and B % MIX_BATCH_ROWS == 0 and (T * TOP_K) % EXPERT_ROWS == 0
    tm = EXPERT_ROWS
    spare_blocks = -(-(N_EXPERTS * (RUN_CHUNK - 1)) // tm)
    nb = T * TOP_K // tm + N_EXPERTS + spare_blocks
    l = 0
    wall, wgk, w2, bsb, glanw, wr_hi, wr_lo, br = _pack_layer(
        w_in[l], w_gk_up[l], w_spatial[l], b_spatial[l], gla_norm_w[l], w_router[l], b_router[l])
    x1, h2b, lpos8, wts8, cnt = _token_mixing(
        x, norm_mix_w[l][None, :], wall, wgk, b_gk[l][None, :], glanw, sg_ln_w[l][None, :],
        sg_ln_b[l][None, :], w2, bsb, w_out[l].astype(BF16), norm_ffn_w[l][None, :], wr_hi, wr_lo, br)

    tile_counts = cnt[:, :, :, 0].reshape(T // SEQ_TILE, N_EXPERTS).astype(jnp.int32)
    run_tables, zero_tables, block_tables = _routing_tables(tile_counts, nb)
    x_rows = _dispatch(run_tables + zero_tables, h2b, lpos8, nb * tm)
    y_rows = _experts(block_tables, nb, x_rows, w_gate_up[l], b_gate_up[l][:, None, :],
                      w_down[l], b_down[l][:, None, :])
    return _combine(run_tables, lpos8, wts8, y_rows, x1, norm_final_w[None, :])
```

```python
import jax
import jax.numpy as jnp
from jax import lax
from jax.experimental import pallas as pl
from jax.experimental.pallas import tpu as pltpu

D_MODEL = 1024
GLA_HEADS = 4
GLA_DK = 64
GLA_DV = 128
GLA_QK = GLA_HEADS * GLA_DK
GLA_WIDTH = GLA_HEADS * GLA_DV
GLA_GATE_RANK = 16
GLA_GATE_NORMALIZER = 16.0
GLA_CHUNK = 64
SG_GROUPS = 4
SG_CH = 128
SG_WIDTH = SG_GROUPS * SG_CH
SG_CHUNK = 128
N_EXPERTS = 32
TOP_K = 4
D_EXPERT = D_MODEL
SWIGLU_LIMIT = 7.0
SWIGLU_ALPHA = 1.702
EPS = 1e-6

SUBLANES = 8
LANES = 128
ROW_TILES = D_MODEL // LANES
assert ROW_TILES == SUBLANES

SEQ_TILE = 256
MIX_BATCH_ROWS = 4
EXPERT_ROWS = 512
RUN_CHUNK = 8
ZERO_CHUNKS = 8
LOCAL_ROWS = 1280
assert SEQ_TILE * TOP_K + N_EXPERTS * (RUN_CHUNK - 1) <= LOCAL_ROWS
VMEM_LIMIT_BYTES = 56 * 1024 * 1024

_Q0, _K0, _V0, _G0, _U0, _VS0, _GK0 = 0, 256, 512, 1024, 1536, 2048, 2560
IN_COLS_PACKED = 2688

F32 = jnp.float32
BF16 = jnp.bfloat16
_NT = (((1,), (1,)), ((), ()))
_TN = (((0,), (0,)), ((), ()))


def _dot(a, b):
    return jnp.dot(a, b, preferred_element_type=F32)


def _split_bf16(a):
    hi = a.astype(BF16)
    lo = (a - hi.astype(F32)).astype(BF16)
    return hi, lo


def _gelu_tanh(a):
    return 0.5 * a * (1.0 + jnp.tanh(0.7978845608028654 * (a + 0.044715 * (a * a * a))))


def _rows_from_tiles(ref, lead, n_rows, base=0):
    parts = [ref[(*lead, pl.ds(base + j, n_rows, stride=ROW_TILES), slice(None))] for j in range(ROW_TILES)]
    return jnp.concatenate(parts, axis=1)


def _rows_to_tiles(ref, lead, val, n_rows, base=0):
    for j in range(ROW_TILES):
        ref[(*lead, pl.ds(base + j, n_rows, stride=ROW_TILES), slice(None))] = val[:, j * LANES:(j + 1) * LANES]


_DONE = object()


def _mix_kernel(*refs):
    state_ref = refs[-1]

    @pl.when(pl.program_id(1) == 0)
    def _():
        state_ref[...] = jnp.zeros_like(state_ref)

    tiles = [_mix_tile(ti, *refs) for ti in range(MIX_BATCH_ROWS)]
    while tiles:
        tiles = [t for t in tiles if next(t, _DONE) is not _DONE]


def _mix_tile(ti, x_ref, nmw_ref, wall_ref, wgk_ref, bgk_ref, glanw_ref, lnw_ref, lnb_ref, w2_ref, bsb_ref,
              wout_ref, nfw_ref, wrh_ref, wrl_ref, br_ref,
              x1_ref, h2_ref, lpos_ref, wts_ref, cnt_ref,
              state_ref):
    ts = SEQ_TILE
    n_chunks = ts // GLA_CHUNK

    x = x_ref[ti]
    ms = jnp.mean(x * x, axis=-1, keepdims=True)
    hb = (x * lax.rsqrt(ms + EPS) * nmw_ref[...]).astype(BF16)
    yield

    def proj(lo, hi):
        return _dot(hb, wall_ref[:, lo:hi])

    gkl = proj(_GK0, IN_COLS_PACKED)
    z = _dot(gkl.astype(BF16), wgk_ref[...]) + bgk_ref[...]
    log_a = (jnp.minimum(z, 0.0) - jnp.log1p(jnp.exp(-jnp.abs(z)))) * (1.0 / GLA_GATE_NORMALIZER)
    la_hi, la_lo = _split_bf16(log_a)
    yield

    row = lax.broadcasted_iota(jnp.int32, (ts, ts), 0)
    col = lax.broadcasted_iota(jnp.int32, (ts, ts), 1)
    row_base = (row // GLA_CHUNK) * GLA_CHUNK
    in_chunk_le = jnp.where(col <= row, jnp.where(col >= row_base, 1.0, 0.0), 0.0)
    causal = in_chunk_le > 0.5
    lower = in_chunk_le.astype(BF16)
    b_cum = _dot(lower, la_hi) + _dot(lower, la_lo)
    b_last = jnp.concatenate(
        [jnp.broadcast_to(b_cum[(c + 1) * GLA_CHUNK - 1:(c + 1) * GLA_CHUNK, :], (GLA_CHUNK, GLA_QK))
         for c in range(n_chunks)], axis=0)
    b_rest = b_last - b_cum
    yield
    chunk_ind = jnp.where(
        lax.broadcasted_iota(jnp.int32, (ts, n_chunks * LANES), 0) // GLA_CHUNK
        == lax.broadcasted_iota(jnp.int32, (ts, n_chunks * LANES), 1) // LANES, 1.0, 0.0).astype(BF16)
    b_last_t = (lax.dot_general(la_hi, chunk_ind, _TN, preferred_element_type=F32)
                + lax.dot_general(la_lo, chunk_ind, _TN, preferred_element_type=F32))
    decay_t = jnp.exp(b_last_t)
    yield

    qk = proj(_Q0, _V0)
    q = qk[:, :GLA_QK]
    k = qk[:, GLA_QK:]
    q_dec = q * (GLA_DK ** -0.5) * jnp.exp(b_cum)
    k_inv = (k * jnp.exp(-b_cum)).astype(BF16)
    k_end_t = (k * jnp.exp(b_rest)).T
    yield
    v = proj(_V0, _G0)
    vb = v.astype(BF16)
    g = proj(_G0, _U0)
    yield

    head_of_lane = lax.broadcasted_iota(jnp.int32, (ts, GLA_QK), 1) // GLA_DK
    o_heads = []
    for h in range(GLA_HEADS):
        q_h = jnp.where(head_of_lane == h, q_dec, 0.0).astype(BF16)
        sc = lax.dot_general(q_h, k_inv, _NT, preferred_element_type=F32)
        sc = jnp.where(causal, sc, 0.0).astype(BF16)
        o_heads.append(_dot(sc, vb[:, h * GLA_DV:(h + 1) * GLA_DV]))
        yield

    same_chunk = (lax.broadcasted_iota(jnp.int32, (ts, n_chunks * GLA_DV), 0) // GLA_CHUNK
                  == lax.broadcasted_iota(jnp.int32, (ts, n_chunks * GLA_DV), 1) // GLA_DV)
    upd_heads = []
    for h in range(GLA_HEADS):
        v_h = v[:, h * GLA_DV:(h + 1) * GLA_DV]
        v_by_chunk = jnp.where(same_chunk, jnp.concatenate([v_h] * n_chunks, axis=1), 0.0).astype(BF16)
        upd_heads.append(_dot(k_end_t[h * GLA_DK:(h + 1) * GLA_DK, :].astype(BF16), v_by_chunk))
        yield

    state = state_ref[ti]
    zero_blk = jnp.zeros((GLA_DK, GLA_DV), F32)
    o_inter = []
    for c in range(n_chunks):
        rows = slice(c * GLA_CHUNK, (c + 1) * GLA_CHUNK)
        o_inter.append(_dot(q_dec[rows].astype(BF16), state.astype(BF16)))
        upd = jnp.concatenate(
            [jnp.concatenate([upd_heads[h][:, c * GLA_DV:(c + 1) * GLA_DV] if h2 == h else zero_blk
                              for h2 in range(GLA_HEADS)], axis=1) for h in range(GLA_HEADS)], axis=0)
        decay_c = jnp.concatenate([decay_t[:, c * LANES:(c + 1) * LANES]] * GLA_HEADS, axis=1)
        state = decay_c * state + upd
        yield
    state_ref[ti] = state
    inter = jnp.concatenate(o_inter, axis=0)

    gla_parts = []
    for h in range(GLA_HEADS):
        o_h = o_heads[h] + inter[:, h * GLA_DV:(h + 1) * GLA_DV]
        ms_h = jnp.mean(o_h * o_h, axis=-1, keepdims=True)
        g_h = g[:, h * GLA_DV:(h + 1) * GLA_DV]
        gla_parts.append(o_h * lax.rsqrt(ms_h + EPS) * glanw_ref[:, h * GLA_DV:(h + 1) * GLA_DV]
                         * (g_h * jax.nn.sigmoid(g_h)))
        yield

    u = _gelu_tanh(proj(_U0, _VS0))
    vf = _gelu_tanh(proj(_VS0, _GK0))
    mu = jnp.mean(vf, axis=-1, keepdims=True)
    dv = vf - mu
    var = jnp.mean(dv * dv, axis=-1, keepdims=True)
    vn = (dv * lax.rsqrt(var + EPS) * lnw_ref[...] + lnb_ref[...]).astype(BF16)
    yield
    sg_base = (row // SG_CHUNK) * SG_CHUNK
    sg_mask = jnp.where(col <= row, jnp.where(col >= sg_base, 1.0, 0.0), 0.0) > 0.5
    sg_parts = []
    for gi in range(SG_GROUPS):
        w_c = jnp.where(sg_mask, w2_ref[gi], 0.0).astype(BF16)
        mixed_g = _dot(w_c, vn[:, gi * SG_CH:(gi + 1) * SG_CH]) + bsb_ref[gi]
        sg_parts.append(u[:, gi * SG_CH:(gi + 1) * SG_CH] * mixed_g)
        yield

    mixed = jnp.concatenate(gla_parts + sg_parts, axis=1).astype(BF16)
    x1 = x + _dot(mixed, wout_ref[...])
    x1_ref[ti] = x1
    yield

    ms2 = jnp.mean(x1 * x1, axis=-1, keepdims=True)
    h2 = x1 * lax.rsqrt(ms2 + EPS) * nfw_ref[...]
    h2_hi, h2_lo = _split_bf16(h2)
    h2_ref[ti] = h2_hi
    yield
    wr_hi = wrh_ref[...]
    logits = (lax.dot_general(wr_hi, h2_hi, _NT, preferred_element_type=F32)
              + lax.dot_general(wr_hi, h2_lo, _NT, preferred_element_type=F32)
              + lax.dot_general(wrl_ref[...], h2_hi, _NT, preferred_element_type=F32)
              + br_ref[...])
    yield
    e_iota = lax.broadcasted_iota(jnp.int32, (N_EXPERTS, ts), 0)
    work = logits
    top_val, top_idx = [], []
    for _ in range(TOP_K):
        m = jnp.max(work, axis=0, keepdims=True)
        i = jnp.min(jnp.where(work == m, e_iota, N_EXPERTS), axis=0, keepdims=True)
        top_val.append(m)
        top_idx.append(i)
        work = jnp.where(e_iota == i, -jnp.inf, work)
        yield
    ex = [jnp.exp(v - top_val[0]) for v in top_val]
    den = ex[0] + ex[1] + ex[2] + ex[3]
    top_w = [e / den for e in ex]
    yield

    onehot = jnp.zeros((N_EXPERTS, ts), F32)
    for i in top_idx:
        onehot = onehot + jnp.where(e_iota == i, 1.0, 0.0)
    onehot_b = onehot.astype(BF16)
    incl = jnp.where(row <= col, 1.0, 0.0).astype(BF16)
    csum = _dot(onehot_b, incl)
    total = _dot(onehot_b, jnp.ones((ts, ts), BF16))
    yield
    n_chunks_e = jnp.floor((total + (RUN_CHUNK - 1)) * (1.0 / RUN_CHUNK))
    e_lower = jnp.where(lax.broadcasted_iota(jnp.int32, (N_EXPERTS, N_EXPERTS), 1)
                        < lax.broadcasted_iota(jnp.int32, (N_EXPERTS, N_EXPERTS), 0), 1.0, 0.0).astype(BF16)
    local_start = RUN_CHUNK * _dot(e_lower, n_chunks_e.astype(BF16))
    local_pos = local_start + csum - onehot
    yield
    lpos = [jnp.sum(jnp.where(e_iota == i, local_pos, 0.0), axis=0, keepdims=True).astype(jnp.int32)
            for i in top_idx]
    cnt_ref[ti, 0] = total[:, :LANES]

    r8 = lax.broadcasted_iota(jnp.int32, (SUBLANES, ts), 0)

    def rows8(vals, fill):
        out = jnp.full((SUBLANES, ts), fill, vals[0].dtype)
        for kk, v in enumerate(vals):
            out = jnp.where(r8 == kk, v, out)
        return out

    lpos_ref[ti] = rows8(lpos, -1)
    wts_ref[ti] = rows8(top_w, 0.0)


def _token_mixing(x, nmw, wall, wgk, bgk, glanw, lnw, lnb, w2, bsb, wout, nfw, wrh, wrl, br):
    B, S, D = x.shape
    ts = SEQ_TILE
    ns = S // ts
    T = B * S

    def const(shape):
        return pl.BlockSpec(shape, lambda b, s: (0,) * len(shape))

    nbr = MIX_BATCH_ROWS
    return pl.pallas_call(
        _mix_kernel,
        grid=(B // nbr, ns),
        in_specs=[
            pl.BlockSpec((nbr, ts, D), lambda b, s: (b, s, 0)),
            const((1, D)), const((D, IN_COLS_PACKED)), const((LANES, GLA_QK)), const((1, GLA_QK)),
            const((1, GLA_WIDTH)), const((1, SG_WIDTH)), const((1, SG_WIDTH)),
            const((SG_GROUPS, ts, ts)), const((SG_GROUPS, ts, SG_CH)),
            const((D, D)), const((1, D)), const((N_EXPERTS, D)), const((N_EXPERTS, D)), const((N_EXPERTS, 1)),
        ],
        out_specs=[
            pl.BlockSpec((nbr, ts, D), lambda b, s: (b, s, 0)),
            pl.BlockSpec((nbr, ts, D), lambda b, s: (b, s, 0)),
            pl.BlockSpec((nbr, SUBLANES, ts), lambda b, s: (b, 0, s)),
            pl.BlockSpec((nbr, SUBLANES, ts), lambda b, s: (b, 0, s)),
            pl.BlockSpec((nbr, 1, N_EXPERTS, LANES), lambda b, s: (b, s, 0, 0)),
        ],
        out_shape=[
            jax.ShapeDtypeStruct((B, S, D), F32),
            jax.ShapeDtypeStruct((B, S, D), BF16),
            jax.ShapeDtypeStruct((B, SUBLANES, S), jnp.int32),
            jax.ShapeDtypeStruct((B, SUBLANES, S), F32),
            jax.ShapeDtypeStruct((B, ns, N_EXPERTS, LANES), F32),
        ],
        scratch_shapes=[pltpu.VMEM((nbr, GLA_QK, GLA_WIDTH), F32)],
        compiler_params=pltpu.CompilerParams(
            dimension_semantics=("arbitrary", "arbitrary"), vmem_limit_bytes=VMEM_LIMIT_BYTES),
        name="token_mixing",
    )(x, nmw, wall, wgk, bgk, glanw, lnw, lnb, w2, bsb, wout, nfw, wrh, wrl, br)


def _chunk_rows(start_row, n_chunks=1):
    return pl.ds(pl.multiple_of(start_row * ROW_TILES, ROW_TILES), n_chunks * (RUN_CHUNK * ROW_TILES))


def _wait_chunks(hbm_ref, sem, n_chunks):
    @pl.when(n_chunks > 0)
    def _():
        rows = pl.ds(0, n_chunks * (RUN_CHUNK * ROW_TILES))
        pltpu.make_async_copy(hbm_ref.at[rows, :], hbm_ref.at[rows, :], sem).wait()


def _one_hot_positions(lpos_ref, values):
    ts = SEQ_TILE
    p_iota = lax.broadcasted_iota(jnp.int32, (LOCAL_ROWS, ts), 0)
    out = jnp.zeros((LOCAL_ROWS, ts), F32)
    for kk in range(TOP_K):
        out = jnp.where(p_iota == lpos_ref[kk:kk + 1, :], values[kk], out)
    return out.astype(BF16)


def _dispatch_kernel(goff_ref, nch_ref, loff_ref, ntot_ref, zst_ref, znc_ref,
                     h2_ref, lpos_ref, xbuf_hbm, srt_ref, zero_ref, sem, zsem):
    s = pl.program_id(0)
    n_steps = pl.num_programs(0)
    slot = s % 2

    @pl.when(s == 0)
    def _():
        zero_ref[...] = jnp.zeros_like(zero_ref)

        def per_expert(e, total):
            n_full = znc_ref[e] // ZERO_CHUNKS
            n_rest = znc_ref[e] % ZERO_CHUNKS

            def per_copy(c, carry):
                first = zst_ref[e] + c * (ZERO_CHUNKS * RUN_CHUNK)
                pltpu.make_async_copy(zero_ref, xbuf_hbm.at[_chunk_rows(first, ZERO_CHUNKS), :], zsem).start()
                return carry
            lax.fori_loop(0, n_full, per_copy, 0)

            @pl.when(n_rest > 0)
            def _():
                first = zst_ref[e] + n_full * (ZERO_CHUNKS * RUN_CHUNK)
                pltpu.make_async_copy(zero_ref.at[_chunk_rows(0, n_rest), :],
                                      xbuf_hbm.at[_chunk_rows(first, n_rest), :], zsem).start()
            return total + znc_ref[e]
        n_zero = lax.fori_loop(0, N_EXPERTS + 1, per_expert, 0)
        _wait_chunks(xbuf_hbm, zsem, n_zero)

    ones = [1.0] * TOP_K
    perm = _one_hot_positions(lpos_ref, ones)
    local_sorted = _dot(perm, h2_ref[...])
    _rows_to_tiles(srt_ref, (slot,), local_sorted, LOCAL_ROWS)

    @pl.when(s > 0)
    def _():
        _wait_chunks(xbuf_hbm, sem.at[1 - slot], ntot_ref[s - 1])

    def per_expert(e, carry):
        g0 = goff_ref[s * N_EXPERTS + e]
        l0 = loff_ref[s * N_EXPERTS + e]

        n = nch_ref[s * N_EXPERTS + e]

        @pl.when(n > 0)
        def _():
            pltpu.make_async_copy(srt_ref.at[slot, _chunk_rows(l0, n), :],
                                  xbuf_hbm.at[_chunk_rows(g0, n), :], sem.at[slot]).start()
        return carry
    lax.fori_loop(0, N_EXPERTS, per_expert, 0)

    @pl.when(s == n_steps - 1)
    def _():
        _wait_chunks(xbuf_hbm, sem.at[slot], ntot_ref[s])


def _dispatch(tables, h2b, lpos8, n_rows_buf):
    B, S, D = h2b.shape
    ts = SEQ_TILE
    ns = S // ts
    n_tiles = B * ns
    grid_spec = pltpu.PrefetchScalarGridSpec(
        num_scalar_prefetch=len(tables),
        grid=(n_tiles,),
        in_specs=[
            pl.BlockSpec((None, ts, D), lambda t, *_: (t // ns, t % ns, 0)),
            pl.BlockSpec((None, SUBLANES, ts), lambda t, *_: (t // ns, 0, t % ns)),
        ],
        out_specs=pl.BlockSpec(memory_space=pl.ANY),
        scratch_shapes=[
            pltpu.VMEM((2, LOCAL_ROWS * ROW_TILES, LANES), F32),
            pltpu.VMEM((ZERO_CHUNKS * RUN_CHUNK * ROW_TILES, LANES), F32),
            pltpu.SemaphoreType.DMA((2,)),
            pltpu.SemaphoreType.DMA(()),
        ],
    )
    return pl.pallas_call(
        _dispatch_kernel,
        grid_spec=grid_spec,
        out_shape=jax.ShapeDtypeStruct((n_rows_buf * ROW_TILES, LANES), F32),
        compiler_params=pltpu.CompilerParams(
            dimension_semantics=("arbitrary",), vmem_limit_bytes=VMEM_LIMIT_BYTES),
        name="dispatch",
    )(*tables, h2b, lpos8)


def _expert_kernel(be_ref, nu_ref, x_ref, wgu_hbm, bgu_ref, wd_hbm, bd_ref, y_ref,
                   wgu_f_ref, wd_f_ref, wgu_b_ref, wd_b_ref, wsem):
    tm = EXPERT_ROWS
    i = pl.program_id(0)
    n_used = nu_ref[0]
    e = be_ref[i]

    def weight_copies(expert, slot_):
        return (pltpu.make_async_copy(wgu_hbm.at[expert], wgu_f_ref.at[slot_], wsem.at[0, slot_]),
                pltpu.make_async_copy(wd_hbm.at[expert], wd_f_ref.at[slot_], wsem.at[1, slot_]))

    @pl.when(i == 0)
    def _():
        for cp in weight_copies(e, e % 2):
            cp.start()

    @pl.when(i < n_used)
    def _():
        first_of_expert = jnp.logical_or(i == 0, e != be_ref[jnp.maximum(i - 1, 0)])

        @pl.when(first_of_expert)
        def _():
            slot = e % 2
            for cp in weight_copies(e, slot):
                cp.wait()

            @pl.when(e + 1 < N_EXPERTS)
            def _():
                for cp in weight_copies(e + 1, 1 - slot):
                    cp.start()

            step = 128
            for r0 in range(0, D_MODEL, step):
                wgu_b_ref[r0:r0 + step, :] = wgu_f_ref[slot, r0:r0 + step, :].astype(BF16)
                wd_b_ref[r0:r0 + step, :] = wd_f_ref[slot, r0:r0 + step, :].astype(BF16)

        xb = _rows_from_tiles(x_ref, (), tm).astype(BF16)
        gu = _dot(xb, wgu_b_ref[...]) + bgu_ref[...]
        gate = jnp.minimum(gu[:, :D_EXPERT], SWIGLU_LIMIT)
        up = jnp.clip(gu[:, D_EXPERT:], -SWIGLU_LIMIT, SWIGLU_LIMIT)
        act = ((up + 1.0) * (gate * jax.nn.sigmoid(SWIGLU_ALPHA * gate))).astype(BF16)
        y = _dot(act, wd_b_ref[...]) + bd_ref[...]
        _rows_to_tiles(y_ref, (), y, tm)

    @pl.when(i >= n_used)
    def _():
        y_ref[...] = jnp.zeros_like(y_ref)


def _experts(block_tables, nb, x_rows, w_gate_up, b_gate_up, w_down, b_down):
    tm = EXPERT_ROWS
    D = D_MODEL
    grid_spec = pltpu.PrefetchScalarGridSpec(
        num_scalar_prefetch=len(block_tables),
        grid=(nb,),
        in_specs=[
            pl.BlockSpec((tm * ROW_TILES, LANES), lambda i, be, nu:(jnp.minimum(i, nu[0] - 1), 0)),
            pl.BlockSpec(memory_space=pl.ANY),
            pl.BlockSpec((None, 1, 2 * D_EXPERT), lambda i, be, nu:(be[i], 0, 0)),
            pl.BlockSpec(memory_space=pl.ANY),
            pl.BlockSpec((None, 1, D), lambda i, be, nu:(be[i], 0, 0)),
        ],
        out_specs=pl.BlockSpec((tm * ROW_TILES, LANES), lambda i, be, nu:(i, 0)),
        scratch_shapes=[
            pltpu.VMEM((2, D, 2 * D_EXPERT), F32),
            pltpu.VMEM((2, D_EXPERT, D), F32),
            pltpu.VMEM((D, 2 * D_EXPERT), BF16),
            pltpu.VMEM((D_EXPERT, D), BF16),
            pltpu.SemaphoreType.DMA((2, 2)),
        ],
    )
    return pl.pallas_call(
        _expert_kernel,
        grid_spec=grid_spec,
        out_shape=jax.ShapeDtypeStruct((nb * tm * ROW_TILES, LANES), F32),
        compiler_params=pltpu.CompilerParams(
            dimension_semantics=("arbitrary",), vmem_limit_bytes=VMEM_LIMIT_BYTES),
        name="routed_experts",
    )(*block_tables, x_rows, w_gate_up, b_gate_up, w_down, b_down)


def _combine_kernel(goff_ref, nch_ref, loff_ref, ntot_ref,
                    lpos_ref, wts_ref, y_hbm, x1_ref, nw_ref, out_ref, ysrt_ref, sem):
    s = pl.program_id(0)
    n_steps = pl.num_programs(0)
    slot = s % 2

    def issue(tile, slot_):
        def per_expert(e, carry):
            g0 = goff_ref[tile * N_EXPERTS + e]
            l0 = loff_ref[tile * N_EXPERTS + e]

            n = nch_ref[tile * N_EXPERTS + e]

            @pl.when(n > 0)
            def _():
                pltpu.make_async_copy(y_hbm.at[_chunk_rows(g0, n), :],
                                      ysrt_ref.at[slot_, _chunk_rows(l0, n), :], sem.at[slot_]).start()
            return carry
        lax.fori_loop(0, N_EXPERTS, per_expert, 0)

    @pl.when(s == 0)
    def _():
        ysrt_ref[...] = jnp.zeros_like(ysrt_ref)
        issue(0, 0)

    @pl.when(s + 1 < n_steps)
    def _():
        issue(s + 1, 1 - slot)

    _wait_chunks(y_hbm, sem.at[slot], ntot_ref[s])

    weights = [wts_ref[kk:kk + 1, :] for kk in range(TOP_K)]
    perm_w = _one_hot_positions(lpos_ref, weights)
    y_local = _rows_from_tiles(ysrt_ref, (slot,), LOCAL_ROWS).astype(BF16)
    acc = x1_ref[...] + lax.dot_general(perm_w, y_local, _TN, preferred_element_type=F32)
    ms = jnp.mean(acc * acc, axis=-1, keepdims=True)
    out_ref[...] = acc * lax.rsqrt(ms + EPS) * nw_ref[...]


def _combine(tables, lpos8, wts8, y_rows, x1, norm_final_w):
    B, S, D = x1.shape
    ts = SEQ_TILE
    ns = S // ts
    n_tiles = B * ns
    meta = pl.BlockSpec((None, SUBLANES, ts), lambda t, *_: (t // ns, 0, t % ns))
    rows = pl.BlockSpec((None, ts, D), lambda t, *_: (t // ns, t % ns, 0))
    grid_spec = pltpu.PrefetchScalarGridSpec(
        num_scalar_prefetch=len(tables),
        grid=(n_tiles,),
        in_specs=[
            meta,
            meta,
            pl.BlockSpec(memory_space=pl.ANY),
            rows,
            pl.BlockSpec((1, D), lambda t, *_: (0, 0)),
        ],
        out_specs=rows,
        scratch_shapes=[
            pltpu.VMEM((2, LOCAL_ROWS * ROW_TILES, LANES), F32),
            pltpu.SemaphoreType.DMA((2,)),
        ],
    )
    return pl.pallas_call(
        _combine_kernel,
        grid_spec=grid_spec,
        out_shape=jax.ShapeDtypeStruct((B, S, D), F32),
        compiler_params=pltpu.CompilerParams(
            dimension_semantics=("arbitrary",), vmem_limit_bytes=VMEM_LIMIT_BYTES),
        name="combine",
    )(*tables, lpos8, wts8, y_rows, x1, norm_final_w)


def _pack_layer(w_in, w_gk_up, w_spatial, b_spatial, gla_norm_w, w_router, b_router):
    ts = SEQ_TILE
    q, k, v, g, gkl, u, vs = jnp.split(w_in, [256, 512, 1024, 1536, 1552, 2064], axis=1)
    gkl = jnp.pad(gkl, ((0, 0), (0, LANES - GLA_GATE_RANK)))
    wall = jnp.concatenate([q, k, v, g, u, vs, gkl], axis=1).astype(BF16)
    wgk = jnp.pad(w_gk_up, ((0, LANES - GLA_GATE_RANK), (0, 0))).astype(BF16)
    reps = ts // SG_CHUNK
    w2 = jnp.tile(w_spatial, (1, reps, reps)).astype(BF16)
    bsb = jnp.broadcast_to(jnp.tile(b_spatial, (1, reps))[:, :, None], (SG_GROUPS, ts, SG_CH))
    glanw = jnp.tile(gla_norm_w, GLA_HEADS)[None, :]
    wr_t = w_router.T
    wr_hi = wr_t.astype(BF16)
    wr_lo = (wr_t - wr_hi.astype(F32)).astype(BF16)
    return wall, wgk, w2, bsb, glanw, wr_hi, wr_lo, b_router[:, None]


def _routing_tables(tile_counts, nb):
    tm = EXPERT_ROWS
    n = tile_counts
    counts = jnp.sum(n, axis=0)
    padded = (counts + (RUN_CHUNK - 1) + tm - 1) // tm * tm
    padded_ends = jnp.cumsum(padded)
    padded_starts = padded_ends - padded
    goff = padded_starts[None, :] + jnp.cumsum(n, axis=0) - n
    nch = (n + RUN_CHUNK - 1) // RUN_CHUNK
    loff = RUN_CHUNK * (jnp.cumsum(nch, axis=1) - nch)
    ntot = jnp.sum(nch, axis=1)
    zero_start = jnp.concatenate([(padded_starts + counts) // RUN_CHUNK * RUN_CHUNK, padded_ends[-1:]])
    zero_end = jnp.concatenate([padded_ends, jnp.full((1,), nb * tm, padded_ends.dtype)])
    zero_chunks = (zero_end - zero_start) // RUN_CHUNK
    block_start = jnp.arange(nb, dtype=jnp.int32) * tm
    block_e = jnp.minimum(jnp.sum((block_start[:, None] >= padded_ends[None, :]).astype(jnp.int32), axis=1),
                          N_EXPERTS - 1)
    n_used = padded_ends[-1:] // tm
    i32 = lambda a: a.astype(jnp.int32)
    run_tables = (i32(goff.reshape(-1)), i32(nch.reshape(-1)), i32(loff.reshape(-1)), i32(ntot))
    return run_tables, (i32(zero_start), i32(zero_chunks)), (i32(block_e), i32(n_used))


def kernel(x, norm_mix_w, w_in, w_gk_up, b_gk, gla_norm_w, sg_ln_w, sg_ln_b, w_spatial, b_spatial, w_out,
           norm_ffn_w, w_router, b_router, w_gate_up, b_gate_up, w_down, b_down, norm_final_w):
    B, S, D = x.shape
    T = B * S
    assert w_in.shape[0] == 1
    assert D == D_MODEL and S % SEQ_TILE == 0 and B % MIX_BATCH_ROWS == 0 and (T * TOP_K) % EXPERT_ROWS == 0
    tm = EXPERT_ROWS
    spare_blocks = -(-(N_EXPERTS * (RUN_CHUNK - 1)) // tm)
    nb = T * TOP_K // tm + N_EXPERTS + spare_blocks
    l = 0
    wall, wgk, w2, bsb, glanw, wr_hi, wr_lo, br = _pack_layer(
        w_in[l], w_gk_up[l], w_spatial[l], b_spatial[l], gla_norm_w[l], w_router[l], b_router[l])
    x1, h2b, lpos8, wts8, cnt = _token_mixing(
        x, norm_mix_w[l][None, :], wall, wgk, b_gk[l][None, :], glanw, sg_ln_w[l][None, :],
        sg_ln_b[l][None, :], w2, bsb, w_out[l].astype(BF16), norm_ffn_w[l][None, :], wr_hi, wr_lo, br)

    tile_counts = cnt[:, :, :, 0].reshape(T // SEQ_TILE, N_EXPERTS).astype(jnp.int32)
    run_tables, zero_tables, block_tables = _routing_tables(tile_counts, nb)
    x_rows = _dispatch(run_tables + zero_tables, h2b, lpos8, nb * tm)
    y_rows = _experts(block_tables, nb, x_rows, w_gate_up[l], b_gate_up[l][:, None, :],
                      w_down[l], b_down[l][:, None, :])
    return _combine(run_tables, lpos8, wts8, y_rows, x1, norm_final_w[None, :])
```

```python
import jax
import jax.numpy as jnp
from jax import lax
from jax.experimental import pallas as pl
from jax.experimental.pallas import tpu as pltpu

D_MODEL = 1024
GLA_HEADS = 4
GLA_DK = 64
GLA_DV = 128
GLA_QK = GLA_HEADS * GLA_DK
GLA_WIDTH = GLA_HEADS * GLA_DV
GLA_GATE_RANK = 16
GLA_GATE_NORMALIZER = 16.0
GLA_CHUNK = 64
SG_GROUPS = 4
SG_CH = 128
SG_WIDTH = SG_GROUPS * SG_CH
SG_CHUNK = 128
N_EXPERTS = 32
TOP_K = 4
D_EXPERT = D_MODEL
SWIGLU_LIMIT = 7.0
SWIGLU_ALPHA = 1.702
EPS = 1e-6

SUBLANES = 8
LANES = 128
ROW_TILES = D_MODEL // LANES
assert ROW_TILES == SUBLANES

SEQ_TILE = 256
MIX_BATCH_ROWS = 4
EXPERT_ROWS = 512
RUN_CHUNK = 8
ZERO_CHUNKS = 8
LOCAL_ROWS = 1280
assert SEQ_TILE * TOP_K + N_EXPERTS * (RUN_CHUNK - 1) <= LOCAL_ROWS
VMEM_LIMIT_BYTES = 56 * 1024 * 1024

_Q0, _K0, _V0, _G0, _U0, _VS0, _GK0 = 0, 256, 512, 1024, 1536, 2048, 2560
IN_COLS_PACKED = 2688

F32 = jnp.float32
BF16 = jnp.bfloat16
_NT = (((1,), (1,)), ((), ()))
_TN = (((0,), (0,)), ((), ()))


def _dot(a, b):
    return jnp.dot(a, b, preferred_element_type=F32)


def _split_bf16(a):
    hi = a.astype(BF16)
    lo = (a - hi.astype(F32)).astype(BF16)
    return hi, lo


def _gelu_tanh(a):
    return 0.5 * a * (1.0 + jnp.tanh(0.7978845608028654 * (a + 0.044715 * (a * a * a))))


def _rows_from_tiles(ref, lead, n_rows, base=0):
    parts = [ref[(*lead, pl.ds(base + j, n_rows, stride=ROW_TILES), slice(None))] for j in range(ROW_TILES)]
    return jnp.concatenate(parts, axis=1)


def _rows_to_tiles(ref, lead, val, n_rows, base=0):
    for j in range(ROW_TILES):
        ref[(*lead, pl.ds(base + j, n_rows, stride=ROW_TILES), slice(None))] = val[:, j * LANES:(j + 1) * LANES]


_DONE = object()


def _mix_kernel(*refs):
    state_ref = refs[-1]

    @pl.when(pl.program_id(1) == 0)
    def _():
        state_ref[...] = jnp.zeros_like(state_ref)

    tiles = [_mix_tile(ti, *refs) for ti in range(MIX_BATCH_ROWS)]
    while tiles:
        tiles = [t for t in tiles if next(t, _DONE) is not _DONE]


def _mix_tile(ti, x_ref, nmw_ref, wall_ref, wgk_ref, bgk_ref, glanw_ref, lnw_ref, lnb_ref, w2_ref, bsb_ref,
              wout_ref, nfw_ref, wrh_ref, wrl_ref, br_ref,
              x1_ref, h2_ref, lpos_ref, wts_ref, cnt_ref,
              state_ref):
    ts = SEQ_TILE
    n_chunks = ts // GLA_CHUNK

    x = x_ref[ti]
    ms = jnp.mean(x * x, axis=-1, keepdims=True)
    hb = (x * lax.rsqrt(ms + EPS) * nmw_ref[...]).astype(BF16)
    yield

    def proj(lo, hi):
        return _dot(hb, wall_ref[:, lo:hi])

    gkl = proj(_GK0, IN_COLS_PACKED)
    z = _dot(gkl.astype(BF16), wgk_ref[...]) + bgk_ref[...]
    log_a = (jnp.minimum(z, 0.0) - jnp.log(1.0 + jnp.exp(-jnp.abs(z)))) * (1.0 / GLA_GATE_NORMALIZER)
    la_hi, la_lo = _split_bf16(log_a)
    yield

    row = lax.broadcasted_iota(jnp.int32, (ts, ts), 0)
    col = lax.broadcasted_iota(jnp.int32, (ts, ts), 1)
    row_base = (row // GLA_CHUNK) * GLA_CHUNK
    in_chunk_le = jnp.where(col <= row, jnp.where(col >= row_base, 1.0, 0.0), 0.0)
    causal = in_chunk_le > 0.5
    lower = in_chunk_le.astype(BF16)
    b_cum = _dot(lower, la_hi) + _dot(lower, la_lo)
    b_last = jnp.concatenate(
        [jnp.broadcast_to(b_cum[(c + 1) * GLA_CHUNK - 1:(c + 1) * GLA_CHUNK, :], (GLA_CHUNK, GLA_QK))
         for c in range(n_chunks)], axis=0)
    b_rest = b_last - b_cum
    yield
    chunk_ind = jnp.where(
        lax.broadcasted_iota(jnp.int32, (ts, n_chunks * LANES), 0) // GLA_CHUNK
        == lax.broadcasted_iota(jnp.int32, (ts, n_chunks * LANES), 1) // LANES, 1.0, 0.0).astype(BF16)
    b_last_t = (lax.dot_general(la_hi, chunk_ind, _TN, preferred_element_type=F32)
                + lax.dot_general(la_lo, chunk_ind, _TN, preferred_element_type=F32))
    decay_t = jnp.exp(b_last_t)
    yield

    qk = proj(_Q0, _V0)
    q = qk[:, :GLA_QK]
    k = qk[:, GLA_QK:]
    q_dec = q * (GLA_DK ** -0.5) * jnp.exp(b_cum)
    k_inv = (k * jnp.exp(-b_cum)).astype(BF16)
    k_end_t = (k * jnp.exp(b_rest)).T
    yield
    v = proj(_V0, _G0)
    vb = v.astype(BF16)
    g = proj(_G0, _U0)
    yield

    head_of_lane = lax.broadcasted_iota(jnp.int32, (ts, GLA_QK), 1) // GLA_DK
    o_heads = []
    for h in range(GLA_HEADS):
        q_h = jnp.where(head_of_lane == h, q_dec, 0.0).astype(BF16)
        sc = lax.dot_general(q_h, k_inv, _NT, preferred_element_type=F32)
        sc = jnp.where(causal, sc, 0.0).astype(BF16)
        o_heads.append(_dot(sc, vb[:, h * GLA_DV:(h + 1) * GLA_DV]))
        yield

    same_chunk = (lax.broadcasted_iota(jnp.int32, (ts, n_chunks * GLA_DV), 0) // GLA_CHUNK
                  == lax.broadcasted_iota(jnp.int32, (ts, n_chunks * GLA_DV), 1) // GLA_DV)
    upd_heads = []
    for h in range(GLA_HEADS):
        v_h = v[:, h * GLA_DV:(h + 1) * GLA_DV]
        v_by_chunk = jnp.where(same_chunk, jnp.concatenate([v_h] * n_chunks, axis=1), 0.0).astype(BF16)
        upd_heads.append(_dot(k_end_t[h * GLA_DK:(h + 1) * GLA_DK, :].astype(BF16), v_by_chunk))
        yield

    state = state_ref[ti]
    zero_blk = jnp.zeros((GLA_DK, GLA_DV), BF16)
    o_inter = []
    for c in range(n_chunks):
        rows = slice(c * GLA_CHUNK, (c + 1) * GLA_CHUNK)
        state_b = state.astype(BF16)
        state_bd = jnp.concatenate(
            [jnp.concatenate([state_b[h * GLA_DK:(h + 1) * GLA_DK] if h2 == h else zero_blk
                              for h2 in range(GLA_HEADS)], axis=1) for h in range(GLA_HEADS)], axis=0)
        o_inter.append(_dot(q_dec[rows].astype(BF16), state_bd))
        upd = jnp.concatenate([upd_heads[h][:, c * GLA_DV:(c + 1) * GLA_DV] for h in range(GLA_HEADS)], axis=0)
        state = decay_t[:, c * LANES:(c + 1) * LANES] * state + upd
        yield
    state_ref[ti] = state
    inter = jnp.concatenate(o_inter, axis=0)

    gla_parts = []
    for h in range(GLA_HEADS):
        o_h = o_heads[h] + inter[:, h * GLA_DV:(h + 1) * GLA_DV]
        ms_h = jnp.mean(o_h * o_h, axis=-1, keepdims=True)
        g_h = g[:, h * GLA_DV:(h + 1) * GLA_DV]
        gla_parts.append(o_h * lax.rsqrt(ms_h + EPS) * glanw_ref[:, h * GLA_DV:(h + 1) * GLA_DV]
                         * (g_h * jax.nn.sigmoid(g_h)))
        yield

    u = _gelu_tanh(proj(_U0, _VS0))
    vf = _gelu_tanh(proj(_VS0, _GK0))
    mu = jnp.mean(vf, axis=-1, keepdims=True)
    dv = vf - mu
    var = jnp.mean(dv * dv, axis=-1, keepdims=True)
    vn = (dv * lax.rsqrt(var + EPS) * lnw_ref[...] + lnb_ref[...]).astype(BF16)
    yield
    sg_base = (row // SG_CHUNK) * SG_CHUNK
    sg_mask = jnp.where(col <= row, jnp.where(col >= sg_base, 1.0, 0.0), 0.0) > 0.5
    sg_parts = []
    for gi in range(SG_GROUPS):
        w_c = jnp.where(sg_mask, w2_ref[gi], 0.0).astype(BF16)
        mixed_g = _dot(w_c, vn[:, gi * SG_CH:(gi + 1) * SG_CH]) + bsb_ref[gi]
        sg_parts.append(u[:, gi * SG_CH:(gi + 1) * SG_CH] * mixed_g)
        yield

    mixed = jnp.concatenate(gla_parts + sg_parts, axis=1).astype(BF16)
    x1 = x + _dot(mixed, wout_ref[...])
    x1_ref[ti] = x1
    yield

    ms2 = jnp.mean(x1 * x1, axis=-1, keepdims=True)
    h2 = x1 * lax.rsqrt(ms2 + EPS) * nfw_ref[...]
    h2_hi, h2_lo = _split_bf16(h2)
    h2_ref[ti] = h2_hi
    yield
    wr_hi = wrh_ref[...]
    logits = (lax.dot_general(wr_hi, h2_hi, _NT, preferred_element_type=F32)
              + lax.dot_general(wr_hi, h2_lo, _NT, preferred_element_type=F32)
              + lax.dot_general(wrl_ref[...], h2_hi, _NT, preferred_element_type=F32)
              + br_ref[...])
    yield
    e_iota = lax.broadcasted_iota(jnp.int32, (N_EXPERTS, ts), 0)
    work = logits
    top_val, top_idx = [], []
    for _ in range(TOP_K):
        m = jnp.max(work, axis=0, keepdims=True)
        i = jnp.min(jnp.where(work == m, e_iota, N_EXPERTS), axis=0, keepdims=True)
        top_val.append(m)
        top_idx.append(i)
        work = jnp.where(e_iota == i, -jnp.inf, work)
        yield
    ex = [jnp.exp(v - top_val[0]) for v in top_val]
    den = ex[0] + ex[1] + ex[2] + ex[3]
    top_w = [e / den for e in ex]
    yield

    onehot = jnp.zeros((N_EXPERTS, ts), F32)
    for i in top_idx:
        onehot = onehot + jnp.where(e_iota == i, 1.0, 0.0)
    onehot_b = onehot.astype(BF16)
    incl = jnp.where(row <= col, 1.0, 0.0).astype(BF16)
    csum = _dot(onehot_b, incl)
    total = _dot(onehot_b, jnp.ones((ts, ts), BF16))
    yield
    n_chunks_e = jnp.floor((total + (RUN_CHUNK - 1)) * (1.0 / RUN_CHUNK))
    e_lower = jnp.where(lax.broadcasted_iota(jnp.int32, (N_EXPERTS, N_EXPERTS), 1)
                        < lax.broadcasted_iota(jnp.int32, (N_EXPERTS, N_EXPERTS), 0), 1.0, 0.0).astype(BF16)
    local_start = RUN_CHUNK * _dot(e_lower, n_chunks_e.astype(BF16))
    local_pos = local_start + csum - onehot
    yield
    lpos = [jnp.sum(jnp.where(e_iota == i, local_pos, 0.0), axis=0, keepdims=True).astype(jnp.int32)
            for i in top_idx]
    cnt_ref[ti, 0] = total[:, :LANES]

    r8 = lax.broadcasted_iota(jnp.int32, (SUBLANES, ts), 0)

    def rows8(vals, fill):
        out = jnp.full((SUBLANES, ts), fill, vals[0].dtype)
        for kk, v in enumerate(vals):
            out = jnp.where(r8 == kk, v, out)
        return out

    lpos_ref[ti] = rows8(lpos, -1)
    wts_ref[ti] = rows8(top_w, 0.0)


def _token_mixing(x, nmw, wall, wgk, bgk, glanw, lnw, lnb, w2, bsb, wout, nfw, wrh, wrl, br):
    B, S, D = x.shape
    ts = SEQ_TILE
    ns = S // ts
    T = B * S

    def const(shape):
        return pl.BlockSpec(shape, lambda b, s: (0,) * len(shape))

    nbr = MIX_BATCH_ROWS
    return pl.pallas_call(
        _mix_kernel,
        grid=(B // nbr, ns),
        in_specs=[
            pl.BlockSpec((nbr, ts, D), lambda b, s: (b, s, 0)),
            const((1, D)), const((D, IN_COLS_PACKED)), const((LANES, GLA_QK)), const((1, GLA_QK)),
            const((1, GLA_WIDTH)), const((1, SG_WIDTH)), const((1, SG_WIDTH)),
            const((SG_GROUPS, ts, ts)), const((SG_GROUPS, ts, SG_CH)),
            const((D, D)), const((1, D)), const((N_EXPERTS, D)), const((N_EXPERTS, D)), const((N_EXPERTS, 1)),
        ],
        out_specs=[
            pl.BlockSpec((nbr, ts, D), lambda b, s: (b, s, 0)),
            pl.BlockSpec((nbr, ts, D), lambda b, s: (b, s, 0)),
            pl.BlockSpec((nbr, SUBLANES, ts), lambda b, s: (b, 0, s)),
            pl.BlockSpec((nbr, SUBLANES, ts), lambda b, s: (b, 0, s)),
            pl.BlockSpec((nbr, 1, N_EXPERTS, LANES), lambda b, s: (b, s, 0, 0)),
        ],
        out_shape=[
            jax.ShapeDtypeStruct((B, S, D), F32),
            jax.ShapeDtypeStruct((B, S, D), BF16),
            jax.ShapeDtypeStruct((B, SUBLANES, S), jnp.int32),
            jax.ShapeDtypeStruct((B, SUBLANES, S), F32),
            jax.ShapeDtypeStruct((B, ns, N_EXPERTS, LANES), F32),
        ],
        scratch_shapes=[pltpu.VMEM((nbr, GLA_QK, GLA_DV), F32)],
        compiler_params=pltpu.CompilerParams(
            dimension_semantics=("arbitrary", "arbitrary"), vmem_limit_bytes=VMEM_LIMIT_BYTES),
        name="token_mixing",
    )(x, nmw, wall, wgk, bgk, glanw, lnw, lnb, w2, bsb, wout, nfw, wrh, wrl, br)


def _chunk_rows(start_row, n_chunks=1):
    return pl.ds(pl.multiple_of(start_row * ROW_TILES, ROW_TILES), n_chunks * (RUN_CHUNK * ROW_TILES))


def _wait_chunks(hbm_ref, sem, n_chunks):
    @pl.when(n_chunks > 0)
    def _():
        rows = pl.ds(0, n_chunks * (RUN_CHUNK * ROW_TILES))
        pltpu.make_async_copy(hbm_ref.at[rows, :], hbm_ref.at[rows, :], sem).wait()


def _one_hot_positions(lpos_ref, values):
    ts = SEQ_TILE
    p_iota = lax.broadcasted_iota(jnp.int32, (LOCAL_ROWS, ts), 0)
    out = jnp.zeros((LOCAL_ROWS, ts), F32)
    for kk in range(TOP_K):
        out = jnp.where(p_iota == lpos_ref[kk:kk + 1, :], values[kk], out)
    return out.astype(BF16)


def _dispatch_kernel(goff_ref, nch_ref, loff_ref, ntot_ref, zst_ref, znc_ref,
                     h2_ref, lpos_ref, xbuf_hbm, srt_ref, zero_ref, sem, zsem):
    s = pl.program_id(0)
    n_steps = pl.num_programs(0)
    slot = s % 2

    @pl.when(s == 0)
    def _():
        zero_ref[...] = jnp.zeros_like(zero_ref)

        def per_expert(e, total):
            n_full = znc_ref[e] // ZERO_CHUNKS
            n_rest = znc_ref[e] % ZERO_CHUNKS

            def per_copy(c, carry):
                first = zst_ref[e] + c * (ZERO_CHUNKS * RUN_CHUNK)
                pltpu.make_async_copy(zero_ref, xbuf_hbm.at[_chunk_rows(first, ZERO_CHUNKS), :], zsem).start()
                return carry
            lax.fori_loop(0, n_full, per_copy, 0)

            @pl.when(n_rest > 0)
            def _():
                first = zst_ref[e] + n_full * (ZERO_CHUNKS * RUN_CHUNK)
                pltpu.make_async_copy(zero_ref.at[_chunk_rows(0, n_rest), :],
                                      xbuf_hbm.at[_chunk_rows(first, n_rest), :], zsem).start()
            return total + znc_ref[e]
        n_zero = lax.fori_loop(0, N_EXPERTS + 1, per_expert, 0)
        _wait_chunks(xbuf_hbm, zsem, n_zero)

    ones = [1.0] * TOP_K
    perm = _one_hot_positions(lpos_ref, ones)
    local_sorted = _dot(perm, h2_ref[...])
    _rows_to_tiles(srt_ref, (slot,), local_sorted, LOCAL_ROWS)

    @pl.when(s > 0)
    def _():
        _wait_chunks(xbuf_hbm, sem.at[1 - slot], ntot_ref[s - 1])

    def per_expert(e, carry):
        g0 = goff_ref[s * N_EXPERTS + e]
        l0 = loff_ref[s * N_EXPERTS + e]

        n = nch_ref[s * N_EXPERTS + e]

        @pl.when(n > 0)
        def _():
            pltpu.make_async_copy(srt_ref.at[slot, _chunk_rows(l0, n), :],
                                  xbuf_hbm.at[_chunk_rows(g0, n), :], sem.at[slot]).start()
        return carry
    lax.fori_loop(0, N_EXPERTS, per_expert, 0)

    @pl.when(s == n_steps - 1)
    def _():
        _wait_chunks(xbuf_hbm, sem.at[slot], ntot_ref[s])


def _dispatch(tables, h2b, lpos8, n_rows_buf):
    B, S, D = h2b.shape
    ts = SEQ_TILE
    ns = S // ts
    n_tiles = B * ns
    grid_spec = pltpu.PrefetchScalarGridSpec(
        num_scalar_prefetch=len(tables),
        grid=(n_tiles,),
        in_specs=[
            pl.BlockSpec((None, ts, D), lambda t, *_: (t // ns, t % ns, 0)),
            pl.BlockSpec((None, SUBLANES, ts), lambda t, *_: (t // ns, 0, t % ns)),
        ],
        out_specs=pl.BlockSpec(memory_space=pl.ANY),
        scratch_shapes=[
            pltpu.VMEM((2, LOCAL_ROWS * ROW_TILES, LANES), F32),
            pltpu.VMEM((ZERO_CHUNKS * RUN_CHUNK * ROW_TILES, LANES), F32),
            pltpu.SemaphoreType.DMA((2,)),
            pltpu.SemaphoreType.DMA(()),
        ],
    )
    return pl.pallas_call(
        _dispatch_kernel,
        grid_spec=grid_spec,
        out_shape=jax.ShapeDtypeStruct((n_rows_buf * ROW_TILES, LANES), F32),
        compiler_params=pltpu.CompilerParams(
            dimension_semantics=("arbitrary",), vmem_limit_bytes=VMEM_LIMIT_BYTES),
        name="dispatch",
    )(*tables, h2b, lpos8)


def _expert_kernel(be_ref, nu_ref, x_ref, wgu_hbm, bgu_ref, wd_hbm, bd_ref, y_ref,
                   wgu_f_ref, wd_f_ref, wgu_b_ref, wd_b_ref, wsem):
    tm = EXPERT_ROWS
    i = pl.program_id(0)
    n_used = nu_ref[0]
    e = be_ref[i]

    def weight_copies(expert, slot_):
        return (pltpu.make_async_copy(wgu_hbm.at[expert], wgu_f_ref.at[slot_], wsem.at[0, slot_]),
                pltpu.make_async_copy(wd_hbm.at[expert], wd_f_ref.at[slot_], wsem.at[1, slot_]))

    @pl.when(i == 0)
    def _():
        for cp in weight_copies(e, e % 2):
            cp.start()

    @pl.when(i < n_used)
    def _():
        first_of_expert = jnp.logical_or(i == 0, e != be_ref[jnp.maximum(i - 1, 0)])

        @pl.when(first_of_expert)
        def _():
            slot = e % 2
            for cp in weight_copies(e, slot):
                cp.wait()

            @pl.when(e + 1 < N_EXPERTS)
            def _():
                for cp in weight_copies(e + 1, 1 - slot):
                    cp.start()

            step = 128
            for r0 in range(0, D_MODEL, step):
                wgu_b_ref[r0:r0 + step, :] = wgu_f_ref[slot, r0:r0 + step, :].astype(BF16)
                wd_b_ref[r0:r0 + step, :] = wd_f_ref[slot, r0:r0 + step, :].astype(BF16)

        xb = _rows_from_tiles(x_ref, (), tm).astype(BF16)
        gu = _dot(xb, wgu_b_ref[...]) + bgu_ref[...]
        gate = jnp.minimum(gu[:, :D_EXPERT], SWIGLU_LIMIT)
        up = jnp.clip(gu[:, D_EXPERT:], -SWIGLU_LIMIT, SWIGLU_LIMIT)
        act = ((up + 1.0) * (gate * jax.nn.sigmoid(SWIGLU_ALPHA * gate))).astype(BF16)
        y = _dot(act, wd_b_ref[...]) + bd_ref[...]
        _rows_to_tiles(y_ref, (), y, tm)

    @pl.when(i >= n_used)
    def _():
        y_ref[...] = jnp.zeros_like(y_ref)


def _experts(block_tables, nb, x_rows, w_gate_up, b_gate_up, w_down, b_down):
    tm = EXPERT_ROWS
    D = D_MODEL
    grid_spec = pltpu.PrefetchScalarGridSpec(
        num_scalar_prefetch=len(block_tables),
        grid=(nb,),
        in_specs=[
            pl.BlockSpec((tm * ROW_TILES, LANES), lambda i, be, nu:(jnp.minimum(i, nu[0] - 1), 0)),
            pl.BlockSpec(memory_space=pl.ANY),
            pl.BlockSpec((None, 1, 2 * D_EXPERT), lambda i, be, nu:(be[i], 0, 0)),
            pl.BlockSpec(memory_space=pl.ANY),
            pl.BlockSpec((None, 1, D), lambda i, be, nu:(be[i], 0, 0)),
        ],
        out_specs=pl.BlockSpec((tm * ROW_TILES, LANES), lambda i, be, nu:(i, 0)),
        scratch_shapes=[
            pltpu.VMEM((2, D, 2 * D_EXPERT), F32),
            pltpu.VMEM((2, D_EXPERT, D), F32),
            pltpu.VMEM((D, 2 * D_EXPERT), BF16),
            pltpu.VMEM((D_EXPERT, D), BF16),
            pltpu.SemaphoreType.DMA((2, 2)),
        ],
    )
    return pl.pallas_call(
        _expert_kernel,
        grid_spec=grid_spec,
        out_shape=jax.ShapeDtypeStruct((nb * tm * ROW_TILES, LANES), F32),
        compiler_params=pltpu.CompilerParams(
            dimension_semantics=("arbitrary",), vmem_limit_bytes=VMEM_LIMIT_BYTES),
        name="routed_experts",
    )(*block_tables, x_rows, w_gate_up, b_gate_up, w_down, b_down)


def _combine_kernel(goff_ref, nch_ref, loff_ref, ntot_ref,
                    lpos_ref, wts_ref, y_hbm, x1_ref, nw_ref, out_ref, ysrt_ref, sem):
    s = pl.program_id(0)
    n_steps = pl.num_programs(0)
    slot = s % 2

    def issue(tile, slot_):
        def per_expert(e, carry):
            g0 = goff_ref[tile * N_EXPERTS + e]
            l0 = loff_ref[tile * N_EXPERTS + e]

            n = nch_ref[tile * N_EXPERTS + e]

            @pl.when(n > 0)
            def _():
                pltpu.make_async_copy(y_hbm.at[_chunk_rows(g0, n), :],
                                      ysrt_ref.at[slot_, _chunk_rows(l0, n), :], sem.at[slot_]).start()
            return carry
        lax.fori_loop(0, N_EXPERTS, per_expert, 0)

    @pl.when(s == 0)
    def _():
        ysrt_ref[...] = jnp.zeros_like(ysrt_ref)
        issue(0, 0)

    @pl.when(s + 1 < n_steps)
    def _():
        issue(s + 1, 1 - slot)

    _wait_chunks(y_hbm, sem.at[slot], ntot_ref[s])

    weights = [wts_ref[kk:kk + 1, :] for kk in range(TOP_K)]
    perm_w = _one_hot_positions(lpos_ref, weights)
    y_local = _rows_from_tiles(ysrt_ref, (slot,), LOCAL_ROWS).astype(BF16)
    acc = x1_ref[...] + lax.dot_general(perm_w, y_local, _TN, preferred_element_type=F32)
    ms = jnp.mean(acc * acc, axis=-1, keepdims=True)
    out_ref[...] = acc * lax.rsqrt(ms + EPS) * nw_ref[...]


def _combine(tables, lpos8, wts8, y_rows, x1, norm_final_w):
    B, S, D = x1.shape
    ts = SEQ_TILE
    ns = S // ts
    n_tiles = B * ns
    meta = pl.BlockSpec((None, SUBLANES, ts), lambda t, *_: (t // ns, 0, t % ns))
    rows = pl.BlockSpec((None, ts, D), lambda t, *_: (t // ns, t % ns, 0))
    grid_spec = pltpu.PrefetchScalarGridSpec(
        num_scalar_prefetch=len(tables),
        grid=(n_tiles,),
        in_specs=[
            meta,
            meta,
            pl.BlockSpec(memory_space=pl.ANY),
            rows,
            pl.BlockSpec((1, D), lambda t, *_: (0, 0)),
        ],
        out_specs=rows,
        scratch_shapes=[
            pltpu.VMEM((2, LOCAL_ROWS * ROW_TILES, LANES), F32),
            pltpu.SemaphoreType.DMA((2,)),
        ],
    )
    return pl.pallas_call(
        _combine_kernel,
        grid_spec=grid_spec,
        out_shape=jax.ShapeDtypeStruct((B, S, D), F32),
        compiler_params=pltpu.CompilerParams(
            dimension_semantics=("arbitrary",), vmem_limit_bytes=VMEM_LIMIT_BYTES),
        name="combine",
    )(*tables, lpos8, wts8, y_rows, x1, norm_final_w)


def _pack_layer(w_in, w_gk_up, w_spatial, b_spatial, gla_norm_w, w_router, b_router):
    ts = SEQ_TILE
    q, k, v, g, gkl, u, vs = jnp.split(w_in, [256, 512, 1024, 1536, 1552, 2064], axis=1)
    gkl = jnp.pad(gkl, ((0, 0), (0, LANES - GLA_GATE_RANK)))
    wall = jnp.concatenate([q, k, v, g, u, vs, gkl], axis=1).astype(BF16)
    wgk = jnp.pad(w_gk_up, ((0, LANES - GLA_GATE_RANK), (0, 0))).astype(BF16)
    reps = ts // SG_CHUNK
    w2 = jnp.tile(w_spatial, (1, reps, reps)).astype(BF16)
    bsb = jnp.broadcast_to(jnp.tile(b_spatial, (1, reps))[:, :, None], (SG_GROUPS, ts, SG_CH))
    glanw = jnp.tile(gla_norm_w, GLA_HEADS)[None, :]
    wr_t = w_router.T
    wr_hi = wr_t.astype(BF16)
    wr_lo = (wr_t - wr_hi.astype(F32)).astype(BF16)
    return wall, wgk, w2, bsb, glanw, wr_hi, wr_lo, b_router[:, None]


def _routing_tables(tile_counts, nb):
    tm = EXPERT_ROWS
    n = tile_counts
    counts = jnp.sum(n, axis=0)
    padded = (counts + (RUN_CHUNK - 1) + tm - 1) // tm * tm
    padded_ends = jnp.cumsum(padded)
    padded_starts = padded_ends - padded
    goff = padded_starts[None, :] + jnp.cumsum(n, axis=0) - n
    nch = (n + RUN_CHUNK - 1) // RUN_CHUNK
    loff = RUN_CHUNK * (jnp.cumsum(nch, axis=1) - nch)
    ntot = jnp.sum(nch, axis=1)
    zero_start = jnp.concatenate([(padded_starts + counts) // RUN_CHUNK * RUN_CHUNK, padded_ends[-1:]])
    zero_end = jnp.concatenate([padded_ends, jnp.full((1,), nb * tm, padded_ends.dtype)])
    zero_chunks = (zero_end - zero_start) // RUN_CHUNK
    block_start = jnp.arange(nb, dtype=jnp.int32) * tm
    block_e = jnp.minimum(jnp.sum((block_start[:, None] >= padded_ends[None, :]).astype(jnp.int32), axis=1),
                          N_EXPERTS - 1)
    n_used = padded_ends[-1:] // tm
    i32 = lambda a: a.astype(jnp.int32)
    run_tables = (i32(goff.reshape(-1)), i32(nch.reshape(-1)), i32(loff.reshape(-1)), i32(ntot))
    return run_tables, (i32(zero_start), i32(zero_chunks)), (i32(block_e), i32(n_used))


def kernel(x, norm_mix_w, w_in, w_gk_up, b_gk, gla_norm_w, sg_ln_w, sg_ln_b, w_spatial, b_spatial, w_out,
           norm_ffn_w, w_router, b_router, w_gate_up, b_gate_up, w_down, b_down, norm_final_w):
    B, S, D = x.shape
    T = B * S
    assert w_in.shape[0] == 1
    assert D == D_MODEL and S % SEQ_TILE == 0 and B % MIX_BATCH_ROWS == 0 and (T * TOP_K) % EXPERT_ROWS == 0
    tm = EXPERT_ROWS
    spare_blocks = -(-(N_EXPERTS * (RUN_CHUNK - 1)) // tm)
    nb = T * TOP_K // tm + N_EXPERTS + spare_blocks
    l = 0
    wall, wgk, w2, bsb, glanw, wr_hi, wr_lo, br = _pack_layer(
        w_in[l], w_gk_up[l], w_spatial[l], b_spatial[l], gla_norm_w[l], w_router[l], b_router[l])
    x1, h2b, lpos8, wts8, cnt = _token_mixing(
        x, norm_mix_w[l][None, :], wall, wgk, b_gk[l][None, :], glanw, sg_ln_w[l][None, :],
        sg_ln_b[l][None, :], w2, bsb, w_out[l].astype(BF16), norm_ffn_w[l][None, :], wr_hi, wr_lo, br)

    tile_counts = cnt[:, :, :, 0].reshape(T // SEQ_TILE, N_EXPERTS).astype(jnp.int32)
    run_tables, zero_tables, block_tables = _routing_tables(tile_counts, nb)
    x_rows = _dispatch(run_tables + zero_tables, h2b, lpos8, nb * tm)
    y_rows = _experts(block_tables, nb, x_rows, w_gate_up[l], b_gate_up[l][:, None, :],
                      w_down[l], b_down[l][:, None, :])
    return _combine(run_tables, lpos8, wts8, y_rows, x1, norm_final_w[None, :])
```

```python
import jax
import jax.numpy as jnp
from jax import lax
from jax.experimental import pallas as pl
from jax.experimental.pallas import tpu as pltpu

D_MODEL = 1024
GLA_HEADS = 4
GLA_DK = 64
GLA_DV = 128
GLA_QK = GLA_HEADS * GLA_DK
GLA_WIDTH = GLA_HEADS * GLA_DV
GLA_GATE_RANK = 16
GLA_GATE_NORMALIZER = 16.0
GLA_CHUNK = 64
SG_GROUPS = 4
SG_CH = 128
SG_WIDTH = SG_GROUPS * SG_CH
SG_CHUNK = 128
N_EXPERTS = 32
TOP_K = 4
D_EXPERT = D_MODEL
SWIGLU_LIMIT = 7.0
SWIGLU_ALPHA = 1.702
EPS = 1e-6

SUBLANES = 8
LANES = 128
ROW_TILES = D_MODEL // LANES
assert ROW_TILES == SUBLANES

SEQ_TILE = 256
MIX_BATCH_ROWS = 4
EXPERT_ROWS = 512
RUN_CHUNK = 8
ZERO_CHUNKS = 8
LOCAL_ROWS = 1280
assert SEQ_TILE * TOP_K + N_EXPERTS * (RUN_CHUNK - 1) <= LOCAL_ROWS
VMEM_LIMIT_BYTES = 56 * 1024 * 1024

_Q0, _K0, _V0, _G0, _U0, _VS0, _GK0 = 0, 256, 512, 1024, 1536, 2048, 2560
IN_COLS_PACKED = 2688

F32 = jnp.float32
BF16 = jnp.bfloat16
_NT = (((1,), (1,)), ((), ()))
_TN = (((0,), (0,)), ((), ()))


def _dot(a, b):
    return jnp.dot(a, b, preferred_element_type=F32)


def _split_bf16(a):
    hi = a.astype(BF16)
    lo = (a - hi.astype(F32)).astype(BF16)
    return hi, lo


def _gelu_tanh(a):
    return 0.5 * a * (1.0 + jnp.tanh(0.7978845608028654 * (a + 0.044715 * (a * a * a))))


def _rows_from_tiles(ref, lead, n_rows, base=0):
    parts = [ref[(*lead, pl.ds(base + j, n_rows, stride=ROW_TILES), slice(None))] for j in range(ROW_TILES)]
    return jnp.concatenate(parts, axis=1)


def _rows_to_tiles(ref, lead, val, n_rows, base=0):
    for j in range(ROW_TILES):
        ref[(*lead, pl.ds(base + j, n_rows, stride=ROW_TILES), slice(None))] = val[:, j * LANES:(j + 1) * LANES]


_DONE = object()


def _mix_kernel(*refs):
    state_ref = refs[-1]

    @pl.when(pl.program_id(1) == 0)
    def _():
        state_ref[...] = jnp.zeros_like(state_ref)

    tiles = [_mix_tile(ti, *refs) for ti in range(MIX_BATCH_ROWS)]
    while tiles:
        tiles = [t for t in tiles if next(t, _DONE) is not _DONE]


def _mix_tile(ti, x_ref, nmw_ref, wall_ref, wgk_ref, bgk_ref, glanw_ref, lnw_ref, lnb_ref, w2_ref, bsb_ref,
              wout_ref, nfw_ref, wrh_ref, wrl_ref, br_ref,
              x1_ref, h2_ref, lpos_ref, wts_ref, cnt_ref,
              state_ref):
    ts = SEQ_TILE
    n_chunks = ts // GLA_CHUNK

    x = x_ref[ti]
    ms = jnp.mean(x * x, axis=-1, keepdims=True)
    hb = (x * lax.rsqrt(ms + EPS) * nmw_ref[...]).astype(BF16)
    yield

    def proj(lo, hi):
        return _dot(hb, wall_ref[:, lo:hi])

    gkl = proj(_GK0, IN_COLS_PACKED)
    z = _dot(gkl.astype(BF16), wgk_ref[...]) + bgk_ref[...]
    log_a = (jnp.minimum(z, 0.0) - jnp.log(1.0 + jnp.exp(-jnp.abs(z)))) * (1.0 / GLA_GATE_NORMALIZER)
    la_hi, la_lo = _split_bf16(log_a)
    yield

    row = lax.broadcasted_iota(jnp.int32, (ts, ts), 0)
    col = lax.broadcasted_iota(jnp.int32, (ts, ts), 1)
    row_base = (row // GLA_CHUNK) * GLA_CHUNK
    in_chunk_le = jnp.where(col <= row, jnp.where(col >= row_base, 1.0, 0.0), 0.0)
    causal = in_chunk_le > 0.5
    lower = in_chunk_le.astype(BF16)
    b_cum = _dot(lower, la_hi) + _dot(lower, la_lo)
    b_last = jnp.concatenate(
        [jnp.broadcast_to(b_cum[(c + 1) * GLA_CHUNK - 1:(c + 1) * GLA_CHUNK, :], (GLA_CHUNK, GLA_QK))
         for c in range(n_chunks)], axis=0)
    b_rest = b_last - b_cum
    yield
    assert 2 * n_chunks == SUBLANES
    r8 = lax.broadcasted_iota(jnp.int32, (SUBLANES, GLA_QK), 0)
    ends = jnp.zeros((SUBLANES, GLA_QK), F32)
    for c in range(n_chunks):
        end_row = b_cum[(c + 1) * GLA_CHUNK - 1:(c + 1) * GLA_CHUNK, :]
        end_hi = end_row.astype(BF16).astype(F32)
        ends = jnp.where(r8 == c, end_hi, jnp.where(r8 == n_chunks + c, end_row - end_hi, ends))
    chunk_sel = jnp.where(
        lax.broadcasted_iota(jnp.int32, (SUBLANES, n_chunks * LANES), 0) % n_chunks
        == lax.broadcasted_iota(jnp.int32, (SUBLANES, n_chunks * LANES), 1) // LANES, 1.0, 0.0).astype(BF16)
    b_last_t = lax.dot_general(ends.astype(BF16), chunk_sel, _TN, preferred_element_type=F32)
    decay_t = jnp.exp(b_last_t)
    yield

    qk = proj(_Q0, _V0)
    q = qk[:, :GLA_QK]
    k = qk[:, GLA_QK:]
    q_dec = q * (GLA_DK ** -0.5) * jnp.exp(b_cum)
    k_inv = (k * jnp.exp(-b_cum)).astype(BF16)
    k_end_t = (k * jnp.exp(b_rest)).T
    yield
    v = proj(_V0, _G0)
    vb = v.astype(BF16)
    g = proj(_G0, _U0)
    yield

    head_of_lane = lax.broadcasted_iota(jnp.int32, (ts, GLA_QK), 1) // GLA_DK
    o_heads = []
    for h in range(GLA_HEADS):
        q_h = jnp.where(head_of_lane == h, q_dec, 0.0).astype(BF16)
        sc = lax.dot_general(q_h, k_inv, _NT, preferred_element_type=F32)
        sc = jnp.where(causal, sc, 0.0).astype(BF16)
        o_heads.append(_dot(sc, vb[:, h * GLA_DV:(h + 1) * GLA_DV]))
        yield

    same_chunk = (lax.broadcasted_iota(jnp.int32, (ts, n_chunks * GLA_DV), 0) // GLA_CHUNK
                  == lax.broadcasted_iota(jnp.int32, (ts, n_chunks * GLA_DV), 1) // GLA_DV)
    upd_heads = []
    for h in range(GLA_HEADS):
        v_h = v[:, h * GLA_DV:(h + 1) * GLA_DV]
        v_by_chunk = jnp.where(same_chunk, jnp.concatenate([v_h] * n_chunks, axis=1), 0.0).astype(BF16)
        upd_heads.append(_dot(k_end_t[h * GLA_DK:(h + 1) * GLA_DK, :].astype(BF16), v_by_chunk))
        yield

    state = state_ref[ti]
    zero_blk = jnp.zeros((GLA_DK, GLA_DV), BF16)
    o_inter = []
    for c in range(n_chunks):
        rows = slice(c * GLA_CHUNK, (c + 1) * GLA_CHUNK)
        state_b = state.astype(BF16)
        state_bd = jnp.concatenate(
            [jnp.concatenate([state_b[h * GLA_DK:(h + 1) * GLA_DK] if h2 == h else zero_blk
                              for h2 in range(GLA_HEADS)], axis=1) for h in range(GLA_HEADS)], axis=0)
        o_inter.append(_dot(q_dec[rows].astype(BF16), state_bd))
        upd = jnp.concatenate([upd_heads[h][:, c * GLA_DV:(c + 1) * GLA_DV] for h in range(GLA_HEADS)], axis=0)
        state = decay_t[:, c * LANES:(c + 1) * LANES] * state + upd
        yield
    state_ref[ti] = state
    inter = jnp.concatenate(o_inter, axis=0)

    gla_parts = []
    for h in range(GLA_HEADS):
        o_h = o_heads[h] + inter[:, h * GLA_DV:(h + 1) * GLA_DV]
        ms_h = jnp.mean(o_h * o_h, axis=-1, keepdims=True)
        g_h = g[:, h * GLA_DV:(h + 1) * GLA_DV]
        gla_parts.append(o_h * lax.rsqrt(ms_h + EPS) * glanw_ref[:, h * GLA_DV:(h + 1) * GLA_DV]
                         * (g_h * jax.nn.sigmoid(g_h)))
        yield

    u = _gelu_tanh(proj(_U0, _VS0))
    vf = _gelu_tanh(proj(_VS0, _GK0))
    mu = jnp.mean(vf, axis=-1, keepdims=True)
    dv = vf - mu
    var = jnp.mean(dv * dv, axis=-1, keepdims=True)
    vn = (dv * lax.rsqrt(var + EPS) * lnw_ref[...] + lnb_ref[...]).astype(BF16)
    yield
    sg_base = (row // SG_CHUNK) * SG_CHUNK
    sg_mask = jnp.where(col <= row, jnp.where(col >= sg_base, 1.0, 0.0), 0.0) > 0.5
    sg_parts = []
    for gi in range(SG_GROUPS):
        w_c = jnp.where(sg_mask, w2_ref[gi], 0.0).astype(BF16)
        mixed_g = _dot(w_c, vn[:, gi * SG_CH:(gi + 1) * SG_CH]) + bsb_ref[gi]
        sg_parts.append(u[:, gi * SG_CH:(gi + 1) * SG_CH] * mixed_g)
        yield

    mixed = jnp.concatenate(gla_parts + sg_parts, axis=1).astype(BF16)
    x1 = x + _dot(mixed, wout_ref[...])
    x1_ref[ti] = x1
    yield

    ms2 = jnp.mean(x1 * x1, axis=-1, keepdims=True)
    h2 = x1 * lax.rsqrt(ms2 + EPS) * nfw_ref[...]
    h2_hi, h2_lo = _split_bf16(h2)
    h2_ref[ti] = h2_hi
    yield
    wr_hi = wrh_ref[...]
    logits = (lax.dot_general(wr_hi, h2_hi, _NT, preferred_element_type=F32)
              + lax.dot_general(wr_hi, h2_lo, _NT, preferred_element_type=F32)
              + lax.dot_general(wrl_ref[...], h2_hi, _NT, preferred_element_type=F32)
              + br_ref[...])
    yield
    e_iota = lax.broadcasted_iota(jnp.int32, (N_EXPERTS, ts), 0)
    work = logits
    top_val, top_idx = [], []
    for _ in range(TOP_K):
        m = jnp.max(work, axis=0, keepdims=True)
        i = jnp.min(jnp.where(work == m, e_iota, N_EXPERTS), axis=0, keepdims=True)
        top_val.append(m)
        top_idx.append(i)
        work = jnp.where(e_iota == i, -jnp.inf, work)
        yield
    ex = [jnp.exp(v - top_val[0]) for v in top_val]
    den = ex[0] + ex[1] + ex[2] + ex[3]
    top_w = [e / den for e in ex]
    yield

    onehot = jnp.zeros((N_EXPERTS, ts), F32)
    for i in top_idx:
        onehot = onehot + jnp.where(e_iota == i, 1.0, 0.0)
    onehot_b = onehot.astype(BF16)
    incl = jnp.where(row <= col, 1.0, 0.0).astype(BF16)
    csum = _dot(onehot_b, incl)
    total = _dot(onehot_b, jnp.ones((ts, ts), BF16))
    yield
    n_chunks_e = jnp.floor((total + (RUN_CHUNK - 1)) * (1.0 / RUN_CHUNK))
    e_lower = jnp.where(lax.broadcasted_iota(jnp.int32, (N_EXPERTS, N_EXPERTS), 1)
                        < lax.broadcasted_iota(jnp.int32, (N_EXPERTS, N_EXPERTS), 0), 1.0, 0.0).astype(BF16)
    local_start = RUN_CHUNK * _dot(e_lower, n_chunks_e.astype(BF16))
    local_pos = local_start + csum - onehot
    yield
    lpos = [jnp.sum(jnp.where(e_iota == i, local_pos, 0.0), axis=0, keepdims=True).astype(jnp.int32)
            for i in top_idx]
    cnt_ref[ti, 0] = total[:, :LANES]

    r8 = lax.broadcasted_iota(jnp.int32, (SUBLANES, ts), 0)

    def rows8(vals, fill):
        out = jnp.full((SUBLANES, ts), fill, vals[0].dtype)
        for kk, v in enumerate(vals):
            out = jnp.where(r8 == kk, v, out)
        return out

    lpos_ref[ti] = rows8(lpos, -1)
    wts_ref[ti] = rows8(top_w, 0.0)


def _token_mixing(x, nmw, wall, wgk, bgk, glanw, lnw, lnb, w2, bsb, wout, nfw, wrh, wrl, br):
    B, S, D = x.shape
    ts = SEQ_TILE
    ns = S // ts
    T = B * S

    def const(shape):
        return pl.BlockSpec(shape, lambda b, s: (0,) * len(shape))

    nbr = MIX_BATCH_ROWS
    return pl.pallas_call(
        _mix_kernel,
        grid=(B // nbr, ns),
        in_specs=[
            pl.BlockSpec((nbr, ts, D), lambda b, s: (b, s, 0)),
            const((1, D)), const((D, IN_COLS_PACKED)), const((LANES, GLA_QK)), const((1, GLA_QK)),
            const((1, GLA_WIDTH)), const((1, SG_WIDTH)), const((1, SG_WIDTH)),
            const((SG_GROUPS, ts, ts)), const((SG_GROUPS, ts, SG_CH)),
            const((D, D)), const((1, D)), const((N_EXPERTS, D)), const((N_EXPERTS, D)), const((N_EXPERTS, 1)),
        ],
        out_specs=[
            pl.BlockSpec((nbr, ts, D), lambda b, s: (b, s, 0)),
            pl.BlockSpec((nbr, ts, D), lambda b, s: (b, s, 0)),
            pl.BlockSpec((nbr, SUBLANES, ts), lambda b, s: (b, 0, s)),
            pl.BlockSpec((nbr, SUBLANES, ts), lambda b, s: (b, 0, s)),
            pl.BlockSpec((nbr, 1, N_EXPERTS, LANES), lambda b, s: (b, s, 0, 0)),
        ],
        out_shape=[
            jax.ShapeDtypeStruct((B, S, D), F32),
            jax.ShapeDtypeStruct((B, S, D), BF16),
            jax.ShapeDtypeStruct((B, SUBLANES, S), jnp.int32),
            jax.ShapeDtypeStruct((B, SUBLANES, S), F32),
            jax.ShapeDtypeStruct((B, ns, N_EXPERTS, LANES), F32),
        ],
        scratch_shapes=[pltpu.VMEM((nbr, GLA_QK, GLA_DV), F32)],
        compiler_params=pltpu.CompilerParams(
            dimension_semantics=("arbitrary", "arbitrary"), vmem_limit_bytes=VMEM_LIMIT_BYTES),
        name="token_mixing",
    )(x, nmw, wall, wgk, bgk, glanw, lnw, lnb, w2, bsb, wout, nfw, wrh, wrl, br)


def _chunk_rows(start_row, n_chunks=1):
    return pl.ds(pl.multiple_of(start_row * ROW_TILES, ROW_TILES), n_chunks * (RUN_CHUNK * ROW_TILES))


def _wait_chunks(hbm_ref, sem, n_chunks):
    @pl.when(n_chunks > 0)
    def _():
        rows = pl.ds(0, n_chunks * (RUN_CHUNK * ROW_TILES))
        pltpu.make_async_copy(hbm_ref.at[rows, :], hbm_ref.at[rows, :], sem).wait()


def _one_hot_positions(lpos_ref, values):
    ts = SEQ_TILE
    p_iota = lax.broadcasted_iota(jnp.int32, (LOCAL_ROWS, ts), 0)
    out = jnp.zeros((LOCAL_ROWS, ts), F32)
    for kk in range(TOP_K):
        out = jnp.where(p_iota == lpos_ref[kk:kk + 1, :], values[kk], out)
    return out.astype(BF16)


def _dispatch_kernel(goff_ref, nch_ref, loff_ref, ntot_ref, zst_ref, znc_ref,
                     h2_ref, lpos_ref, xbuf_hbm, srt_ref, zero_ref, sem, zsem):
    s = pl.program_id(0)
    n_steps = pl.num_programs(0)
    slot = s % 2

    @pl.when(s == 0)
    def _():
        zero_ref[...] = jnp.zeros_like(zero_ref)

        def per_expert(e, total):
            n_full = znc_ref[e] // ZERO_CHUNKS
            n_rest = znc_ref[e] % ZERO_CHUNKS

            def per_copy(c, carry):
                first = zst_ref[e] + c * (ZERO_CHUNKS * RUN_CHUNK)
                pltpu.make_async_copy(zero_ref, xbuf_hbm.at[_chunk_rows(first, ZERO_CHUNKS), :], zsem).start()
                return carry
            lax.fori_loop(0, n_full, per_copy, 0)

            @pl.when(n_rest > 0)
            def _():
                first = zst_ref[e] + n_full * (ZERO_CHUNKS * RUN_CHUNK)
                pltpu.make_async_copy(zero_ref.at[_chunk_rows(0, n_rest), :],
                                      xbuf_hbm.at[_chunk_rows(first, n_rest), :], zsem).start()
            return total + znc_ref[e]
        n_zero = lax.fori_loop(0, N_EXPERTS + 1, per_expert, 0)
        _wait_chunks(xbuf_hbm, zsem, n_zero)

    ones = [1.0] * TOP_K
    perm = _one_hot_positions(lpos_ref, ones)
    local_sorted = _dot(perm, h2_ref[...])
    _rows_to_tiles(srt_ref, (slot,), local_sorted, LOCAL_ROWS)

    @pl.when(s > 0)
    def _():
        _wait_chunks(xbuf_hbm, sem.at[1 - slot], ntot_ref[s - 1])

    def per_expert(e, carry):
        g0 = goff_ref[s * N_EXPERTS + e]
        l0 = loff_ref[s * N_EXPERTS + e]

        n = nch_ref[s * N_EXPERTS + e]

        @pl.when(n > 0)
        def _():
            pltpu.make_async_copy(srt_ref.at[slot, _chunk_rows(l0, n), :],
                                  xbuf_hbm.at[_chunk_rows(g0, n), :], sem.at[slot]).start()
        return carry
    lax.fori_loop(0, N_EXPERTS, per_expert, 0)

    @pl.when(s == n_steps - 1)
    def _():
        _wait_chunks(xbuf_hbm, sem.at[slot], ntot_ref[s])


def _dispatch(tables, h2b, lpos8, n_rows_buf):
    B, S, D = h2b.shape
    ts = SEQ_TILE
    ns = S // ts
    n_tiles = B * ns
    grid_spec = pltpu.PrefetchScalarGridSpec(
        num_scalar_prefetch=len(tables),
        grid=(n_tiles,),
        in_specs=[
            pl.BlockSpec((None, ts, D), lambda t, *_: (t // ns, t % ns, 0)),
            pl.BlockSpec((None, SUBLANES, ts), lambda t, *_: (t // ns, 0, t % ns)),
        ],
        out_specs=pl.BlockSpec(memory_space=pl.ANY),
        scratch_shapes=[
            pltpu.VMEM((2, LOCAL_ROWS * ROW_TILES, LANES), F32),
            pltpu.VMEM((ZERO_CHUNKS * RUN_CHUNK * ROW_TILES, LANES), F32),
            pltpu.SemaphoreType.DMA((2,)),
            pltpu.SemaphoreType.DMA(()),
        ],
    )
    return pl.pallas_call(
        _dispatch_kernel,
        grid_spec=grid_spec,
        out_shape=jax.ShapeDtypeStruct((n_rows_buf * ROW_TILES, LANES), F32),
        compiler_params=pltpu.CompilerParams(
            dimension_semantics=("arbitrary",), vmem_limit_bytes=VMEM_LIMIT_BYTES),
        name="dispatch",
    )(*tables, h2b, lpos8)


def _expert_kernel(be_ref, nu_ref, nv_ref, x_ref, wgu_hbm, bgu_ref, wd_hbm, bd_ref, y_ref,
                   wgu_f_ref, wd_f_ref, wgu_b_ref, wd_b_ref, wsem):
    tm = EXPERT_ROWS
    i = pl.program_id(0)
    n_used = nu_ref[0]
    e = be_ref[i]

    def weight_copies(expert, slot_):
        return (pltpu.make_async_copy(wgu_hbm.at[expert], wgu_f_ref.at[slot_], wsem.at[0, slot_]),
                pltpu.make_async_copy(wd_hbm.at[expert], wd_f_ref.at[slot_], wsem.at[1, slot_]))

    @pl.when(i == 0)
    def _():
        for cp in weight_copies(e, e % 2):
            cp.start()

    @pl.when(i < n_used)
    def _():
        first_of_expert = jnp.logical_or(i == 0, e != be_ref[jnp.maximum(i - 1, 0)])

        @pl.when(first_of_expert)
        def _():
            slot = e % 2
            for cp in weight_copies(e, slot):
                cp.wait()

            @pl.when(e + 1 < N_EXPERTS)
            def _():
                for cp in weight_copies(e + 1, 1 - slot):
                    cp.start()

            step = 128
            for r0 in range(0, D_MODEL, step):
                wgu_b_ref[r0:r0 + step, :] = wgu_f_ref[slot, r0:r0 + step, :].astype(BF16)
                wd_b_ref[r0:r0 + step, :] = wd_f_ref[slot, r0:r0 + step, :].astype(BF16)

        def mlp(n_rows):
            xb = _rows_from_tiles(x_ref, (), n_rows).astype(BF16)
            gu = _dot(xb, wgu_b_ref[...]) + bgu_ref[...]
            gate = jnp.minimum(gu[:, :D_EXPERT], SWIGLU_LIMIT)
            up = jnp.clip(gu[:, D_EXPERT:], -SWIGLU_LIMIT, SWIGLU_LIMIT)
            act = ((up + 1.0) * (gate * jax.nn.sigmoid(SWIGLU_ALPHA * gate))).astype(BF16)
            y = _dot(act, wd_b_ref[...]) + bd_ref[...]
            _rows_to_tiles(y_ref, (), y, n_rows)

        half = tm // 2

        @pl.when(nv_ref[i] > half)
        def _():
            mlp(tm)

        @pl.when(nv_ref[i] <= half)
        def _():
            mlp(half)
            y_ref[half * ROW_TILES:, :] = jnp.zeros((half * ROW_TILES, LANES), F32)

    @pl.when(i >= n_used)
    def _():
        y_ref[...] = jnp.zeros_like(y_ref)


def _experts(block_tables, nb, x_rows, w_gate_up, b_gate_up, w_down, b_down):
    tm = EXPERT_ROWS
    D = D_MODEL
    grid_spec = pltpu.PrefetchScalarGridSpec(
        num_scalar_prefetch=len(block_tables),
        grid=(nb,),
        in_specs=[
            pl.BlockSpec((tm * ROW_TILES, LANES), lambda i, be, nu, nv:(jnp.minimum(i, nu[0] - 1), 0)),
            pl.BlockSpec(memory_space=pl.ANY),
            pl.BlockSpec((None, 1, 2 * D_EXPERT), lambda i, be, nu, nv:(be[i], 0, 0)),
            pl.BlockSpec(memory_space=pl.ANY),
            pl.BlockSpec((None, 1, D), lambda i, be, nu, nv:(be[i], 0, 0)),
        ],
        out_specs=pl.BlockSpec((tm * ROW_TILES, LANES), lambda i, be, nu, nv:(i, 0)),
        scratch_shapes=[
            pltpu.VMEM((2, D, 2 * D_EXPERT), F32),
            pltpu.VMEM((2, D_EXPERT, D), F32),
            pltpu.VMEM((D, 2 * D_EXPERT), BF16),
            pltpu.VMEM((D_EXPERT, D), BF16),
            pltpu.SemaphoreType.DMA((2, 2)),
        ],
    )
    return pl.pallas_call(
        _expert_kernel,
        grid_spec=grid_spec,
        out_shape=jax.ShapeDtypeStruct((nb * tm * ROW_TILES, LANES), F32),
        compiler_params=pltpu.CompilerParams(
            dimension_semantics=("arbitrary",), vmem_limit_bytes=VMEM_LIMIT_BYTES),
        name="routed_experts",
    )(*block_tables, x_rows, w_gate_up, b_gate_up, w_down, b_down)


def _combine_kernel(goff_ref, nch_ref, loff_ref, ntot_ref,
                    lpos_ref, wts_ref, y_hbm, x1_ref, nw_ref, out_ref, ysrt_ref, sem):
    s = pl.program_id(0)
    n_steps = pl.num_programs(0)
    slot = s % 2

    def issue(tile, slot_):
        def per_expert(e, carry):
            g0 = goff_ref[tile * N_EXPERTS + e]
            l0 = loff_ref[tile * N_EXPERTS + e]

            n = nch_ref[tile * N_EXPERTS + e]

            @pl.when(n > 0)
            def _():
                pltpu.make_async_copy(y_hbm.at[_chunk_rows(g0, n), :],
                                      ysrt_ref.at[slot_, _chunk_rows(l0, n), :], sem.at[slot_]).start()
            return carry
        lax.fori_loop(0, N_EXPERTS, per_expert, 0)

    @pl.when(s == 0)
    def _():
        ysrt_ref[...] = jnp.zeros_like(ysrt_ref)
        issue(0, 0)

    @pl.when(s + 1 < n_steps)
    def _():
        issue(s + 1, 1 - slot)

    _wait_chunks(y_hbm, sem.at[slot], ntot_ref[s])

    weights = [wts_ref[kk:kk + 1, :] for kk in range(TOP_K)]
    perm_w = _one_hot_positions(lpos_ref, weights)
    y_local = _rows_from_tiles(ysrt_ref, (slot,), LOCAL_ROWS).astype(BF16)
    acc = x1_ref[...] + lax.dot_general(perm_w, y_local, _TN, preferred_element_type=F32)
    ms = jnp.mean(acc * acc, axis=-1, keepdims=True)
    out_ref[...] = acc * lax.rsqrt(ms + EPS) * nw_ref[...]


def _combine(tables, lpos8, wts8, y_rows, x1, norm_final_w):
    B, S, D = x1.shape
    ts = SEQ_TILE
    ns = S // ts
    n_tiles = B * ns
    meta = pl.BlockSpec((None, SUBLANES, ts), lambda t, *_: (t // ns, 0, t % ns))
    rows = pl.BlockSpec((None, ts, D), lambda t, *_: (t // ns, t % ns, 0))
    grid_spec = pltpu.PrefetchScalarGridSpec(
        num_scalar_prefetch=len(tables),
        grid=(n_tiles,),
        in_specs=[
            meta,
            meta,
            pl.BlockSpec(memory_space=pl.ANY),
            rows,
            pl.BlockSpec((1, D), lambda t, *_: (0, 0)),
        ],
        out_specs=rows,
        scratch_shapes=[
            pltpu.VMEM((2, LOCAL_ROWS * ROW_TILES, LANES), F32),
            pltpu.SemaphoreType.DMA((2,)),
        ],
    )
    return pl.pallas_call(
        _combine_kernel,
        grid_spec=grid_spec,
        out_shape=jax.ShapeDtypeStruct((B, S, D), F32),
        compiler_params=pltpu.CompilerParams(
            dimension_semantics=("arbitrary",), vmem_limit_bytes=VMEM_LIMIT_BYTES),
        name="combine",
    )(*tables, lpos8, wts8, y_rows, x1, norm_final_w)


def _pack_layer(w_in, w_gk_up, w_spatial, b_spatial, gla_norm_w, w_router, b_router):
    ts = SEQ_TILE
    q, k, v, g, gkl, u, vs = jnp.split(w_in, [256, 512, 1024, 1536, 1552, 2064], axis=1)
    gkl = jnp.pad(gkl, ((0, 0), (0, LANES - GLA_GATE_RANK)))
    wall = jnp.concatenate([q, k, v, g, u, vs, gkl], axis=1).astype(BF16)
    wgk = jnp.pad(w_gk_up, ((0, LANES - GLA_GATE_RANK), (0, 0))).astype(BF16)
    reps = ts // SG_CHUNK
    w2 = jnp.tile(w_spatial, (1, reps, reps)).astype(BF16)
    bsb = jnp.broadcast_to(jnp.tile(b_spatial, (1, reps))[:, :, None], (SG_GROUPS, ts, SG_CH))
    glanw = jnp.tile(gla_norm_w, GLA_HEADS)[None, :]
    wr_t = w_router.T
    wr_hi = wr_t.astype(BF16)
    wr_lo = (wr_t - wr_hi.astype(F32)).astype(BF16)
    return wall, wgk, w2, bsb, glanw, wr_hi, wr_lo, b_router[:, None]


def _routing_tables(tile_counts, nb):
    tm = EXPERT_ROWS
    n = tile_counts
    counts = jnp.sum(n, axis=0)
    padded = (counts + (RUN_CHUNK - 1) + tm - 1) // tm * tm
    padded_ends = jnp.cumsum(padded)
    padded_starts = padded_ends - padded
    goff = padded_starts[None, :] + jnp.cumsum(n, axis=0) - n
    nch = (n + RUN_CHUNK - 1) // RUN_CHUNK
    loff = RUN_CHUNK * (jnp.cumsum(nch, axis=1) - nch)
    ntot = jnp.sum(nch, axis=1)
    zero_start = jnp.concatenate([(padded_starts + counts) // RUN_CHUNK * RUN_CHUNK, padded_ends[-1:]])
    zero_end = jnp.concatenate([padded_ends, jnp.full((1,), nb * tm, padded_ends.dtype)])
    zero_chunks = (zero_end - zero_start) // RUN_CHUNK
    block_start = jnp.arange(nb, dtype=jnp.int32) * tm
    block_e = jnp.minimum(jnp.sum((block_start[:, None] >= padded_ends[None, :]).astype(jnp.int32), axis=1),
                          N_EXPERTS - 1)
    n_used = padded_ends[-1:] // tm
    real_end = padded_starts + counts
    is_e = block_e[:, None] == jnp.arange(N_EXPERTS, dtype=jnp.int32)[None, :]
    block_rows = jnp.clip(jnp.sum(jnp.where(is_e, real_end[None, :], 0), axis=1) - block_start, 0, tm)
    i32 = lambda a: a.astype(jnp.int32)
    run_tables = (i32(goff.reshape(-1)), i32(nch.reshape(-1)), i32(loff.reshape(-1)), i32(ntot))
    return run_tables, (i32(zero_start), i32(zero_chunks)), (i32(block_e), i32(n_used), i32(block_rows))


def kernel(x, norm_mix_w, w_in, w_gk_up, b_gk, gla_norm_w, sg_ln_w, sg_ln_b, w_spatial, b_spatial, w_out,
           norm_ffn_w, w_router, b_router, w_gate_up, b_gate_up, w_down, b_down, norm_final_w):
    B, S, D = x.shape
    T = B * S
    assert w_in.shape[0] == 1
    assert D == D_MODEL and S % SEQ_TILE == 0 and B % MIX_BATCH_ROWS == 0 and (T * TOP_K) % EXPERT_ROWS == 0
    tm = EXPERT_ROWS
    spare_blocks = -(-(N_EXPERTS * (RUN_CHUNK - 1)) // tm)
    nb = T * TOP_K // tm + N_EXPERTS + spare_blocks
    l = 0
    wall, wgk, w2, bsb, glanw, wr_hi, wr_lo, br = _pack_layer(
        w_in[l], w_gk_up[l], w_spatial[l], b_spatial[l], gla_norm_w[l], w_router[l], b_router[l])
    x1, h2b, lpos8, wts8, cnt = _token_mixing(
        x, norm_mix_w[l][None, :], wall, wgk, b_gk[l][None, :], glanw, sg_ln_w[l][None, :],
        sg_ln_b[l][None, :], w2, bsb, w_out[l].astype(BF16), norm_ffn_w[l][None, :], wr_hi, wr_lo, br)

    tile_counts = cnt[:, :, :, 0].reshape(T // SEQ_TILE, N_EXPERTS).astype(jnp.int32)
    run_tables, zero_tables, block_tables = _routing_tables(tile_counts, nb)
    x_rows = _dispatch(run_tables + zero_tables, h2b, lpos8, nb * tm)
    y_rows = _experts(block_tables, nb, x_rows, w_gate_up[l], b_gate_up[l][:, None, :],
                      w_down[l], b_down[l][:, None, :])
    return _combine(run_tables, lpos8, wts8, y_rows, x1, norm_final_w[None, :])
```

```python
import jax
import jax.numpy as jnp
from jax import lax
from jax.experimental import pallas as pl
from jax.experimental.pallas import tpu as pltpu

D_MODEL = 1024
GLA_HEADS = 4
GLA_DK = 64
GLA_DV = 128
GLA_QK = GLA_HEADS * GLA_DK
GLA_WIDTH = GLA_HEADS * GLA_DV
GLA_GATE_RANK = 16
GLA_GATE_NORMALIZER = 16.0
GLA_CHUNK = 64
SG_GROUPS = 4
SG_CH = 128
SG_WIDTH = SG_GROUPS * SG_CH
SG_CHUNK = 128
N_EXPERTS = 32
TOP_K = 4
D_EXPERT = D_MODEL
SWIGLU_LIMIT = 7.0
SWIGLU_ALPHA = 1.702
EPS = 1e-6

SUBLANES = 8
LANES = 128
ROW_TILES = D_MODEL // LANES
assert ROW_TILES == SUBLANES

SEQ_TILE = 256
MIX_BATCH_ROWS = 4
EXPERT_ROWS = 512
RUN_CHUNK = 1
ZERO_CHUNKS = 64
LOCAL_ROWS = SEQ_TILE * TOP_K
assert SEQ_TILE * TOP_K + N_EXPERTS * (RUN_CHUNK - 1) <= LOCAL_ROWS
VMEM_LIMIT_BYTES = 56 * 1024 * 1024

_Q0, _K0, _V0, _G0, _U0, _VS0, _GK0 = 0, 256, 512, 1024, 1536, 2048, 2560
IN_COLS_PACKED = 2688

F32 = jnp.float32
BF16 = jnp.bfloat16
_NT = (((1,), (1,)), ((), ()))
_TN = (((0,), (0,)), ((), ()))


def _dot(a, b):
    return jnp.dot(a, b, preferred_element_type=F32)


def _split_bf16(a):
    hi = a.astype(BF16)
    lo = (a - hi.astype(F32)).astype(BF16)
    return hi, lo


def _gelu_tanh(a):
    return 0.5 * a * (1.0 + jnp.tanh(0.7978845608028654 * (a + 0.044715 * (a * a * a))))


def _rows_from_tiles(ref, lead, n_rows, base=0):
    parts = [ref[(*lead, pl.ds(base + j, n_rows, stride=ROW_TILES), slice(None))] for j in range(ROW_TILES)]
    return jnp.concatenate(parts, axis=1)


def _rows_to_tiles(ref, lead, val, n_rows, base=0):
    for j in range(ROW_TILES):
        ref[(*lead, pl.ds(base + j, n_rows, stride=ROW_TILES), slice(None))] = val[:, j * LANES:(j + 1) * LANES]


_DONE = object()


def _mix_kernel(*refs):
    state_ref = refs[-1]

    @pl.when(pl.program_id(1) == 0)
    def _():
        state_ref[...] = jnp.zeros_like(state_ref)

    tiles = [_mix_tile(ti, *refs) for ti in range(MIX_BATCH_ROWS)]
    while tiles:
        tiles = [t for t in tiles if next(t, _DONE) is not _DONE]


def _mix_tile(ti, x_ref, nmw_ref, wall_ref, wgk_ref, bgk_ref, glanw_ref, lnw_ref, lnb_ref, w2_ref, bsb_ref,
              wout_ref, nfw_ref, wrh_ref, wrl_ref, br_ref,
              x1_ref, h2_ref, lpos_ref, wts_ref, cnt_ref,
              state_ref):
    ts = SEQ_TILE
    n_chunks = ts // GLA_CHUNK

    x = x_ref[ti]
    ms = jnp.mean(x * x, axis=-1, keepdims=True)
    hb = (x * lax.rsqrt(ms + EPS) * nmw_ref[...]).astype(BF16)
    yield

    def proj(lo, hi):
        return _dot(hb, wall_ref[:, lo:hi])

    gkl = proj(_GK0, IN_COLS_PACKED)
    z = _dot(gkl.astype(BF16), wgk_ref[...]) + bgk_ref[...]
    log_a = (jnp.minimum(z, 0.0) - jnp.log(1.0 + jnp.exp(-jnp.abs(z)))) * (1.0 / GLA_GATE_NORMALIZER)
    la_hi, la_lo = _split_bf16(log_a)
    yield

    row = lax.broadcasted_iota(jnp.int32, (ts, ts), 0)
    col = lax.broadcasted_iota(jnp.int32, (ts, ts), 1)
    row_base = (row // GLA_CHUNK) * GLA_CHUNK
    in_chunk_le = jnp.where(col <= row, jnp.where(col >= row_base, 1.0, 0.0), 0.0)
    causal = in_chunk_le > 0.5
    lower = in_chunk_le.astype(BF16)
    b_cum = _dot(lower, la_hi) + _dot(lower, la_lo)
    b_last = jnp.concatenate(
        [jnp.broadcast_to(b_cum[(c + 1) * GLA_CHUNK - 1:(c + 1) * GLA_CHUNK, :], (GLA_CHUNK, GLA_QK))
         for c in range(n_chunks)], axis=0)
    b_rest = b_last - b_cum
    yield
    assert 2 * n_chunks == SUBLANES
    r8 = lax.broadcasted_iota(jnp.int32, (SUBLANES, GLA_QK), 0)
    ends = jnp.zeros((SUBLANES, GLA_QK), F32)
    for c in range(n_chunks):
        end_row = b_cum[(c + 1) * GLA_CHUNK - 1:(c + 1) * GLA_CHUNK, :]
        end_hi = end_row.astype(BF16).astype(F32)
        ends = jnp.where(r8 == c, end_hi, jnp.where(r8 == n_chunks + c, end_row - end_hi, ends))
    chunk_sel = jnp.where(
        lax.broadcasted_iota(jnp.int32, (SUBLANES, n_chunks * LANES), 0) % n_chunks
        == lax.broadcasted_iota(jnp.int32, (SUBLANES, n_chunks * LANES), 1) // LANES, 1.0, 0.0).astype(BF16)
    b_last_t = lax.dot_general(ends.astype(BF16), chunk_sel, _TN, preferred_element_type=F32)
    decay_t = jnp.exp(b_last_t)
    yield

    qk = proj(_Q0, _V0)
    q = qk[:, :GLA_QK]
    k = qk[:, GLA_QK:]
    q_dec = q * (GLA_DK ** -0.5) * jnp.exp(b_cum)
    k_inv = (k * jnp.exp(-b_cum)).astype(BF16)
    k_end_t = (k * jnp.exp(b_rest)).T
    yield
    v = proj(_V0, _G0)
    vb = v.astype(BF16)
    g = proj(_G0, _U0)
    yield

    head_of_lane = lax.broadcasted_iota(jnp.int32, (ts, GLA_QK), 1) // GLA_DK
    o_heads = []
    for h in range(GLA_HEADS):
        q_h = jnp.where(head_of_lane == h, q_dec, 0.0).astype(BF16)
        sc = lax.dot_general(q_h, k_inv, _NT, preferred_element_type=F32)
        sc = jnp.where(causal, sc, 0.0).astype(BF16)
        o_heads.append(_dot(sc, vb[:, h * GLA_DV:(h + 1) * GLA_DV]))
        yield

    same_chunk = (lax.broadcasted_iota(jnp.int32, (ts, n_chunks * GLA_DV), 0) // GLA_CHUNK
                  == lax.broadcasted_iota(jnp.int32, (ts, n_chunks * GLA_DV), 1) // GLA_DV)
    upd_heads = []
    for h in range(GLA_HEADS):
        v_h = v[:, h * GLA_DV:(h + 1) * GLA_DV]
        v_by_chunk = jnp.where(same_chunk, jnp.concatenate([v_h] * n_chunks, axis=1), 0.0).astype(BF16)
        upd_heads.append(_dot(k_end_t[h * GLA_DK:(h + 1) * GLA_DK, :].astype(BF16), v_by_chunk))
        yield

    state = state_ref[ti]
    zero_blk = jnp.zeros((GLA_DK, GLA_DV), BF16)
    o_inter = []
    for c in range(n_chunks):
        rows = slice(c * GLA_CHUNK, (c + 1) * GLA_CHUNK)
        state_b = state.astype(BF16)
        state_bd = jnp.concatenate(
            [jnp.concatenate([state_b[h * GLA_DK:(h + 1) * GLA_DK] if h2 == h else zero_blk
                              for h2 in range(GLA_HEADS)], axis=1) for h in range(GLA_HEADS)], axis=0)
        o_inter.append(_dot(q_dec[rows].astype(BF16), state_bd))
        upd = jnp.concatenate([upd_heads[h][:, c * GLA_DV:(c + 1) * GLA_DV] for h in range(GLA_HEADS)], axis=0)
        state = decay_t[:, c * LANES:(c + 1) * LANES] * state + upd
        yield
    state_ref[ti] = state
    inter = jnp.concatenate(o_inter, axis=0)

    gla_parts = []
    for h in range(GLA_HEADS):
        o_h = o_heads[h] + inter[:, h * GLA_DV:(h + 1) * GLA_DV]
        ms_h = jnp.mean(o_h * o_h, axis=-1, keepdims=True)
        g_h = g[:, h * GLA_DV:(h + 1) * GLA_DV]
        gla_parts.append(o_h * lax.rsqrt(ms_h + EPS) * glanw_ref[:, h * GLA_DV:(h + 1) * GLA_DV]
                         * (g_h * jax.nn.sigmoid(g_h)))
        yield

    u = _gelu_tanh(proj(_U0, _VS0))
    vf = _gelu_tanh(proj(_VS0, _GK0))
    mu = jnp.mean(vf, axis=-1, keepdims=True)
    dv = vf - mu
    var = jnp.mean(dv * dv, axis=-1, keepdims=True)
    vn = (dv * lax.rsqrt(var + EPS) * lnw_ref[...] + lnb_ref[...]).astype(BF16)
    yield
    sg_base = (row // SG_CHUNK) * SG_CHUNK
    sg_mask = jnp.where(col <= row, jnp.where(col >= sg_base, 1.0, 0.0), 0.0) > 0.5
    sg_parts = []
    for gi in range(SG_GROUPS):
        w_c = jnp.where(sg_mask, w2_ref[gi], 0.0).astype(BF16)
        mixed_g = _dot(w_c, vn[:, gi * SG_CH:(gi + 1) * SG_CH]) + bsb_ref[gi]
        sg_parts.append(u[:, gi * SG_CH:(gi + 1) * SG_CH] * mixed_g)
        yield

    mixed = jnp.concatenate(gla_parts + sg_parts, axis=1).astype(BF16)
    x1 = x + _dot(mixed, wout_ref[...])
    x1_ref[ti] = x1
    yield

    ms2 = jnp.mean(x1 * x1, axis=-1, keepdims=True)
    h2 = x1 * lax.rsqrt(ms2 + EPS) * nfw_ref[...]
    h2_hi, h2_lo = _split_bf16(h2)
    h2_ref[ti] = h2_hi
    yield
    wr_hi = wrh_ref[...]
    logits = (lax.dot_general(wr_hi, h2_hi, _NT, preferred_element_type=F32)
              + lax.dot_general(wr_hi, h2_lo, _NT, preferred_element_type=F32)
              + lax.dot_general(wrl_ref[...], h2_hi, _NT, preferred_element_type=F32)
              + br_ref[...])
    yield
    e_iota = lax.broadcasted_iota(jnp.int32, (N_EXPERTS, ts), 0)
    work = logits
    top_val, top_idx = [], []
    for _ in range(TOP_K):
        m = jnp.max(work, axis=0, keepdims=True)
        i = jnp.min(jnp.where(work == m, e_iota, N_EXPERTS), axis=0, keepdims=True)
        top_val.append(m)
        top_idx.append(i)
        work = jnp.where(e_iota == i, -jnp.inf, work)
        yield
    ex = [jnp.exp(v - top_val[0]) for v in top_val]
    den = ex[0] + ex[1] + ex[2] + ex[3]
    top_w = [e / den for e in ex]
    yield

    onehot = jnp.zeros((N_EXPERTS, ts), F32)
    for i in top_idx:
        onehot = onehot + jnp.where(e_iota == i, 1.0, 0.0)
    onehot_b = onehot.astype(BF16)
    incl = jnp.where(row <= col, 1.0, 0.0).astype(BF16)
    csum = _dot(onehot_b, incl)
    total = _dot(onehot_b, jnp.ones((ts, ts), BF16))
    yield
    n_chunks_e = jnp.floor((total + (RUN_CHUNK - 1)) * (1.0 / RUN_CHUNK))
    e_lower = jnp.where(lax.broadcasted_iota(jnp.int32, (N_EXPERTS, N_EXPERTS), 1)
                        < lax.broadcasted_iota(jnp.int32, (N_EXPERTS, N_EXPERTS), 0), 1.0, 0.0).astype(BF16)
    local_start = RUN_CHUNK * _dot(e_lower, n_chunks_e.astype(BF16))
    local_pos = local_start + csum - onehot
    yield
    lpos = [jnp.sum(jnp.where(e_iota == i, local_pos, 0.0), axis=0, keepdims=True).astype(jnp.int32)
            for i in top_idx]
    cnt_ref[ti, 0] = total[:, :LANES]

    r8 = lax.broadcasted_iota(jnp.int32, (SUBLANES, ts), 0)

    def rows8(vals, fill):
        out = jnp.full((SUBLANES, ts), fill, vals[0].dtype)
        for kk, v in enumerate(vals):
            out = jnp.where(r8 == kk, v, out)
        return out

    lpos_ref[ti] = rows8(lpos, -1)
    wts_ref[ti] = rows8(top_w, 0.0)


def _token_mixing(x, nmw, wall, wgk, bgk, glanw, lnw, lnb, w2, bsb, wout, nfw, wrh, wrl, br):
    B, S, D = x.shape
    ts = SEQ_TILE
    ns = S // ts
    T = B * S

    def const(shape):
        return pl.BlockSpec(shape, lambda b, s: (0,) * len(shape))

    nbr = MIX_BATCH_ROWS
    return pl.pallas_call(
        _mix_kernel,
        grid=(B // nbr, ns),
        in_specs=[
            pl.BlockSpec((nbr, ts, D), lambda b, s: (b, s, 0)),
            const((1, D)), const((D, IN_COLS_PACKED)), const((LANES, GLA_QK)), const((1, GLA_QK)),
            const((1, GLA_WIDTH)), const((1, SG_WIDTH)), const((1, SG_WIDTH)),
            const((SG_GROUPS, ts, ts)), const((SG_GROUPS, ts, SG_CH)),
            const((D, D)), const((1, D)), const((N_EXPERTS, D)), const((N_EXPERTS, D)), const((N_EXPERTS, 1)),
        ],
        out_specs=[
            pl.BlockSpec((nbr, ts, D), lambda b, s: (b, s, 0)),
            pl.BlockSpec((nbr, ts, D), lambda b, s: (b, s, 0)),
            pl.BlockSpec((nbr, SUBLANES, ts), lambda b, s: (b, 0, s)),
            pl.BlockSpec((nbr, SUBLANES, ts), lambda b, s: (b, 0, s)),
            pl.BlockSpec((nbr, 1, N_EXPERTS, LANES), lambda b, s: (b, s, 0, 0)),
        ],
        out_shape=[
            jax.ShapeDtypeStruct((B, S, D), F32),
            jax.ShapeDtypeStruct((B, S, D), BF16),
            jax.ShapeDtypeStruct((B, SUBLANES, S), jnp.int32),
            jax.ShapeDtypeStruct((B, SUBLANES, S), F32),
            jax.ShapeDtypeStruct((B, ns, N_EXPERTS, LANES), F32),
        ],
        scratch_shapes=[pltpu.VMEM((nbr, GLA_QK, GLA_DV), F32)],
        compiler_params=pltpu.CompilerParams(
            dimension_semantics=("arbitrary", "arbitrary"), vmem_limit_bytes=VMEM_LIMIT_BYTES),
        name="token_mixing",
    )(x, nmw, wall, wgk, bgk, glanw, lnw, lnb, w2, bsb, wout, nfw, wrh, wrl, br)


def _chunk_rows(start_row, n_chunks=1):
    return pl.ds(pl.multiple_of(start_row * ROW_TILES, ROW_TILES), n_chunks * (RUN_CHUNK * ROW_TILES))


def _wait_chunks(hbm_ref, sem, n_chunks):
    @pl.when(n_chunks > 0)
    def _():
        rows = pl.ds(0, n_chunks * (RUN_CHUNK * ROW_TILES))
        pltpu.make_async_copy(hbm_ref.at[rows, :], hbm_ref.at[rows, :], sem).wait()


def _one_hot_positions(lpos_ref, values):
    ts = SEQ_TILE
    p_iota = lax.broadcasted_iota(jnp.int32, (LOCAL_ROWS, ts), 0)
    out = jnp.zeros((LOCAL_ROWS, ts), F32)
    for kk in range(TOP_K):
        out = jnp.where(p_iota == lpos_ref[kk:kk + 1, :], values[kk], out)
    return out.astype(BF16)


def _dispatch_kernel(goff_ref, nch_ref, loff_ref, ntot_ref, zst_ref, znc_ref,
                     h2_ref, lpos_ref, xbuf_hbm, srt_ref, zero_ref, sem, zsem):
    s = pl.program_id(0)
    n_steps = pl.num_programs(0)
    slot = s % 2

    @pl.when(s == 0)
    def _():
        zero_ref[...] = jnp.zeros_like(zero_ref)

        def per_expert(e, total):
            n_full = znc_ref[e] // ZERO_CHUNKS
            n_rest = znc_ref[e] % ZERO_CHUNKS

            def per_copy(c, carry):
                first = zst_ref[e] + c * (ZERO_CHUNKS * RUN_CHUNK)
                pltpu.make_async_copy(zero_ref, xbuf_hbm.at[_chunk_rows(first, ZERO_CHUNKS), :], zsem).start()
                return carry
            lax.fori_loop(0, n_full, per_copy, 0)

            @pl.when(n_rest > 0)
            def _():
                first = zst_ref[e] + n_full * (ZERO_CHUNKS * RUN_CHUNK)
                pltpu.make_async_copy(zero_ref.at[_chunk_rows(0, n_rest), :],
                                      xbuf_hbm.at[_chunk_rows(first, n_rest), :], zsem).start()
            return total + znc_ref[e]
        n_zero = lax.fori_loop(0, N_EXPERTS + 1, per_expert, 0)
        _wait_chunks(xbuf_hbm, zsem, n_zero)

    ones = [1.0] * TOP_K
    perm = _one_hot_positions(lpos_ref, ones)
    local_sorted = _dot(perm, h2_ref[...])

    assert RUN_CHUNK == 1

    @pl.when(s >= 2)
    def _():
        _wait_chunks(xbuf_hbm, sem.at[slot], ntot_ref[s - 2])
    _rows_to_tiles(srt_ref, (slot,), local_sorted, LOCAL_ROWS)

    def per_expert(e, carry):
        g0 = goff_ref[s * N_EXPERTS + e]
        l0 = loff_ref[s * N_EXPERTS + e]

        n = nch_ref[s * N_EXPERTS + e]

        @pl.when(n > 0)
        def _():
            pltpu.make_async_copy(srt_ref.at[slot, _chunk_rows(l0, n), :],
                                  xbuf_hbm.at[_chunk_rows(g0, n), :], sem.at[slot]).start()
        return carry
    lax.fori_loop(0, N_EXPERTS, per_expert, 0)

    @pl.when(s == n_steps - 1)
    def _():
        @pl.when(s >= 1)
        def _():
            _wait_chunks(xbuf_hbm, sem.at[1 - slot], ntot_ref[s - 1])
        _wait_chunks(xbuf_hbm, sem.at[slot], ntot_ref[s])


def _dispatch(tables, h2b, lpos8, n_rows_buf):
    B, S, D = h2b.shape
    ts = SEQ_TILE
    ns = S // ts
    n_tiles = B * ns
    grid_spec = pltpu.PrefetchScalarGridSpec(
        num_scalar_prefetch=len(tables),
        grid=(n_tiles,),
        in_specs=[
            pl.BlockSpec((None, ts, D), lambda t, *_: (t // ns, t % ns, 0)),
            pl.BlockSpec((None, SUBLANES, ts), lambda t, *_: (t // ns, 0, t % ns)),
        ],
        out_specs=pl.BlockSpec(memory_space=pl.ANY),
        scratch_shapes=[
            pltpu.VMEM((2, LOCAL_ROWS * ROW_TILES, LANES), F32),
            pltpu.VMEM((ZERO_CHUNKS * RUN_CHUNK * ROW_TILES, LANES), F32),
            pltpu.SemaphoreType.DMA((2,)),
            pltpu.SemaphoreType.DMA(()),
        ],
    )
    return pl.pallas_call(
        _dispatch_kernel,
        grid_spec=grid_spec,
        out_shape=jax.ShapeDtypeStruct((n_rows_buf * ROW_TILES, LANES), F32),
        compiler_params=pltpu.CompilerParams(
            dimension_semantics=("arbitrary",), vmem_limit_bytes=VMEM_LIMIT_BYTES),
        name="dispatch",
    )(*tables, h2b, lpos8)


def _expert_kernel(be_ref, nu_ref, nv_ref, x_ref, wgu_hbm, bgu_ref, wd_hbm, bd_ref, y_ref,
                   wgu_f_ref, wd_f_ref, wgu_b_ref, wd_b_ref, wsem):
    tm = EXPERT_ROWS
    i = pl.program_id(0)
    n_used = nu_ref[0]
    e = be_ref[i]

    def weight_copies(expert, slot_):
        return (pltpu.make_async_copy(wgu_hbm.at[expert], wgu_f_ref.at[slot_], wsem.at[0, slot_]),
                pltpu.make_async_copy(wd_hbm.at[expert], wd_f_ref.at[slot_], wsem.at[1, slot_]))

    @pl.when(i == 0)
    def _():
        for cp in weight_copies(e, e % 2):
            cp.start()

    @pl.when(i < n_used)
    def _():
        first_of_expert = jnp.logical_or(i == 0, e != be_ref[jnp.maximum(i - 1, 0)])

        @pl.when(first_of_expert)
        def _():
            slot = e % 2
            for cp in weight_copies(e, slot):
                cp.wait()

            @pl.when(e + 1 < N_EXPERTS)
            def _():
                for cp in weight_copies(e + 1, 1 - slot):
                    cp.start()

            step = 128
            for r0 in range(0, D_MODEL, step):
                wgu_b_ref[r0:r0 + step, :] = wgu_f_ref[slot, r0:r0 + step, :].astype(BF16)
                wd_b_ref[r0:r0 + step, :] = wd_f_ref[slot, r0:r0 + step, :].astype(BF16)

        def mlp(n_rows):
            xb = _rows_from_tiles(x_ref, (), n_rows).astype(BF16)
            gu = _dot(xb, wgu_b_ref[...]) + bgu_ref[...]
            gate = jnp.minimum(gu[:, :D_EXPERT], SWIGLU_LIMIT)
            up = jnp.clip(gu[:, D_EXPERT:], -SWIGLU_LIMIT, SWIGLU_LIMIT)
            act = ((up + 1.0) * (gate * jax.nn.sigmoid(SWIGLU_ALPHA * gate))).astype(BF16)
            y = _dot(act, wd_b_ref[...]) + bd_ref[...]
            _rows_to_tiles(y_ref, (), y, n_rows)

        half = tm // 2

        @pl.when(nv_ref[i] > half)
        def _():
            mlp(tm)

        @pl.when(nv_ref[i] <= half)
        def _():
            mlp(half)
            y_ref[half * ROW_TILES:, :] = jnp.zeros((half * ROW_TILES, LANES), F32)

    @pl.when(i >= n_used)
    def _():
        y_ref[...] = jnp.zeros_like(y_ref)


def _experts(block_tables, nb, x_rows, w_gate_up, b_gate_up, w_down, b_down):
    tm = EXPERT_ROWS
    D = D_MODEL
    grid_spec = pltpu.PrefetchScalarGridSpec(
        num_scalar_prefetch=len(block_tables),
        grid=(nb,),
        in_specs=[
            pl.BlockSpec((tm * ROW_TILES, LANES), lambda i, be, nu, nv:(jnp.minimum(i, nu[0] - 1), 0)),
            pl.BlockSpec(memory_space=pl.ANY),
            pl.BlockSpec((None, 1, 2 * D_EXPERT), lambda i, be, nu, nv:(be[i], 0, 0)),
            pl.BlockSpec(memory_space=pl.ANY),
            pl.BlockSpec((None, 1, D), lambda i, be, nu, nv:(be[i], 0, 0)),
        ],
        out_specs=pl.BlockSpec((tm * ROW_TILES, LANES), lambda i, be, nu, nv:(i, 0)),
        scratch_shapes=[
            pltpu.VMEM((2, D, 2 * D_EXPERT), F32),
            pltpu.VMEM((2, D_EXPERT, D), F32),
            pltpu.VMEM((D, 2 * D_EXPERT), BF16),
            pltpu.VMEM((D_EXPERT, D), BF16),
            pltpu.SemaphoreType.DMA((2, 2)),
        ],
    )
    return pl.pallas_call(
        _expert_kernel,
        grid_spec=grid_spec,
        out_shape=jax.ShapeDtypeStruct((nb * tm * ROW_TILES, LANES), F32),
        compiler_params=pltpu.CompilerParams(
            dimension_semantics=("arbitrary",), vmem_limit_bytes=VMEM_LIMIT_BYTES),
        name="routed_experts",
    )(*block_tables, x_rows, w_gate_up, b_gate_up, w_down, b_down)


def _combine_kernel(goff_ref, nch_ref, loff_ref, ntot_ref,
                    lpos_ref, wts_ref, y_hbm, x1_ref, nw_ref, out_ref, ysrt_ref, sem):
    s = pl.program_id(0)
    n_steps = pl.num_programs(0)
    slot = s % 2

    def issue(tile, slot_):
        def per_expert(e, carry):
            g0 = goff_ref[tile * N_EXPERTS + e]
            l0 = loff_ref[tile * N_EXPERTS + e]

            n = nch_ref[tile * N_EXPERTS + e]

            @pl.when(n > 0)
            def _():
                pltpu.make_async_copy(y_hbm.at[_chunk_rows(g0, n), :],
                                      ysrt_ref.at[slot_, _chunk_rows(l0, n), :], sem.at[slot_]).start()
            return carry
        lax.fori_loop(0, N_EXPERTS, per_expert, 0)

    @pl.when(s == 0)
    def _():
        issue(0, 0)

    @pl.when(s + 1 < n_steps)
    def _():
        issue(s + 1, 1 - slot)

    _wait_chunks(y_hbm, sem.at[slot], ntot_ref[s])

    weights = [wts_ref[kk:kk + 1, :] for kk in range(TOP_K)]
    perm_w = _one_hot_positions(lpos_ref, weights)
    y_local = _rows_from_tiles(ysrt_ref, (slot,), LOCAL_ROWS).astype(BF16)
    acc = x1_ref[...] + lax.dot_general(perm_w, y_local, _TN, preferred_element_type=F32)
    ms = jnp.mean(acc * acc, axis=-1, keepdims=True)
    out_ref[...] = acc * lax.rsqrt(ms + EPS) * nw_ref[...]


def _combine(tables, lpos8, wts8, y_rows, x1, norm_final_w):
    B, S, D = x1.shape
    ts = SEQ_TILE
    ns = S // ts
    n_tiles = B * ns
    meta = pl.BlockSpec((None, SUBLANES, ts), lambda t, *_: (t // ns, 0, t % ns))
    rows = pl.BlockSpec((None, ts, D), lambda t, *_: (t // ns, t % ns, 0))
    grid_spec = pltpu.PrefetchScalarGridSpec(
        num_scalar_prefetch=len(tables),
        grid=(n_tiles,),
        in_specs=[
            meta,
            meta,
            pl.BlockSpec(memory_space=pl.ANY),
            rows,
            pl.BlockSpec((1, D), lambda t, *_: (0, 0)),
        ],
        out_specs=rows,
        scratch_shapes=[
            pltpu.VMEM((2, LOCAL_ROWS * ROW_TILES, LANES), F32),
            pltpu.SemaphoreType.DMA((2,)),
        ],
    )
    return pl.pallas_call(
        _combine_kernel,
        grid_spec=grid_spec,
        out_shape=jax.ShapeDtypeStruct((B, S, D), F32),
        compiler_params=pltpu.CompilerParams(
            dimension_semantics=("arbitrary",), vmem_limit_bytes=VMEM_LIMIT_BYTES),
        name="combine",
    )(*tables, lpos8, wts8, y_rows, x1, norm_final_w)


def _pack_layer(w_in, w_gk_up, w_spatial, b_spatial, gla_norm_w, w_router, b_router):
    ts = SEQ_TILE
    q, k, v, g, gkl, u, vs = jnp.split(w_in, [256, 512, 1024, 1536, 1552, 2064], axis=1)
    gkl = jnp.pad(gkl, ((0, 0), (0, LANES - GLA_GATE_RANK)))
    wall = jnp.concatenate([q, k, v, g, u, vs, gkl], axis=1).astype(BF16)
    wgk = jnp.pad(w_gk_up, ((0, LANES - GLA_GATE_RANK), (0, 0))).astype(BF16)
    reps = ts // SG_CHUNK
    w2 = jnp.tile(w_spatial, (1, reps, reps)).astype(BF16)
    bsb = jnp.broadcast_to(jnp.tile(b_spatial, (1, reps))[:, :, None], (SG_GROUPS, ts, SG_CH))
    glanw = jnp.tile(gla_norm_w, GLA_HEADS)[None, :]
    wr_t = w_router.T
    wr_hi = wr_t.astype(BF16)
    wr_lo = (wr_t - wr_hi.astype(F32)).astype(BF16)
    return wall, wgk, w2, bsb, glanw, wr_hi, wr_lo, b_router[:, None]


def _routing_tables(tile_counts, nb):
    tm = EXPERT_ROWS
    n = tile_counts
    counts = jnp.sum(n, axis=0)
    padded = (counts + (RUN_CHUNK - 1) + tm - 1) // tm * tm
    padded_ends = jnp.cumsum(padded)
    padded_starts = padded_ends - padded
    goff = padded_starts[None, :] + jnp.cumsum(n, axis=0) - n
    nch = (n + RUN_CHUNK - 1) // RUN_CHUNK
    loff = RUN_CHUNK * (jnp.cumsum(nch, axis=1) - nch)
    ntot = jnp.sum(nch, axis=1)
    zero_start = jnp.concatenate([(padded_starts + counts) // RUN_CHUNK * RUN_CHUNK, padded_ends[-1:]])
    zero_end = jnp.concatenate([padded_ends, jnp.full((1,), nb * tm, padded_ends.dtype)])
    zero_chunks = (zero_end - zero_start) // RUN_CHUNK
    block_start = jnp.arange(nb, dtype=jnp.int32) * tm
    block_e = jnp.minimum(jnp.sum((block_start[:, None] >= padded_ends[None, :]).astype(jnp.int32), axis=1),
                          N_EXPERTS - 1)
    n_used = padded_ends[-1:] // tm
    real_end = padded_starts + counts
    is_e = block_e[:, None] == jnp.arange(N_EXPERTS, dtype=jnp.int32)[None, :]
    block_rows = jnp.clip(jnp.sum(jnp.where(is_e, real_end[None, :], 0), axis=1) - block_start, 0, tm)
    i32 = lambda a: a.astype(jnp.int32)
    run_tables = (i32(goff.reshape(-1)), i32(nch.reshape(-1)), i32(loff.reshape(-1)), i32(ntot))
    return run_tables, (i32(zero_start), i32(zero_chunks)), (i32(block_e), i32(n_used), i32(block_rows))


def kernel(x, norm_mix_w, w_in, w_gk_up, b_gk, gla_norm_w, sg_ln_w, sg_ln_b, w_spatial, b_spatial, w_out,
           norm_ffn_w, w_router, b_router, w_gate_up, b_gate_up, w_down, b_down, norm_final_w):
    B, S, D = x.shape
    T = B * S
    assert w_in.shape[0] == 1
    assert D == D_MODEL and S % SEQ_TILE == 0 and B % MIX_BATCH_ROWS == 0 and (T * TOP_K) % EXPERT_ROWS == 0
    tm = EXPERT_ROWS
    spare_blocks = -(-(N_EXPERTS * (RUN_CHUNK - 1)) // tm)
    nb = T * TOP_K // tm + N_EXPERTS + spare_blocks
    l = 0
    wall, wgk, w2, bsb, glanw, wr_hi, wr_lo, br = _pack_layer(
        w_in[l], w_gk_up[l], w_spatial[l], b_spatial[l], gla_norm_w[l], w_router[l], b_router[l])
    x1, h2b, lpos8, wts8, cnt = _token_mixing(
        x, norm_mix_w[l][None, :], wall, wgk, b_gk[l][None, :], glanw, sg_ln_w[l][None, :],
        sg_ln_b[l][None, :], w2, bsb, w_out[l].astype(BF16), norm_ffn_w[l][None, :], wr_hi, wr_lo, br)

    tile_counts = cnt[:, :, :, 0].reshape(T // SEQ_TILE, N_EXPERTS).astype(jnp.int32)
    run_tables, zero_tables, block_tables = _routing_tables(tile_counts, nb)
    x_rows = _dispatch(run_tables + zero_tables, h2b, lpos8, nb * tm)
    y_rows = _experts(block_tables, nb, x_rows, w_gate_up[l], b_gate_up[l][:, None, :],
                      w_down[l], b_down[l][:, None, :])
    return _combine(run_tables, lpos8, wts8, y_rows, x1, norm_final_w[None, :])
```

```python
import jax
import jax.numpy as jnp
from jax import lax
from jax.experimental import pallas as pl
from jax.experimental.pallas import tpu as pltpu

D_MODEL = 1024
GLA_HEADS = 4
GLA_DK = 64
GLA_DV = 128
GLA_QK = GLA_HEADS * GLA_DK
GLA_WIDTH = GLA_HEADS * GLA_DV
GLA_GATE_RANK = 16
GLA_GATE_NORMALIZER = 16.0
GLA_CHUNK = 64
SG_GROUPS = 4
SG_CH = 128
SG_WIDTH = SG_GROUPS * SG_CH
SG_CHUNK = 128
N_EXPERTS = 32
TOP_K = 4
D_EXPERT = D_MODEL
SWIGLU_LIMIT = 7.0
SWIGLU_ALPHA = 1.702
EPS = 1e-6

SUBLANES = 8
LANES = 128
ROW_TILES = D_MODEL // LANES
assert ROW_TILES == SUBLANES

SEQ_TILE = 256
MIX_BATCH_ROWS = 4
EXPERT_ROWS = 512
RUN_CHUNK = 1
ZERO_CHUNKS = 64
LOCAL_ROWS = SEQ_TILE * TOP_K
assert SEQ_TILE * TOP_K + N_EXPERTS * (RUN_CHUNK - 1) <= LOCAL_ROWS
VMEM_LIMIT_BYTES = 56 * 1024 * 1024

_Q0, _K0, _V0, _G0, _U0, _VS0, _GK0 = 0, 256, 512, 1024, 1536, 2048, 2560
IN_COLS_PACKED = 2688

F32 = jnp.float32
BF16 = jnp.bfloat16
_NT = (((1,), (1,)), ((), ()))
_TN = (((0,), (0,)), ((), ()))


def _dot(a, b):
    return jnp.dot(a, b, preferred_element_type=F32)


def _split_bf16(a):
    hi = a.astype(BF16)
    lo = (a - hi.astype(F32)).astype(BF16)
    return hi, lo


def _gelu_tanh(a):
    return 0.5 * a * (1.0 + jnp.tanh(0.7978845608028654 * (a + 0.044715 * (a * a * a))))


def _rows_from_tiles(ref, lead, n_rows, base=0):
    parts = [ref[(*lead, pl.ds(base + j, n_rows, stride=ROW_TILES), slice(None))] for j in range(ROW_TILES)]
    return jnp.concatenate(parts, axis=1)


def _rows_to_tiles(ref, lead, val, n_rows, base=0):
    for j in range(ROW_TILES):
        ref[(*lead, pl.ds(base + j, n_rows, stride=ROW_TILES), slice(None))] = val[:, j * LANES:(j + 1) * LANES]


_DONE = object()


def _mix_kernel(*refs):
    state_ref = refs[-1]

    @pl.when(pl.program_id(1) == 0)
    def _():
        state_ref[...] = jnp.zeros_like(state_ref)

    tiles = [_mix_tile(ti, *refs) for ti in range(MIX_BATCH_ROWS)]
    while tiles:
        tiles = [t for t in tiles if next(t, _DONE) is not _DONE]


def _mix_tile(ti, x_ref, nmw_ref, wall_ref, wgk_ref, bgk_ref, glanw_ref, lnw_ref, lnb_ref, w2_ref, bsb_ref,
              wout_ref, nfw_ref, wrh_ref, wrl_ref, br_ref,
              x1_ref, h2_ref, lpos_ref, wts_ref, cnt_ref,
              state_ref):
    ts = SEQ_TILE
    n_chunks = ts // GLA_CHUNK

    x = x_ref[ti]
    ms = jnp.mean(x * x, axis=-1, keepdims=True)
    hb = (x * lax.rsqrt(ms + EPS) * nmw_ref[...]).astype(BF16)
    yield

    def proj(lo, hi):
        return _dot(hb, wall_ref[:, lo:hi])

    gkl = proj(_GK0, IN_COLS_PACKED)
    z = _dot(gkl.astype(BF16), wgk_ref[...]) + bgk_ref[...]
    log_a = (jnp.minimum(z, 0.0) - jnp.log(1.0 + jnp.exp(-jnp.abs(z)))) * (1.0 / GLA_GATE_NORMALIZER)
    la_hi, la_lo = _split_bf16(log_a)
    yield

    row = lax.broadcasted_iota(jnp.int32, (ts, ts), 0)
    col = lax.broadcasted_iota(jnp.int32, (ts, ts), 1)
    row_base = (row // GLA_CHUNK) * GLA_CHUNK
    in_chunk_le = jnp.where(col <= row, jnp.where(col >= row_base, 1.0, 0.0), 0.0)
    causal = in_chunk_le > 0.5
    lower = in_chunk_le.astype(BF16)
    b_cum = _dot(lower, la_hi) + _dot(lower, la_lo)
    b_last = jnp.concatenate(
        [jnp.broadcast_to(b_cum[(c + 1) * GLA_CHUNK - 1:(c + 1) * GLA_CHUNK, :], (GLA_CHUNK, GLA_QK))
         for c in range(n_chunks)], axis=0)
    b_rest = b_last - b_cum
    yield
    assert 2 * n_chunks == SUBLANES
    r8 = lax.broadcasted_iota(jnp.int32, (SUBLANES, GLA_QK), 0)
    ends = jnp.zeros((SUBLANES, GLA_QK), F32)
    for c in range(n_chunks):
        end_row = b_cum[(c + 1) * GLA_CHUNK - 1:(c + 1) * GLA_CHUNK, :]
        end_hi = end_row.astype(BF16).astype(F32)
        ends = jnp.where(r8 == c, end_hi, jnp.where(r8 == n_chunks + c, end_row - end_hi, ends))
    chunk_sel = jnp.where(
        lax.broadcasted_iota(jnp.int32, (SUBLANES, n_chunks * LANES), 0) % n_chunks
        == lax.broadcasted_iota(jnp.int32, (SUBLANES, n_chunks * LANES), 1) // LANES, 1.0, 0.0).astype(BF16)
    b_last_t = lax.dot_general(ends.astype(BF16), chunk_sel, _TN, preferred_element_type=F32)
    decay_t = jnp.exp(b_last_t)
    yield

    qk = proj(_Q0, _V0)
    q = qk[:, :GLA_QK]
    k = qk[:, GLA_QK:]
    q_dec = q * (GLA_DK ** -0.5) * jnp.exp(b_cum)
    k_inv = (k * jnp.exp(-b_cum)).astype(BF16)
    k_end_t = (k * jnp.exp(b_rest)).T
    yield
    v = proj(_V0, _G0)
    vb = v.astype(BF16)
    g = proj(_G0, _U0)
    yield

    head_of_lane = lax.broadcasted_iota(jnp.int32, (ts, GLA_QK), 1) // GLA_DK
    o_heads = []
    for h in range(GLA_HEADS):
        q_h = jnp.where(head_of_lane == h, q_dec, 0.0).astype(BF16)
        sc = lax.dot_general(q_h, k_inv, _NT, preferred_element_type=F32)
        sc = jnp.where(causal, sc, 0.0).astype(BF16)
        o_heads.append(_dot(sc, vb[:, h * GLA_DV:(h + 1) * GLA_DV]))
        yield

    same_chunk = (lax.broadcasted_iota(jnp.int32, (ts, n_chunks * GLA_DV), 0) // GLA_CHUNK
                  == lax.broadcasted_iota(jnp.int32, (ts, n_chunks * GLA_DV), 1) // GLA_DV)
    upd_heads = []
    for h in range(GLA_HEADS):
        v_h = v[:, h * GLA_DV:(h + 1) * GLA_DV]
        v_by_chunk = jnp.where(same_chunk, jnp.concatenate([v_h] * n_chunks, axis=1), 0.0).astype(BF16)
        upd_heads.append(_dot(k_end_t[h * GLA_DK:(h + 1) * GLA_DK, :].astype(BF16), v_by_chunk))
        yield

    state = state_ref[ti]
    zero_blk = jnp.zeros((GLA_DK, GLA_DV), BF16)
    o_inter = []
    for c in range(n_chunks):
        rows = slice(c * GLA_CHUNK, (c + 1) * GLA_CHUNK)
        state_b = state.astype(BF16)
        state_bd = jnp.concatenate(
            [jnp.concatenate([state_b[h * GLA_DK:(h + 1) * GLA_DK] if h2 == h else zero_blk
                              for h2 in range(GLA_HEADS)], axis=1) for h in range(GLA_HEADS)], axis=0)
        o_inter.append(_dot(q_dec[rows].astype(BF16), state_bd))
        upd = jnp.concatenate([upd_heads[h][:, c * GLA_DV:(c + 1) * GLA_DV] for h in range(GLA_HEADS)], axis=0)
        state = decay_t[:, c * LANES:(c + 1) * LANES] * state + upd
        yield
    state_ref[ti] = state
    inter = jnp.concatenate(o_inter, axis=0)

    gla_parts = []
    for h in range(GLA_HEADS):
        o_h = o_heads[h] + inter[:, h * GLA_DV:(h + 1) * GLA_DV]
        ms_h = jnp.mean(o_h * o_h, axis=-1, keepdims=True)
        g_h = g[:, h * GLA_DV:(h + 1) * GLA_DV]
        gla_parts.append(o_h * lax.rsqrt(ms_h + EPS) * glanw_ref[:, h * GLA_DV:(h + 1) * GLA_DV]
                         * (g_h * jax.nn.sigmoid(g_h)))
        yield

    u = _gelu_tanh(proj(_U0, _VS0))
    vf = _gelu_tanh(proj(_VS0, _GK0))
    mu = jnp.mean(vf, axis=-1, keepdims=True)
    dv = vf - mu
    var = jnp.mean(dv * dv, axis=-1, keepdims=True)
    vn = (dv * lax.rsqrt(var + EPS) * lnw_ref[...] + lnb_ref[...]).astype(BF16)
    yield
    sg_base = (row // SG_CHUNK) * SG_CHUNK
    sg_mask = jnp.where(col <= row, jnp.where(col >= sg_base, 1.0, 0.0), 0.0) > 0.5
    sg_parts = []
    for gi in range(SG_GROUPS):
        w_c = jnp.where(sg_mask, w2_ref[gi], 0.0).astype(BF16)
        mixed_g = _dot(w_c, vn[:, gi * SG_CH:(gi + 1) * SG_CH]) + bsb_ref[gi]
        sg_parts.append(u[:, gi * SG_CH:(gi + 1) * SG_CH] * mixed_g)
        yield

    mixed = jnp.concatenate(gla_parts + sg_parts, axis=1).astype(BF16)
    x1 = x + _dot(mixed, wout_ref[...])
    x1_ref[ti] = x1
    yield

    ms2 = jnp.mean(x1 * x1, axis=-1, keepdims=True)
    h2 = x1 * lax.rsqrt(ms2 + EPS) * nfw_ref[...]
    h2_hi, h2_lo = _split_bf16(h2)
    h2_ref[ti] = h2_hi
    yield
    wr_hi = wrh_ref[...]
    logits = (lax.dot_general(wr_hi, h2_hi, _NT, preferred_element_type=F32)
              + lax.dot_general(wr_hi, h2_lo, _NT, preferred_element_type=F32)
              + lax.dot_general(wrl_ref[...], h2_hi, _NT, preferred_element_type=F32)
              + br_ref[...])
    yield
    e_iota = lax.broadcasted_iota(jnp.int32, (N_EXPERTS, ts), 0)
    work = logits
    top_val, top_idx = [], []
    for _ in range(TOP_K):
        m = jnp.max(work, axis=0, keepdims=True)
        i = jnp.min(jnp.where(work == m, e_iota, N_EXPERTS), axis=0, keepdims=True)
        top_val.append(m)
        top_idx.append(i)
        work = jnp.where(e_iota == i, -jnp.inf, work)
        yield
    ex = [jnp.exp(v - top_val[0]) for v in top_val]
    den = ex[0] + ex[1] + ex[2] + ex[3]
    top_w = [e / den for e in ex]
    yield

    onehot = jnp.zeros((N_EXPERTS, ts), F32)
    for i in top_idx:
        onehot = onehot + jnp.where(e_iota == i, 1.0, 0.0)
    onehot_b = onehot.astype(BF16)
    incl = jnp.where(row <= col, 1.0, 0.0).astype(BF16)
    csum = _dot(onehot_b, incl)
    total = _dot(onehot_b, jnp.ones((ts, ts), BF16))
    yield
    n_chunks_e = jnp.floor((total + (RUN_CHUNK - 1)) * (1.0 / RUN_CHUNK))
    e_lower = jnp.where(lax.broadcasted_iota(jnp.int32, (N_EXPERTS, N_EXPERTS), 1)
                        < lax.broadcasted_iota(jnp.int32, (N_EXPERTS, N_EXPERTS), 0), 1.0, 0.0).astype(BF16)
    local_start = RUN_CHUNK * _dot(e_lower, n_chunks_e.astype(BF16))
    local_pos = local_start + csum - onehot
    yield
    lpos = [jnp.sum(jnp.where(e_iota == i, local_pos, 0.0), axis=0, keepdims=True).astype(jnp.int32)
            for i in top_idx]
    onehot_lanes = jnp.concatenate([onehot_b, jnp.zeros((LANES - N_EXPERTS, ts), BF16)], axis=0)
    cnt_ref[ti, 0] = lax.dot_general(jnp.ones((SUBLANES, ts), BF16), onehot_lanes, _NT, preferred_element_type=F32)

    r8 = lax.broadcasted_iota(jnp.int32, (SUBLANES, ts), 0)

    def rows8(vals, fill):
        out = jnp.full((SUBLANES, ts), fill, vals[0].dtype)
        for kk, v in enumerate(vals):
            out = jnp.where(r8 == kk, v, out)
        return out

    lpos_ref[ti] = rows8(lpos, -1)
    wts_ref[ti] = rows8(top_w, 0.0)


def _token_mixing(x, nmw, wall, wgk, bgk, glanw, lnw, lnb, w2, bsb, wout, nfw, wrh, wrl, br):
    B, S, D = x.shape
    ts = SEQ_TILE
    ns = S // ts
    T = B * S

    def const(shape):
        return pl.BlockSpec(shape, lambda b, s: (0,) * len(shape))

    nbr = MIX_BATCH_ROWS
    return pl.pallas_call(
        _mix_kernel,
        grid=(B // nbr, ns),
        in_specs=[
            pl.BlockSpec((nbr, ts, D), lambda b, s: (b, s, 0)),
            const((1, D)), const((D, IN_COLS_PACKED)), const((LANES, GLA_QK)), const((1, GLA_QK)),
            const((1, GLA_WIDTH)), const((1, SG_WIDTH)), const((1, SG_WIDTH)),
            const((SG_GROUPS, ts, ts)), const((SG_GROUPS, ts, SG_CH)),
            const((D, D)), const((1, D)), const((N_EXPERTS, D)), const((N_EXPERTS, D)), const((N_EXPERTS, 1)),
        ],
        out_specs=[
            pl.BlockSpec((nbr, ts, D), lambda b, s: (b, s, 0)),
            pl.BlockSpec((nbr, ts, D), lambda b, s: (b, s, 0)),
            pl.BlockSpec((nbr, SUBLANES, ts), lambda b, s: (b, 0, s)),
            pl.BlockSpec((nbr, SUBLANES, ts), lambda b, s: (b, 0, s)),
            pl.BlockSpec((nbr, 1, SUBLANES, LANES), lambda b, s: (b, s, 0, 0)),
        ],
        out_shape=[
            jax.ShapeDtypeStruct((B, S, D), F32),
            jax.ShapeDtypeStruct((B, S, D), BF16),
            jax.ShapeDtypeStruct((B, SUBLANES, S), jnp.int32),
            jax.ShapeDtypeStruct((B, SUBLANES, S), F32),
            jax.ShapeDtypeStruct((B, ns, SUBLANES, LANES), F32),
        ],
        scratch_shapes=[pltpu.VMEM((nbr, GLA_QK, GLA_DV), F32)],
        compiler_params=pltpu.CompilerParams(
            dimension_semantics=("arbitrary", "arbitrary"), vmem_limit_bytes=VMEM_LIMIT_BYTES),
        name="token_mixing",
    )(x, nmw, wall, wgk, bgk, glanw, lnw, lnb, w2, bsb, wout, nfw, wrh, wrl, br)


def _chunk_rows(start_row, n_chunks=1):
    return pl.ds(pl.multiple_of(start_row * ROW_TILES, ROW_TILES), n_chunks * (RUN_CHUNK * ROW_TILES))


def _wait_chunks(hbm_ref, sem, n_chunks):
    @pl.when(n_chunks > 0)
    def _():
        rows = pl.ds(0, n_chunks * (RUN_CHUNK * ROW_TILES))
        pltpu.make_async_copy(hbm_ref.at[rows, :], hbm_ref.at[rows, :], sem).wait()


def _one_hot_positions(lpos_ref, values):
    ts = SEQ_TILE
    p_iota = lax.broadcasted_iota(jnp.int32, (LOCAL_ROWS, ts), 0)
    out = jnp.zeros((LOCAL_ROWS, ts), F32)
    for kk in range(TOP_K):
        out = jnp.where(p_iota == lpos_ref[kk:kk + 1, :], values[kk], out)
    return out.astype(BF16)


def _dispatch_kernel(goff_ref, nch_ref, loff_ref, ntot_ref, zst_ref, znc_ref,
                     h2_ref, lpos_ref, xbuf_hbm, srt_ref, zero_ref, sem, zsem):
    s = pl.program_id(0)
    n_steps = pl.num_programs(0)
    slot = s % 2

    @pl.when(s == 0)
    def _():
        zero_ref[...] = jnp.zeros_like(zero_ref)

        def per_expert(e, total):
            n_full = znc_ref[e] // ZERO_CHUNKS
            n_rest = znc_ref[e] % ZERO_CHUNKS

            def per_copy(c, carry):
                first = zst_ref[e] + c * (ZERO_CHUNKS * RUN_CHUNK)
                pltpu.make_async_copy(zero_ref, xbuf_hbm.at[_chunk_rows(first, ZERO_CHUNKS), :], zsem).start()
                return carry
            lax.fori_loop(0, n_full, per_copy, 0)

            @pl.when(n_rest > 0)
            def _():
                first = zst_ref[e] + n_full * (ZERO_CHUNKS * RUN_CHUNK)
                pltpu.make_async_copy(zero_ref.at[_chunk_rows(0, n_rest), :],
                                      xbuf_hbm.at[_chunk_rows(first, n_rest), :], zsem).start()
            return total + znc_ref[e]
        n_zero = lax.fori_loop(0, N_EXPERTS + 1, per_expert, 0)
        _wait_chunks(xbuf_hbm, zsem, n_zero)

    ones = [1.0] * TOP_K
    perm = _one_hot_positions(lpos_ref, ones)
    local_sorted = _dot(perm, h2_ref[...])

    _rows_to_tiles(srt_ref, (slot,), local_sorted, LOCAL_ROWS)

    @pl.when(s > 0)
    def _():
        _wait_chunks(xbuf_hbm, sem.at[1 - slot], ntot_ref[s - 1])

    def per_expert(e, carry):
        g0 = goff_ref[s * N_EXPERTS + e]
        l0 = loff_ref[s * N_EXPERTS + e]

        n = nch_ref[s * N_EXPERTS + e]

        @pl.when(n > 0)
        def _():
            pltpu.make_async_copy(srt_ref.at[slot, _chunk_rows(l0, n), :],
                                  xbuf_hbm.at[_chunk_rows(g0, n), :], sem.at[slot]).start()
        return carry
    lax.fori_loop(0, N_EXPERTS, per_expert, 0)

    @pl.when(s == n_steps - 1)
    def _():
        _wait_chunks(xbuf_hbm, sem.at[slot], ntot_ref[s])


def _dispatch(tables, h2b, lpos8, n_rows_buf):
    B, S, D = h2b.shape
    ts = SEQ_TILE
    ns = S // ts
    n_tiles = B * ns
    grid_spec = pltpu.PrefetchScalarGridSpec(
        num_scalar_prefetch=len(tables),
        grid=(n_tiles,),
        in_specs=[
            pl.BlockSpec((None, ts, D), lambda t, *_: (t // ns, t % ns, 0)),
            pl.BlockSpec((None, SUBLANES, ts), lambda t, *_: (t // ns, 0, t % ns)),
        ],
        out_specs=pl.BlockSpec(memory_space=pl.ANY),
        scratch_shapes=[
            pltpu.VMEM((2, LOCAL_ROWS * ROW_TILES, LANES), F32),
            pltpu.VMEM((ZERO_CHUNKS * RUN_CHUNK * ROW_TILES, LANES), F32),
            pltpu.SemaphoreType.DMA((2,)),
            pltpu.SemaphoreType.DMA(()),
        ],
    )
    return pl.pallas_call(
        _dispatch_kernel,
        grid_spec=grid_spec,
        out_shape=jax.ShapeDtypeStruct((n_rows_buf * ROW_TILES, LANES), F32),
        compiler_params=pltpu.CompilerParams(
            dimension_semantics=("arbitrary",), vmem_limit_bytes=VMEM_LIMIT_BYTES),
        name="dispatch",
    )(*tables, h2b, lpos8)


def _expert_kernel(be_ref, nu_ref, nv_ref, x_ref, wgu_hbm, bgu_ref, wd_hbm, bd_ref, y_ref,
                   wgu_f_ref, wd_f_ref, wgu_b_ref, wd_b_ref, wsem):
    tm = EXPERT_ROWS
    i = pl.program_id(0)
    n_used = nu_ref[0]
    e = be_ref[i]

    def weight_copies(expert, slot_):
        return (pltpu.make_async_copy(wgu_hbm.at[expert], wgu_f_ref.at[slot_], wsem.at[0, slot_]),
                pltpu.make_async_copy(wd_hbm.at[expert], wd_f_ref.at[slot_], wsem.at[1, slot_]))

    @pl.when(i == 0)
    def _():
        for cp in weight_copies(e, e % 2):
            cp.start()

    @pl.when(i < n_used)
    def _():
        first_of_expert = jnp.logical_or(i == 0, e != be_ref[jnp.maximum(i - 1, 0)])

        @pl.when(first_of_expert)
        def _():
            slot = e % 2
            for cp in weight_copies(e, slot):
                cp.wait()

            @pl.when(e + 1 < N_EXPERTS)
            def _():
                for cp in weight_copies(e + 1, 1 - slot):
                    cp.start()

            step = 128
            for r0 in range(0, D_MODEL, step):
                wgu_b_ref[r0:r0 + step, :] = wgu_f_ref[slot, r0:r0 + step, :].astype(BF16)
                wd_b_ref[r0:r0 + step, :] = wd_f_ref[slot, r0:r0 + step, :].astype(BF16)

        def mlp(n_rows):
            xb = _rows_from_tiles(x_ref, (), n_rows).astype(BF16)
            gu = _dot(xb, wgu_b_ref[...]) + bgu_ref[...]
            gate = jnp.minimum(gu[:, :D_EXPERT], SWIGLU_LIMIT)
            up = jnp.clip(gu[:, D_EXPERT:], -SWIGLU_LIMIT, SWIGLU_LIMIT)
            act = ((up + 1.0) * (gate * jax.nn.sigmoid(SWIGLU_ALPHA * gate))).astype(BF16)
            y = _dot(act, wd_b_ref[...]) + bd_ref[...]
            _rows_to_tiles(y_ref, (), y, n_rows)

        half = tm // 2

        @pl.when(nv_ref[i] > half)
        def _():
            mlp(tm)

        @pl.when(nv_ref[i] <= half)
        def _():
            mlp(half)
            y_ref[half * ROW_TILES:, :] = jnp.zeros((half * ROW_TILES, LANES), F32)

    @pl.when(i >= n_used)
    def _():
        y_ref[...] = jnp.zeros_like(y_ref)


def _experts(block_tables, nb, x_rows, w_gate_up, b_gate_up, w_down, b_down):
    tm = EXPERT_ROWS
    D = D_MODEL
    grid_spec = pltpu.PrefetchScalarGridSpec(
        num_scalar_prefetch=len(block_tables),
        grid=(nb,),
        in_specs=[
            pl.BlockSpec((tm * ROW_TILES, LANES), lambda i, be, nu, nv:(jnp.minimum(i, nu[0] - 1), 0)),
            pl.BlockSpec(memory_space=pl.ANY),
            pl.BlockSpec((None, 1, 2 * D_EXPERT), lambda i, be, nu, nv:(be[i], 0, 0)),
            pl.BlockSpec(memory_space=pl.ANY),
            pl.BlockSpec((None, 1, D), lambda i, be, nu, nv:(be[i], 0, 0)),
        ],
        out_specs=pl.BlockSpec((tm * ROW_TILES, LANES), lambda i, be, nu, nv:(i, 0)),
        scratch_shapes=[
            pltpu.VMEM((2, D, 2 * D_EXPERT), F32),
            pltpu.VMEM((2, D_EXPERT, D), F32),
            pltpu.VMEM((D, 2 * D_EXPERT), BF16),
            pltpu.VMEM((D_EXPERT, D), BF16),
            pltpu.SemaphoreType.DMA((2, 2)),
        ],
    )
    return pl.pallas_call(
        _expert_kernel,
        grid_spec=grid_spec,
        out_shape=jax.ShapeDtypeStruct((nb * tm * ROW_TILES, LANES), F32),
        compiler_params=pltpu.CompilerParams(
            dimension_semantics=("arbitrary",), vmem_limit_bytes=VMEM_LIMIT_BYTES),
        name="routed_experts",
    )(*block_tables, x_rows, w_gate_up, b_gate_up, w_down, b_down)


def _combine_kernel(goff_ref, nch_ref, loff_ref, ntot_ref,
                    lpos_ref, wts_ref, y_hbm, x1_ref, nw_ref, out_ref, ysrt_ref, sem):
    s = pl.program_id(0)
    n_steps = pl.num_programs(0)
    slot = s % 2

    def issue(tile, slot_):
        def per_expert(e, carry):
            g0 = goff_ref[tile * N_EXPERTS + e]
            l0 = loff_ref[tile * N_EXPERTS + e]

            n = nch_ref[tile * N_EXPERTS + e]

            @pl.when(n > 0)
            def _():
                pltpu.make_async_copy(y_hbm.at[_chunk_rows(g0, n), :],
                                      ysrt_ref.at[slot_, _chunk_rows(l0, n), :], sem.at[slot_]).start()
            return carry
        lax.fori_loop(0, N_EXPERTS, per_expert, 0)

    @pl.when(s == 0)
    def _():
        issue(0, 0)

    @pl.when(s + 1 < n_steps)
    def _():
        issue(s + 1, 1 - slot)

    _wait_chunks(y_hbm, sem.at[slot], ntot_ref[s])

    weights = [wts_ref[kk:kk + 1, :] for kk in range(TOP_K)]
    perm_w = _one_hot_positions(lpos_ref, weights)
    y_local = _rows_from_tiles(ysrt_ref, (slot,), LOCAL_ROWS).astype(BF16)
    acc = x1_ref[...] + lax.dot_general(perm_w, y_local, _TN, preferred_element_type=F32)
    ms = jnp.mean(acc * acc, axis=-1, keepdims=True)
    out_ref[...] = acc * lax.rsqrt(ms + EPS) * nw_ref[...]


def _combine(tables, lpos8, wts8, y_rows, x1, norm_final_w):
    B, S, D = x1.shape
    ts = SEQ_TILE
    ns = S // ts
    n_tiles = B * ns
    meta = pl.BlockSpec((None, SUBLANES, ts), lambda t, *_: (t // ns, 0, t % ns))
    rows = pl.BlockSpec((None, ts, D), lambda t, *_: (t // ns, t % ns, 0))
    grid_spec = pltpu.PrefetchScalarGridSpec(
        num_scalar_prefetch=len(tables),
        grid=(n_tiles,),
        in_specs=[
            meta,
            meta,
            pl.BlockSpec(memory_space=pl.ANY),
            rows,
            pl.BlockSpec((1, D), lambda t, *_: (0, 0)),
        ],
        out_specs=rows,
        scratch_shapes=[
            pltpu.VMEM((2, LOCAL_ROWS * ROW_TILES, LANES), F32),
            pltpu.SemaphoreType.DMA((2,)),
        ],
    )
    return pl.pallas_call(
        _combine_kernel,
        grid_spec=grid_spec,
        out_shape=jax.ShapeDtypeStruct((B, S, D), F32),
        compiler_params=pltpu.CompilerParams(
            dimension_semantics=("arbitrary",), vmem_limit_bytes=VMEM_LIMIT_BYTES),
        name="combine",
    )(*tables, lpos8, wts8, y_rows, x1, norm_final_w)


def _pack_layer(w_in, w_gk_up, w_spatial, b_spatial, gla_norm_w, w_router, b_router):
    ts = SEQ_TILE
    q, k, v, g, gkl, u, vs = jnp.split(w_in, [256, 512, 1024, 1536, 1552, 2064], axis=1)
    gkl = jnp.pad(gkl, ((0, 0), (0, LANES - GLA_GATE_RANK)))
    wall = jnp.concatenate([q, k, v, g, u, vs, gkl], axis=1).astype(BF16)
    wgk = jnp.pad(w_gk_up, ((0, LANES - GLA_GATE_RANK), (0, 0))).astype(BF16)
    reps = ts // SG_CHUNK
    w2 = jnp.tile(w_spatial, (1, reps, reps)).astype(BF16)
    bsb = jnp.broadcast_to(jnp.tile(b_spatial, (1, reps))[:, :, None], (SG_GROUPS, ts, SG_CH))
    glanw = jnp.tile(gla_norm_w, GLA_HEADS)[None, :]
    wr_t = w_router.T
    wr_hi = wr_t.astype(BF16)
    wr_lo = (wr_t - wr_hi.astype(F32)).astype(BF16)
    return wall, wgk, w2, bsb, glanw, wr_hi, wr_lo, b_router[:, None]


def _routing_tables(tile_counts, nb):
    tm = EXPERT_ROWS
    n = tile_counts
    counts = jnp.sum(n, axis=0)
    padded = (counts + (RUN_CHUNK - 1) + tm - 1) // tm * tm
    padded_ends = jnp.cumsum(padded)
    padded_starts = padded_ends - padded
    goff = padded_starts[None, :] + jnp.cumsum(n, axis=0) - n
    nch = (n + RUN_CHUNK - 1) // RUN_CHUNK
    loff = RUN_CHUNK * (jnp.cumsum(nch, axis=1) - nch)
    ntot = jnp.sum(nch, axis=1)
    zero_start = jnp.concatenate([(padded_starts + counts) // RUN_CHUNK * RUN_CHUNK, padded_ends[-1:]])
    zero_end = jnp.concatenate([padded_ends, jnp.full((1,), nb * tm, padded_ends.dtype)])
    zero_chunks = (zero_end - zero_start) // RUN_CHUNK
    block_start = jnp.arange(nb, dtype=jnp.int32) * tm
    block_e = jnp.minimum(jnp.sum((block_start[:, None] >= padded_ends[None, :]).astype(jnp.int32), axis=1),
                          N_EXPERTS - 1)
    n_used = padded_ends[-1:] // tm
    real_end = padded_starts + counts
    is_e = block_e[:, None] == jnp.arange(N_EXPERTS, dtype=jnp.int32)[None, :]
    block_rows = jnp.clip(jnp.sum(jnp.where(is_e, real_end[None, :], 0), axis=1) - block_start, 0, tm)
    i32 = lambda a: a.astype(jnp.int32)
    run_tables = (i32(goff.reshape(-1)), i32(nch.reshape(-1)), i32(loff.reshape(-1)), i32(ntot))
    return run_tables, (i32(zero_start), i32(zero_chunks)), (i32(block_e), i32(n_used), i32(block_rows))


def kernel(x, norm_mix_w, w_in, w_gk_up, b_gk, gla_norm_w, sg_ln_w, sg_ln_b, w_spatial, b_spatial, w_out,
           norm_ffn_w, w_router, b_router, w_gate_up, b_gate_up, w_down, b_down, norm_final_w):
    B, S, D = x.shape
    T = B * S
    assert w_in.shape[0] == 1
    assert D == D_MODEL and S % SEQ_TILE == 0 and B % MIX_BATCH_ROWS == 0 and (T * TOP_K) % EXPERT_ROWS == 0
    tm = EXPERT_ROWS
    spare_blocks = -(-(N_EXPERTS * (RUN_CHUNK - 1)) // tm)
    nb = T * TOP_K // tm + N_EXPERTS + spare_blocks
    l = 0
    wall, wgk, w2, bsb, glanw, wr_hi, wr_lo, br = _pack_layer(
        w_in[l], w_gk_up[l], w_spatial[l], b_spatial[l], gla_norm_w[l], w_router[l], b_router[l])
    x1, h2b, lpos8, wts8, cnt = _token_mixing(
        x, norm_mix_w[l][None, :], wall, wgk, b_gk[l][None, :], glanw, sg_ln_w[l][None, :],
        sg_ln_b[l][None, :], w2, bsb, w_out[l].astype(BF16), norm_ffn_w[l][None, :], wr_hi, wr_lo, br)

    tile_counts = cnt[:, :, 0, :N_EXPERTS].reshape(T // SEQ_TILE, N_EXPERTS).astype(jnp.int32)
    run_tables, zero_tables, block_tables = _routing_tables(tile_counts, nb)
    x_rows = _dispatch(run_tables + zero_tables, h2b, lpos8, nb * tm)
    y_rows = _experts(block_tables, nb, x_rows, w_gate_up[l], b_gate_up[l][:, None, :],
                      w_down[l], b_down[l][:, None, :])
    return _combine(run_tables, lpos8, wts8, y_rows, x1, norm_final_w[None, :])
```

```python
import jax
import jax.numpy as jnp
from jax import lax
from jax.experimental import pallas as pl
from jax.experimental.pallas import tpu as pltpu

D_MODEL = 1024
GLA_HEADS = 4
GLA_DK = 64
GLA_DV = 128
GLA_QK = GLA_HEADS * GLA_DK
GLA_WIDTH = GLA_HEADS * GLA_DV
GLA_GATE_RANK = 16
GLA_GATE_NORMALIZER = 16.0
GLA_CHUNK = 64
SG_GROUPS = 4
SG_CH = 128
SG_WIDTH = SG_GROUPS * SG_CH
SG_CHUNK = 128
N_EXPERTS = 32
TOP_K = 4
D_EXPERT = D_MODEL
SWIGLU_LIMIT = 7.0
SWIGLU_ALPHA = 1.702
EPS = 1e-6

SUBLANES = 8
LANES = 128
ROW_TILES = D_MODEL // LANES
assert ROW_TILES == SUBLANES

SEQ_TILE = 256
MIX_BATCH_ROWS = 4
EXPERT_ROWS = 512
RUN_CHUNK = 1
DMA_PRIORITIES = 2
ZERO_CHUNKS = 64
LOCAL_ROWS = SEQ_TILE * TOP_K
assert SEQ_TILE * TOP_K + N_EXPERTS * (RUN_CHUNK - 1) <= LOCAL_ROWS
VMEM_LIMIT_BYTES = 56 * 1024 * 1024

_Q0, _K0, _V0, _G0, _U0, _VS0, _GK0 = 0, 256, 512, 1024, 1536, 2048, 2560
IN_COLS_PACKED = 2688

F32 = jnp.float32
BF16 = jnp.bfloat16
_NT = (((1,), (1,)), ((), ()))
_TN = (((0,), (0,)), ((), ()))


def _dot(a, b):
    return jnp.dot(a, b, preferred_element_type=F32)


def _split_bf16(a):
    hi = a.astype(BF16)
    lo = (a - hi.astype(F32)).astype(BF16)
    return hi, lo


def _gelu_tanh(a):
    return 0.5 * a * (1.0 + jnp.tanh(0.7978845608028654 * (a + 0.044715 * (a * a * a))))


def _rows_from_tiles(ref, lead, n_rows, base=0):
    parts = [ref[(*lead, pl.ds(base + j, n_rows, stride=ROW_TILES), slice(None))] for j in range(ROW_TILES)]
    return jnp.concatenate(parts, axis=1)


def _rows_to_tiles(ref, lead, val, n_rows, base=0):
    for j in range(ROW_TILES):
        ref[(*lead, pl.ds(base + j, n_rows, stride=ROW_TILES), slice(None))] = val[:, j * LANES:(j + 1) * LANES]


_DONE = object()


def _mix_kernel(*refs):
    state_ref = refs[-1]

    @pl.when(pl.program_id(1) == 0)
    def _():
        state_ref[...] = jnp.zeros_like(state_ref)

    tiles = [_mix_tile(ti, *refs) for ti in range(MIX_BATCH_ROWS)]
    while tiles:
        tiles = [t for t in tiles if next(t, _DONE) is not _DONE]


def _mix_tile(ti, x_ref, nmw_ref, wall_ref, wgk_ref, bgk_ref, glanw_ref, lnw_ref, lnb_ref, w2_ref, bsb_ref,
              wout_ref, nfw_ref, wrh_ref, wrl_ref, br_ref,
              x1_ref, h2_ref, lpos_ref, wts_ref, cnt_ref,
              state_ref):
    ts = SEQ_TILE
    n_chunks = ts // GLA_CHUNK

    x = x_ref[ti]
    ms = jnp.mean(x * x, axis=-1, keepdims=True)
    hb = (x * lax.rsqrt(ms + EPS) * nmw_ref[...]).astype(BF16)
    yield

    def proj(lo, hi):
        return _dot(hb, wall_ref[:, lo:hi])

    gkl = proj(_GK0, IN_COLS_PACKED)
    z = _dot(gkl.astype(BF16), wgk_ref[...]) + bgk_ref[...]
    log_a = (jnp.minimum(z, 0.0) - jnp.log(1.0 + jnp.exp(-jnp.abs(z)))) * (1.0 / GLA_GATE_NORMALIZER)
    la_hi, la_lo = _split_bf16(log_a)
    yield

    row = lax.broadcasted_iota(jnp.int32, (ts, ts), 0)
    col = lax.broadcasted_iota(jnp.int32, (ts, ts), 1)
    row_base = (row // GLA_CHUNK) * GLA_CHUNK
    in_chunk_le = jnp.where(col <= row, jnp.where(col >= row_base, 1.0, 0.0), 0.0)
    causal = in_chunk_le > 0.5
    lower = in_chunk_le.astype(BF16)
    b_cum = _dot(lower, la_hi) + _dot(lower, la_lo)
    b_last = jnp.concatenate(
        [jnp.broadcast_to(b_cum[(c + 1) * GLA_CHUNK - 1:(c + 1) * GLA_CHUNK, :], (GLA_CHUNK, GLA_QK))
         for c in range(n_chunks)], axis=0)
    b_rest = b_last - b_cum
    yield
    assert 2 * n_chunks == SUBLANES
    r8 = lax.broadcasted_iota(jnp.int32, (SUBLANES, GLA_QK), 0)
    ends = jnp.zeros((SUBLANES, GLA_QK), F32)
    for c in range(n_chunks):
        end_row = b_cum[(c + 1) * GLA_CHUNK - 1:(c + 1) * GLA_CHUNK, :]
        end_hi = end_row.astype(BF16).astype(F32)
        ends = jnp.where(r8 == c, end_hi, jnp.where(r8 == n_chunks + c, end_row - end_hi, ends))
    chunk_sel = jnp.where(
        lax.broadcasted_iota(jnp.int32, (SUBLANES, n_chunks * LANES), 0) % n_chunks
        == lax.broadcasted_iota(jnp.int32, (SUBLANES, n_chunks * LANES), 1) // LANES, 1.0, 0.0).astype(BF16)
    b_last_t = lax.dot_general(ends.astype(BF16), chunk_sel, _TN, preferred_element_type=F32)
    decay_t = jnp.exp(b_last_t)
    yield

    qk = proj(_Q0, _V0)
    q = qk[:, :GLA_QK]
    k = qk[:, GLA_QK:]
    q_dec = q * (GLA_DK ** -0.5) * jnp.exp(b_cum)
    k_inv = (k * jnp.exp(-b_cum)).astype(BF16)
    k_end_t = (k * jnp.exp(b_rest)).T
    yield
    v = proj(_V0, _G0)
    vb = v.astype(BF16)
    g = proj(_G0, _U0)
    yield

    head_of_lane = lax.broadcasted_iota(jnp.int32, (ts, GLA_QK), 1) // GLA_DK
    o_heads = []
    for h in range(GLA_HEADS):
        q_h = jnp.where(head_of_lane == h, q_dec, 0.0).astype(BF16)
        sc = lax.dot_general(q_h, k_inv, _NT, preferred_element_type=F32)
        sc = jnp.where(causal, sc, 0.0).astype(BF16)
        o_heads.append(_dot(sc, vb[:, h * GLA_DV:(h + 1) * GLA_DV]))
        yield

    same_chunk = (lax.broadcasted_iota(jnp.int32, (ts, n_chunks * GLA_DV), 0) // GLA_CHUNK
                  == lax.broadcasted_iota(jnp.int32, (ts, n_chunks * GLA_DV), 1) // GLA_DV)
    upd_heads = []
    for h in range(GLA_HEADS):
        v_h = v[:, h * GLA_DV:(h + 1) * GLA_DV]
        v_by_chunk = jnp.where(same_chunk, jnp.concatenate([v_h] * n_chunks, axis=1), 0.0).astype(BF16)
        upd_heads.append(_dot(k_end_t[h * GLA_DK:(h + 1) * GLA_DK, :].astype(BF16), v_by_chunk))
        yield

    state = state_ref[ti]
    zero_blk = jnp.zeros((GLA_DK, GLA_DV), BF16)
    o_inter = []
    for c in range(n_chunks):
        rows = slice(c * GLA_CHUNK, (c + 1) * GLA_CHUNK)
        state_b = state.astype(BF16)
        state_bd = jnp.concatenate(
            [jnp.concatenate([state_b[h * GLA_DK:(h + 1) * GLA_DK] if h2 == h else zero_blk
                              for h2 in range(GLA_HEADS)], axis=1) for h in range(GLA_HEADS)], axis=0)
        o_inter.append(_dot(q_dec[rows].astype(BF16), state_bd))
        upd = jnp.concatenate([upd_heads[h][:, c * GLA_DV:(c + 1) * GLA_DV] for h in range(GLA_HEADS)], axis=0)
        state = decay_t[:, c * LANES:(c + 1) * LANES] * state + upd
        yield
    state_ref[ti] = state
    inter = jnp.concatenate(o_inter, axis=0)

    gla_parts = []
    for h in range(GLA_HEADS):
        o_h = o_heads[h] + inter[:, h * GLA_DV:(h + 1) * GLA_DV]
        ms_h = jnp.mean(o_h * o_h, axis=-1, keepdims=True)
        g_h = g[:, h * GLA_DV:(h + 1) * GLA_DV]
        gla_parts.append(o_h * lax.rsqrt(ms_h + EPS) * glanw_ref[:, h * GLA_DV:(h + 1) * GLA_DV]
                         * (g_h * jax.nn.sigmoid(g_h)))
        yield

    u = _gelu_tanh(proj(_U0, _VS0))
    vf = _gelu_tanh(proj(_VS0, _GK0))
    mu = jnp.mean(vf, axis=-1, keepdims=True)
    dv = vf - mu
    var = jnp.mean(dv * dv, axis=-1, keepdims=True)
    vn = (dv * lax.rsqrt(var + EPS) * lnw_ref[...] + lnb_ref[...]).astype(BF16)
    yield
    sg_base = (row // SG_CHUNK) * SG_CHUNK
    sg_mask = jnp.where(col <= row, jnp.where(col >= sg_base, 1.0, 0.0), 0.0) > 0.5
    sg_parts = []
    for gi in range(SG_GROUPS):
        w_c = jnp.where(sg_mask, w2_ref[gi], 0.0).astype(BF16)
        mixed_g = _dot(w_c, vn[:, gi * SG_CH:(gi + 1) * SG_CH]) + bsb_ref[gi]
        sg_parts.append(u[:, gi * SG_CH:(gi + 1) * SG_CH] * mixed_g)
        yield

    mixed = jnp.concatenate(gla_parts + sg_parts, axis=1).astype(BF16)
    x1 = x + _dot(mixed, wout_ref[...])
    x1_ref[ti] = x1
    yield

    ms2 = jnp.mean(x1 * x1, axis=-1, keepdims=True)
    h2 = x1 * lax.rsqrt(ms2 + EPS) * nfw_ref[...]
    h2_hi, h2_lo = _split_bf16(h2)
    h2_ref[ti] = h2_hi
    yield
    wr_hi = wrh_ref[...]
    logits = (lax.dot_general(wr_hi, h2_hi, _NT, preferred_element_type=F32)
              + lax.dot_general(wr_hi, h2_lo, _NT, preferred_element_type=F32)
              + lax.dot_general(wrl_ref[...], h2_hi, _NT, preferred_element_type=F32)
              + br_ref[...])
    yield
    e_iota = lax.broadcasted_iota(jnp.int32, (N_EXPERTS, ts), 0)
    work = logits
    top_val, top_idx = [], []
    for _ in range(TOP_K):
        m = jnp.max(work, axis=0, keepdims=True)
        i = jnp.min(jnp.where(work == m, e_iota, N_EXPERTS), axis=0, keepdims=True)
        top_val.append(m)
        top_idx.append(i)
        work = jnp.where(e_iota == i, -jnp.inf, work)
        yield
    ex = [jnp.exp(v - top_val[0]) for v in top_val]
    den = ex[0] + ex[1] + ex[2] + ex[3]
    top_w = [e / den for e in ex]
    yield

    onehot = jnp.zeros((N_EXPERTS, ts), F32)
    for i in top_idx:
        onehot = onehot + jnp.where(e_iota == i, 1.0, 0.0)
    onehot_b = onehot.astype(BF16)
    incl = jnp.where(row <= col, 1.0, 0.0).astype(BF16)
    csum = _dot(onehot_b, incl)
    total = _dot(onehot_b, jnp.ones((ts, ts), BF16))
    yield
    n_chunks_e = jnp.floor((total + (RUN_CHUNK - 1)) * (1.0 / RUN_CHUNK))
    e_lower = jnp.where(lax.broadcasted_iota(jnp.int32, (N_EXPERTS, N_EXPERTS), 1)
                        < lax.broadcasted_iota(jnp.int32, (N_EXPERTS, N_EXPERTS), 0), 1.0, 0.0).astype(BF16)
    local_start = RUN_CHUNK * _dot(e_lower, n_chunks_e.astype(BF16))
    local_pos = local_start + csum - onehot
    yield
    lpos = [jnp.sum(jnp.where(e_iota == i, local_pos, 0.0), axis=0, keepdims=True).astype(jnp.int32)
            for i in top_idx]
    onehot_lanes = jnp.concatenate([onehot_b, jnp.zeros((LANES - N_EXPERTS, ts), BF16)], axis=0)
    cnt_ref[ti, 0] = lax.dot_general(jnp.ones((SUBLANES, ts), BF16), onehot_lanes, _NT, preferred_element_type=F32)

    r8 = lax.broadcasted_iota(jnp.int32, (SUBLANES, ts), 0)

    def rows8(vals, fill):
        out = jnp.full((SUBLANES, ts), fill, vals[0].dtype)
        for kk, v in enumerate(vals):
            out = jnp.where(r8 == kk, v, out)
        return out

    lpos_ref[ti] = rows8(lpos, -1)
    wts_ref[ti] = rows8(top_w, 0.0)


def _token_mixing(x, nmw, wall, wgk, bgk, glanw, lnw, lnb, w2, bsb, wout, nfw, wrh, wrl, br):
    B, S, D = x.shape
    ts = SEQ_TILE
    ns = S // ts
    T = B * S

    def const(shape):
        return pl.BlockSpec(shape, lambda b, s: (0,) * len(shape))

    nbr = MIX_BATCH_ROWS
    return pl.pallas_call(
        _mix_kernel,
        grid=(B // nbr, ns),
        in_specs=[
            pl.BlockSpec((nbr, ts, D), lambda b, s: (b, s, 0)),
            const((1, D)), const((D, IN_COLS_PACKED)), const((LANES, GLA_QK)), const((1, GLA_QK)),
            const((1, GLA_WIDTH)), const((1, SG_WIDTH)), const((1, SG_WIDTH)),
            const((SG_GROUPS, ts, ts)), const((SG_GROUPS, ts, SG_CH)),
            const((D, D)), const((1, D)), const((N_EXPERTS, D)), const((N_EXPERTS, D)), const((N_EXPERTS, 1)),
        ],
        out_specs=[
            pl.BlockSpec((nbr, ts, D), lambda b, s: (b, s, 0)),
            pl.BlockSpec((nbr, ts, D), lambda b, s: (b, s, 0)),
            pl.BlockSpec((nbr, SUBLANES, ts), lambda b, s: (b, 0, s)),
            pl.BlockSpec((nbr, SUBLANES, ts), lambda b, s: (b, 0, s)),
            pl.BlockSpec((nbr, 1, SUBLANES, LANES), lambda b, s: (b, s, 0, 0)),
        ],
        out_shape=[
            jax.ShapeDtypeStruct((B, S, D), F32),
            jax.ShapeDtypeStruct((B, S, D), BF16),
            jax.ShapeDtypeStruct((B, SUBLANES, S), jnp.int32),
            jax.ShapeDtypeStruct((B, SUBLANES, S), F32),
            jax.ShapeDtypeStruct((B, ns, SUBLANES, LANES), F32),
        ],
        scratch_shapes=[pltpu.VMEM((nbr, GLA_QK, GLA_DV), F32)],
        compiler_params=pltpu.CompilerParams(
            dimension_semantics=("arbitrary", "arbitrary"), vmem_limit_bytes=VMEM_LIMIT_BYTES),
        name="token_mixing",
    )(x, nmw, wall, wgk, bgk, glanw, lnw, lnb, w2, bsb, wout, nfw, wrh, wrl, br)


def _chunk_rows(start_row, n_chunks=1):
    return pl.ds(pl.multiple_of(start_row * ROW_TILES, ROW_TILES), n_chunks * (RUN_CHUNK * ROW_TILES))


def _wait_chunks(hbm_ref, sem, n_chunks):
    @pl.when(n_chunks > 0)
    def _():
        rows = pl.ds(0, n_chunks * (RUN_CHUNK * ROW_TILES))
        pltpu.make_async_copy(hbm_ref.at[rows, :], hbm_ref.at[rows, :], sem).wait()


def _one_hot_positions(lpos_ref, values):
    ts = SEQ_TILE
    p_iota = lax.broadcasted_iota(jnp.int32, (LOCAL_ROWS, ts), 0)
    out = jnp.zeros((LOCAL_ROWS, ts), F32)
    for kk in range(TOP_K):
        out = jnp.where(p_iota == lpos_ref[kk:kk + 1, :], values[kk], out)
    return out.astype(BF16)


def _dispatch_kernel(goff_ref, nch_ref, loff_ref, ntot_ref, zst_ref, znc_ref,
                     h2_ref, lpos_ref, xbuf_hbm, srt_ref, zero_ref, sem, zsem):
    s = pl.program_id(0)
    n_steps = pl.num_programs(0)
    slot = s % 2

    @pl.when(s == 0)
    def _():
        zero_ref[...] = jnp.zeros_like(zero_ref)

        def per_expert(e, total):
            n_full = znc_ref[e] // ZERO_CHUNKS
            n_rest = znc_ref[e] % ZERO_CHUNKS

            def per_copy(c, carry):
                first = zst_ref[e] + c * (ZERO_CHUNKS * RUN_CHUNK)
                pltpu.make_async_copy(zero_ref, xbuf_hbm.at[_chunk_rows(first, ZERO_CHUNKS), :], zsem).start()
                return carry
            lax.fori_loop(0, n_full, per_copy, 0)

            @pl.when(n_rest > 0)
            def _():
                first = zst_ref[e] + n_full * (ZERO_CHUNKS * RUN_CHUNK)
                pltpu.make_async_copy(zero_ref.at[_chunk_rows(0, n_rest), :],
                                      xbuf_hbm.at[_chunk_rows(first, n_rest), :], zsem).start()
            return total + znc_ref[e]
        n_zero = lax.fori_loop(0, N_EXPERTS + 1, per_expert, 0)
        _wait_chunks(xbuf_hbm, zsem, n_zero)

    ones = [1.0] * TOP_K
    perm = _one_hot_positions(lpos_ref, ones)
    local_sorted = _dot(perm, h2_ref[...])

    _rows_to_tiles(srt_ref, (slot,), local_sorted, LOCAL_ROWS)

    @pl.when(s > 0)
    def _():
        _wait_chunks(xbuf_hbm, sem.at[1 - slot], ntot_ref[s - 1])

    def per_expert_pair(pair, carry):
        for priority in range(DMA_PRIORITIES):
            run = s * N_EXPERTS + pair * DMA_PRIORITIES + priority
            g0 = goff_ref[run]
            l0 = loff_ref[run]
            n = nch_ref[run]

            @pl.when(n > 0)
            def _():
                pltpu.make_async_copy(srt_ref.at[slot, _chunk_rows(l0, n), :],
                                      xbuf_hbm.at[_chunk_rows(g0, n), :], sem.at[slot]).start(priority=priority)
        return carry
    lax.fori_loop(0, N_EXPERTS // DMA_PRIORITIES, per_expert_pair, 0)

    @pl.when(s == n_steps - 1)
    def _():
        _wait_chunks(xbuf_hbm, sem.at[slot], ntot_ref[s])


def _dispatch(tables, h2b, lpos8, n_rows_buf):
    B, S, D = h2b.shape
    ts = SEQ_TILE
    ns = S // ts
    n_tiles = B * ns
    grid_spec = pltpu.PrefetchScalarGridSpec(
        num_scalar_prefetch=len(tables),
        grid=(n_tiles,),
        in_specs=[
            pl.BlockSpec((None, ts, D), lambda t, *_: (t // ns, t % ns, 0)),
            pl.BlockSpec((None, SUBLANES, ts), lambda t, *_: (t // ns, 0, t % ns)),
        ],
        out_specs=pl.BlockSpec(memory_space=pl.ANY),
        scratch_shapes=[
            pltpu.VMEM((2, LOCAL_ROWS * ROW_TILES, LANES), F32),
            pltpu.VMEM((ZERO_CHUNKS * RUN_CHUNK * ROW_TILES, LANES), F32),
            pltpu.SemaphoreType.DMA((2,)),
            pltpu.SemaphoreType.DMA(()),
        ],
    )
    return pl.pallas_call(
        _dispatch_kernel,
        grid_spec=grid_spec,
        out_shape=jax.ShapeDtypeStruct((n_rows_buf * ROW_TILES, LANES), F32),
        compiler_params=pltpu.CompilerParams(
            dimension_semantics=("arbitrary",), vmem_limit_bytes=VMEM_LIMIT_BYTES),
        name="dispatch",
    )(*tables, h2b, lpos8)


def _expert_kernel(be_ref, nu_ref, nv_ref, x_ref, wgu_hbm, bgu_ref, wd_hbm, bd_ref, y_ref,
                   wgu_f_ref, wd_f_ref, wgu_b_ref, wd_b_ref, wsem):
    tm = EXPERT_ROWS
    i = pl.program_id(0)
    n_used = nu_ref[0]
    e = be_ref[i]

    def weight_copies(expert, slot_):
        return (pltpu.make_async_copy(wgu_hbm.at[expert], wgu_f_ref.at[slot_], wsem.at[0, slot_]),
                pltpu.make_async_copy(wd_hbm.at[expert], wd_f_ref.at[slot_], wsem.at[1, slot_]))

    @pl.when(i == 0)
    def _():
        for cp in weight_copies(e, e % 2):
            cp.start()

    @pl.when(i < n_used)
    def _():
        first_of_expert = jnp.logical_or(i == 0, e != be_ref[jnp.maximum(i - 1, 0)])

        @pl.when(first_of_expert)
        def _():
            slot = e % 2
            for cp in weight_copies(e, slot):
                cp.wait()

            @pl.when(e + 1 < N_EXPERTS)
            def _():
                for cp in weight_copies(e + 1, 1 - slot):
                    cp.start()

            step = 128
            for r0 in range(0, D_MODEL, step):
                wgu_b_ref[r0:r0 + step, :] = wgu_f_ref[slot, r0:r0 + step, :].astype(BF16)
                wd_b_ref[r0:r0 + step, :] = wd_f_ref[slot, r0:r0 + step, :].astype(BF16)

        def mlp(n_rows):
            xb = _rows_from_tiles(x_ref, (), n_rows).astype(BF16)
            gu = _dot(xb, wgu_b_ref[...]) + bgu_ref[...]
            gate = jnp.minimum(gu[:, :D_EXPERT], SWIGLU_LIMIT)
            up = jnp.clip(gu[:, D_EXPERT:], -SWIGLU_LIMIT, SWIGLU_LIMIT)
            act = ((up + 1.0) * (gate * jax.nn.sigmoid(SWIGLU_ALPHA * gate))).astype(BF16)
            y = _dot(act, wd_b_ref[...]) + bd_ref[...]
            _rows_to_tiles(y_ref, (), y, n_rows)

        half = tm // 2

        @pl.when(nv_ref[i] > half)
        def _():
            mlp(tm)

        @pl.when(nv_ref[i] <= half)
        def _():
            mlp(half)
            y_ref[half * ROW_TILES:, :] = jnp.zeros((half * ROW_TILES, LANES), F32)

    @pl.when(i >= n_used)
    def _():
        y_ref[...] = jnp.zeros_like(y_ref)


def _experts(block_tables, nb, x_rows, w_gate_up, b_gate_up, w_down, b_down):
    tm = EXPERT_ROWS
    D = D_MODEL
    grid_spec = pltpu.PrefetchScalarGridSpec(
        num_scalar_prefetch=len(block_tables),
        grid=(nb,),
        in_specs=[
            pl.BlockSpec((tm * ROW_TILES, LANES), lambda i, be, nu, nv:(jnp.minimum(i, nu[0] - 1), 0)),
            pl.BlockSpec(memory_space=pl.ANY),
            pl.BlockSpec((None, 1, 2 * D_EXPERT), lambda i, be, nu, nv:(be[i], 0, 0)),
            pl.BlockSpec(memory_space=pl.ANY),
            pl.BlockSpec((None, 1, D), lambda i, be, nu, nv:(be[i], 0, 0)),
        ],
        out_specs=pl.BlockSpec((tm * ROW_TILES, LANES), lambda i, be, nu, nv:(i, 0)),
        scratch_shapes=[
            pltpu.VMEM((2, D, 2 * D_EXPERT), F32),
            pltpu.VMEM((2, D_EXPERT, D), F32),
            pltpu.VMEM((D, 2 * D_EXPERT), BF16),
            pltpu.VMEM((D_EXPERT, D), BF16),
            pltpu.SemaphoreType.DMA((2, 2)),
        ],
    )
    return pl.pallas_call(
        _expert_kernel,
        grid_spec=grid_spec,
        out_shape=jax.ShapeDtypeStruct((nb * tm * ROW_TILES, LANES), F32),
        compiler_params=pltpu.CompilerParams(
            dimension_semantics=("arbitrary",), vmem_limit_bytes=VMEM_LIMIT_BYTES),
        name="routed_experts",
    )(*block_tables, x_rows, w_gate_up, b_gate_up, w_down, b_down)


def _combine_kernel(goff_ref, nch_ref, loff_ref, ntot_ref,
                    lpos_ref, wts_ref, y_hbm, x1_ref, nw_ref, out_ref, ysrt_ref, sem):
    s = pl.program_id(0)
    n_steps = pl.num_programs(0)
    slot = s % 2

    def issue(tile, slot_):
        def per_expert_pair(pair, carry):
            for priority in range(DMA_PRIORITIES):
                run = tile * N_EXPERTS + pair * DMA_PRIORITIES + priority
                g0 = goff_ref[run]
                l0 = loff_ref[run]
                n = nch_ref[run]

                @pl.when(n > 0)
                def _():
                    pltpu.make_async_copy(y_hbm.at[_chunk_rows(g0, n), :],
                                          ysrt_ref.at[slot_, _chunk_rows(l0, n), :],
                                          sem.at[slot_]).start(priority=priority)
            return carry
        lax.fori_loop(0, N_EXPERTS // DMA_PRIORITIES, per_expert_pair, 0)

    @pl.when(s == 0)
    def _():
        issue(0, 0)

    @pl.when(s + 1 < n_steps)
    def _():
        issue(s + 1, 1 - slot)

    _wait_chunks(y_hbm, sem.at[slot], ntot_ref[s])

    weights = [wts_ref[kk:kk + 1, :] for kk in range(TOP_K)]
    perm_w = _one_hot_positions(lpos_ref, weights)
    y_local = _rows_from_tiles(ysrt_ref, (slot,), LOCAL_ROWS).astype(BF16)
    acc = x1_ref[...] + lax.dot_general(perm_w, y_local, _TN, preferred_element_type=F32)
    ms = jnp.mean(acc * acc, axis=-1, keepdims=True)
    out_ref[...] = acc * lax.rsqrt(ms + EPS) * nw_ref[...]


def _combine(tables, lpos8, wts8, y_rows, x1, norm_final_w):
    B, S, D = x1.shape
    ts = SEQ_TILE
    ns = S // ts
    n_tiles = B * ns
    meta = pl.BlockSpec((None, SUBLANES, ts), lambda t, *_: (t // ns, 0, t % ns))
    rows = pl.BlockSpec((None, ts, D), lambda t, *_: (t // ns, t % ns, 0))
    grid_spec = pltpu.PrefetchScalarGridSpec(
        num_scalar_prefetch=len(tables),
        grid=(n_tiles,),
        in_specs=[
            meta,
            meta,
            pl.BlockSpec(memory_space=pl.ANY),
            rows,
            pl.BlockSpec((1, D), lambda t, *_: (0, 0)),
        ],
        out_specs=rows,
        scratch_shapes=[
            pltpu.VMEM((2, LOCAL_ROWS * ROW_TILES, LANES), F32),
            pltpu.SemaphoreType.DMA((2,)),
        ],
    )
    return pl.pallas_call(
        _combine_kernel,
        grid_spec=grid_spec,
        out_shape=jax.ShapeDtypeStruct((B, S, D), F32),
        compiler_params=pltpu.CompilerParams(
            dimension_semantics=("arbitrary",), vmem_limit_bytes=VMEM_LIMIT_BYTES),
        name="combine",
    )(*tables, lpos8, wts8, y_rows, x1, norm_final_w)


def _pack_layer(w_in, w_gk_up, w_spatial, b_spatial, gla_norm_w, w_router, b_router):
    ts = SEQ_TILE
    q, k, v, g, gkl, u, vs = jnp.split(w_in, [256, 512, 1024, 1536, 1552, 2064], axis=1)
    gkl = jnp.pad(gkl, ((0, 0), (0, LANES - GLA_GATE_RANK)))
    wall = jnp.concatenate([q, k, v, g, u, vs, gkl], axis=1).astype(BF16)
    wgk = jnp.pad(w_gk_up, ((0, LANES - GLA_GATE_RANK), (0, 0))).astype(BF16)
    reps = ts // SG_CHUNK
    w2 = jnp.tile(w_spatial, (1, reps, reps)).astype(BF16)
    bsb = jnp.broadcast_to(jnp.tile(b_spatial, (1, reps))[:, :, None], (SG_GROUPS, ts, SG_CH))
    glanw = jnp.tile(gla_norm_w, GLA_HEADS)[None, :]
    wr_t = w_router.T
    wr_hi = wr_t.astype(BF16)
    wr_lo = (wr_t - wr_hi.astype(F32)).astype(BF16)
    return wall, wgk, w2, bsb, glanw, wr_hi, wr_lo, b_router[:, None]


def _routing_tables(tile_counts, nb):
    tm = EXPERT_ROWS
    n = tile_counts
    counts = jnp.sum(n, axis=0)
    padded = (counts + (RUN_CHUNK - 1) + tm - 1) // tm * tm
    padded_ends = jnp.cumsum(padded)
    padded_starts = padded_ends - padded
    goff = padded_starts[None, :] + jnp.cumsum(n, axis=0) - n
    nch = (n + RUN_CHUNK - 1) // RUN_CHUNK
    loff = RUN_CHUNK * (jnp.cumsum(nch, axis=1) - nch)
    ntot = jnp.sum(nch, axis=1)
    zero_start = jnp.concatenate([(padded_starts + counts) // RUN_CHUNK * RUN_CHUNK, padded_ends[-1:]])
    zero_end = jnp.concatenate([padded_ends, jnp.full((1,), nb * tm, padded_ends.dtype)])
    zero_chunks = (zero_end - zero_start) // RUN_CHUNK
    block_start = jnp.arange(nb, dtype=jnp.int32) * tm
    block_e = jnp.minimum(jnp.sum((block_start[:, None] >= padded_ends[None, :]).astype(jnp.int32), axis=1),
                          N_EXPERTS - 1)
    n_used = padded_ends[-1:] // tm
    real_end = padded_starts + counts
    is_e = block_e[:, None] == jnp.arange(N_EXPERTS, dtype=jnp.int32)[None, :]
    block_rows = jnp.clip(jnp.sum(jnp.where(is_e, real_end[None, :], 0), axis=1) - block_start, 0, tm)
    i32 = lambda a: a.astype(jnp.int32)
    run_tables = (i32(goff.reshape(-1)), i32(nch.reshape(-1)), i32(loff.reshape(-1)), i32(ntot))
    return run_tables, (i32(zero_start), i32(zero_chunks)), (i32(block_e), i32(n_used), i32(block_rows))


def kernel(x, norm_mix_w, w_in, w_gk_up, b_gk, gla_norm_w, sg_ln_w, sg_ln_b, w_spatial, b_spatial, w_out,
           norm_ffn_w, w_router, b_router, w_gate_up, b_gate_up, w_down, b_down, norm_final_w):
    B, S, D = x.shape
    T = B * S
    assert w_in.shape[0] == 1
    assert D == D_MODEL and S % SEQ_TILE == 0 and B % MIX_BATCH_ROWS == 0 and (T * TOP_K) % EXPERT_ROWS == 0
    tm = EXPERT_ROWS
    spare_blocks = -(-(N_EXPERTS * (RUN_CHUNK - 1)) // tm)
    nb = T * TOP_K // tm + N_EXPERTS + spare_blocks
    l = 0
    wall, wgk, w2, bsb, glanw, wr_hi, wr_lo, br = _pack_layer(
        w_in[l], w_gk_up[l], w_spatial[l], b_spatial[l], gla_norm_w[l], w_router[l], b_router[l])
    x1, h2b, lpos8, wts8, cnt = _token_mixing(
        x, norm_mix_w[l][None, :], wall, wgk, b_gk[l][None, :], glanw, sg_ln_w[l][None, :],
        sg_ln_b[l][None, :], w2, bsb, w_out[l].astype(BF16), norm_ffn_w[l][None, :], wr_hi, wr_lo, br)

    tile_counts = cnt[:, :, 0, :N_EXPERTS].reshape(T // SEQ_TILE, N_EXPERTS).astype(jnp.int32)
    run_tables, zero_tables, block_tables = _routing_tables(tile_counts, nb)
    x_rows = _dispatch(run_tables + zero_tables, h2b, lpos8, nb * tm)
    y_rows = _experts(block_tables, nb, x_rows, w_gate_up[l], b_gate_up[l][:, None, :],
                      w_down[l], b_down[l][:, None, :])
    return _combine(run_tables, lpos8, wts8, y_rows, x1, norm_final_w[None, :])
```

```python
import jax
import jax.numpy as jnp
from jax import lax
from jax.experimental import pallas as pl
from jax.experimental.pallas import tpu as pltpu

D_MODEL = 1024
GLA_HEADS = 4
GLA_DK = 64
GLA_DV = 128
GLA_QK = GLA_HEADS * GLA_DK
GLA_WIDTH = GLA_HEADS * GLA_DV
GLA_GATE_RANK = 16
GLA_GATE_NORMALIZER = 16.0
GLA_CHUNK = 64
SG_GROUPS = 4
SG_CH = 128
SG_WIDTH = SG_GROUPS * SG_CH
SG_CHUNK = 128
N_EXPERTS = 32
TOP_K = 4
D_EXPERT = D_MODEL
SWIGLU_LIMIT = 7.0
SWIGLU_ALPHA = 1.702
EPS = 1e-6

SUBLANES = 8
LANES = 128
ROW_TILES = D_MODEL // LANES
assert ROW_TILES == SUBLANES

SEQ_TILE = 256
MIX_BATCH_ROWS = 4
EXPERT_ROWS = 512
RUN_CHUNK = 1
ZERO_CHUNKS = 64
LOCAL_ROWS = SEQ_TILE * TOP_K
assert SEQ_TILE * TOP_K + N_EXPERTS * (RUN_CHUNK - 1) <= LOCAL_ROWS
VMEM_LIMIT_BYTES = 56 * 1024 * 1024

_Q0 = 0
_K0 = _Q0 + GLA_QK
_V0 = _K0 + GLA_QK
_G0 = _V0 + GLA_WIDTH
_U0 = _G0 + GLA_WIDTH
_VS0 = _U0 + SG_WIDTH
_GK0 = _VS0 + SG_WIDTH
IN_COLS_PACKED = _GK0 + LANES
WEIGHT_CAST_ROWS = 128

F32 = jnp.float32
BF16 = jnp.bfloat16
_NT = (((1,), (1,)), ((), ()))
_TN = (((0,), (0,)), ((), ()))


def _dot(a, b):
    return jnp.dot(a, b, preferred_element_type=F32)


def _split_bf16(a):
    hi = a.astype(BF16)
    lo = (a - hi.astype(F32)).astype(BF16)
    return hi, lo


def _gelu_tanh(a):
    return 0.5 * a * (1.0 + jnp.tanh(0.7978845608028654 * (a + 0.044715 * (a * a * a))))


def _rows_from_tiles(ref, lead, n_rows, base=0):
    parts = [ref[(*lead, pl.ds(base + j, n_rows, stride=ROW_TILES), slice(None))] for j in range(ROW_TILES)]
    return jnp.concatenate(parts, axis=1)


def _rows_to_tiles(ref, lead, val, n_rows, base=0):
    for j in range(ROW_TILES):
        ref[(*lead, pl.ds(base + j, n_rows, stride=ROW_TILES), slice(None))] = val[:, j * LANES:(j + 1) * LANES]


_DONE = object()


def _mix_kernel(*refs):
    state_ref = refs[-1]

    @pl.when(pl.program_id(1) == 0)
    def _():
        state_ref[...] = jnp.zeros_like(state_ref)

    tiles = [_mix_tile(ti, *refs) for ti in range(MIX_BATCH_ROWS)]
    while tiles:
        tiles = [t for t in tiles if next(t, _DONE) is not _DONE]


def _mix_tile(ti, x_ref, nmw_ref, wall_ref, wgk_ref, bgk_ref, glanw_ref, lnw_ref, lnb_ref, w2_ref, bsb_ref,
              wout_ref, nfw_ref, wrh_ref, wrl_ref, br_ref,
              x1_ref, h2_ref, lpos_ref, wts_ref, cnt_ref,
              state_ref):
    ts = SEQ_TILE
    n_chunks = ts // GLA_CHUNK

    x = x_ref[ti]
    ms = jnp.mean(x * x, axis=-1, keepdims=True)
    hb = (x * lax.rsqrt(ms + EPS) * nmw_ref[...]).astype(BF16)
    yield

    def proj(lo, hi):
        return _dot(hb, wall_ref[:, lo:hi])

    gkl = proj(_GK0, IN_COLS_PACKED)
    z = _dot(gkl.astype(BF16), wgk_ref[...]) + bgk_ref[...]
    log_a = (jnp.minimum(z, 0.0) - jnp.log(1.0 + jnp.exp(-jnp.abs(z)))) * (1.0 / GLA_GATE_NORMALIZER)
    la_hi, la_lo = _split_bf16(log_a)
    yield

    row = lax.broadcasted_iota(jnp.int32, (ts, ts), 0)
    col = lax.broadcasted_iota(jnp.int32, (ts, ts), 1)
    row_base = (row // GLA_CHUNK) * GLA_CHUNK
    in_chunk_le = jnp.where(col <= row, jnp.where(col >= row_base, 1.0, 0.0), 0.0)
    causal = in_chunk_le > 0.5
    lower = in_chunk_le.astype(BF16)
    b_cum = _dot(lower, la_hi) + _dot(lower, la_lo)
    b_last = jnp.concatenate(
        [jnp.broadcast_to(b_cum[(c + 1) * GLA_CHUNK - 1:(c + 1) * GLA_CHUNK, :], (GLA_CHUNK, GLA_QK))
         for c in range(n_chunks)], axis=0)
    b_rest = b_last - b_cum
    yield
    assert 2 * n_chunks == SUBLANES
    r8 = lax.broadcasted_iota(jnp.int32, (SUBLANES, GLA_QK), 0)
    ends = jnp.zeros((SUBLANES, GLA_QK), F32)
    for c in range(n_chunks):
        end_row = b_cum[(c + 1) * GLA_CHUNK - 1:(c + 1) * GLA_CHUNK, :]
        end_hi = end_row.astype(BF16).astype(F32)
        ends = jnp.where(r8 == c, end_hi, jnp.where(r8 == n_chunks + c, end_row - end_hi, ends))
    chunk_sel = jnp.where(
        lax.broadcasted_iota(jnp.int32, (SUBLANES, n_chunks * LANES), 0) % n_chunks
        == lax.broadcasted_iota(jnp.int32, (SUBLANES, n_chunks * LANES), 1) // LANES, 1.0, 0.0).astype(BF16)
    b_last_t = lax.dot_general(ends.astype(BF16), chunk_sel, _TN, preferred_element_type=F32)
    decay_t = jnp.exp(b_last_t)
    yield

    qk = proj(_Q0, _V0)
    q = qk[:, :GLA_QK]
    k = qk[:, GLA_QK:]
    q_dec = q * (GLA_DK ** -0.5) * jnp.exp(b_cum)
    k_inv = (k * jnp.exp(-b_cum)).astype(BF16)
    k_end_t = (k * jnp.exp(b_rest)).T
    yield
    v = proj(_V0, _G0)
    vb = v.astype(BF16)
    g = proj(_G0, _U0)
    yield

    head_of_lane = lax.broadcasted_iota(jnp.int32, (ts, GLA_QK), 1) // GLA_DK
    o_heads = []
    for h in range(GLA_HEADS):
        q_h = jnp.where(head_of_lane == h, q_dec, 0.0).astype(BF16)
        sc = lax.dot_general(q_h, k_inv, _NT, preferred_element_type=F32)
        sc = jnp.where(causal, sc, 0.0).astype(BF16)
        o_heads.append(_dot(sc, vb[:, h * GLA_DV:(h + 1) * GLA_DV]))
        yield

    same_chunk = (lax.broadcasted_iota(jnp.int32, (ts, n_chunks * GLA_DV), 0) // GLA_CHUNK
                  == lax.broadcasted_iota(jnp.int32, (ts, n_chunks * GLA_DV), 1) // GLA_DV)
    upd_heads = []
    for h in range(GLA_HEADS):
        v_h = v[:, h * GLA_DV:(h + 1) * GLA_DV]
        v_by_chunk = jnp.where(same_chunk, jnp.concatenate([v_h] * n_chunks, axis=1), 0.0).astype(BF16)
        upd_heads.append(_dot(k_end_t[h * GLA_DK:(h + 1) * GLA_DK, :].astype(BF16), v_by_chunk))
        yield

    state = state_ref[ti]
    zero_blk = jnp.zeros((GLA_DK, GLA_DV), BF16)
    o_inter = []
    for c in range(n_chunks):
        rows = slice(c * GLA_CHUNK, (c + 1) * GLA_CHUNK)
        state_b = state.astype(BF16)
        state_bd = jnp.concatenate(
            [jnp.concatenate([state_b[h * GLA_DK:(h + 1) * GLA_DK] if h2 == h else zero_blk
                              for h2 in range(GLA_HEADS)], axis=1) for h in range(GLA_HEADS)], axis=0)
        o_inter.append(_dot(q_dec[rows].astype(BF16), state_bd))
        upd = jnp.concatenate([upd_heads[h][:, c * GLA_DV:(c + 1) * GLA_DV] for h in range(GLA_HEADS)], axis=0)
        state = decay_t[:, c * LANES:(c + 1) * LANES] * state + upd
        yield
    state_ref[ti] = state
    inter = jnp.concatenate(o_inter, axis=0)

    gla_parts = []
    for h in range(GLA_HEADS):
        o_h = o_heads[h] + inter[:, h * GLA_DV:(h + 1) * GLA_DV]
        ms_h = jnp.mean(o_h * o_h, axis=-1, keepdims=True)
        g_h = g[:, h * GLA_DV:(h + 1) * GLA_DV]
        gla_parts.append(o_h * lax.rsqrt(ms_h + EPS) * glanw_ref[:, h * GLA_DV:(h + 1) * GLA_DV]
                         * (g_h * jax.nn.sigmoid(g_h)))
        yield

    u = _gelu_tanh(proj(_U0, _VS0))
    vf = _gelu_tanh(proj(_VS0, _GK0))
    mu = jnp.mean(vf, axis=-1, keepdims=True)
    dv = vf - mu
    var = jnp.mean(dv * dv, axis=-1, keepdims=True)
    vn = (dv * lax.rsqrt(var + EPS) * lnw_ref[...] + lnb_ref[...]).astype(BF16)
    yield
    sg_base = (row // SG_CHUNK) * SG_CHUNK
    sg_mask = jnp.where(col <= row, jnp.where(col >= sg_base, 1.0, 0.0), 0.0) > 0.5
    sg_parts = []
    for gi in range(SG_GROUPS):
        w_c = jnp.where(sg_mask, w2_ref[gi], 0.0).astype(BF16)
        mixed_g = _dot(w_c, vn[:, gi * SG_CH:(gi + 1) * SG_CH]) + bsb_ref[gi]
        sg_parts.append(u[:, gi * SG_CH:(gi + 1) * SG_CH] * mixed_g)
        yield

    mixed = jnp.concatenate(gla_parts + sg_parts, axis=1).astype(BF16)
    x1 = x + _dot(mixed, wout_ref[...])
    x1_ref[ti] = x1
    yield

    ms2 = jnp.mean(x1 * x1, axis=-1, keepdims=True)
    h2 = x1 * lax.rsqrt(ms2 + EPS) * nfw_ref[...]
    h2_hi, h2_lo = _split_bf16(h2)
    h2_ref[ti] = h2_hi
    yield
    wr_hi = wrh_ref[...]
    logits = (lax.dot_general(wr_hi, h2_hi, _NT, preferred_element_type=F32)
              + lax.dot_general(wr_hi, h2_lo, _NT, preferred_element_type=F32)
              + lax.dot_general(wrl_ref[...], h2_hi, _NT, preferred_element_type=F32)
              + br_ref[...])
    yield
    e_iota = lax.broadcasted_iota(jnp.int32, (N_EXPERTS, ts), 0)
    work = logits
    top_val, top_idx = [], []
    for _ in range(TOP_K):
        m = jnp.max(work, axis=0, keepdims=True)
        i = jnp.min(jnp.where(work == m, e_iota, N_EXPERTS), axis=0, keepdims=True)
        top_val.append(m)
        top_idx.append(i)
        work = jnp.where(e_iota == i, -jnp.inf, work)
        yield
    ex = [jnp.exp(v - top_val[0]) for v in top_val]
    den = ex[0] + ex[1] + ex[2] + ex[3]
    top_w = [e / den for e in ex]
    yield

    onehot = jnp.zeros((N_EXPERTS, ts), F32)
    for i in top_idx:
        onehot = onehot + jnp.where(e_iota == i, 1.0, 0.0)
    onehot_b = onehot.astype(BF16)
    incl = jnp.where(row <= col, 1.0, 0.0).astype(BF16)
    csum = _dot(onehot_b, incl)
    total = _dot(onehot_b, jnp.ones((ts, ts), BF16))
    yield
    n_chunks_e = jnp.floor((total + (RUN_CHUNK - 1)) * (1.0 / RUN_CHUNK))
    e_lower = jnp.where(lax.broadcasted_iota(jnp.int32, (N_EXPERTS, N_EXPERTS), 1)
                        < lax.broadcasted_iota(jnp.int32, (N_EXPERTS, N_EXPERTS), 0), 1.0, 0.0).astype(BF16)
    local_start = RUN_CHUNK * _dot(e_lower, n_chunks_e.astype(BF16))
    local_pos = local_start + csum - onehot
    yield
    lpos = [jnp.sum(jnp.where(e_iota == i, local_pos, 0.0), axis=0, keepdims=True).astype(jnp.int32)
            for i in top_idx]
    onehot_lanes = jnp.concatenate([onehot_b, jnp.zeros((LANES - N_EXPERTS, ts), BF16)], axis=0)
    cnt_ref[ti, 0] = lax.dot_general(jnp.ones((SUBLANES, ts), BF16), onehot_lanes, _NT, preferred_element_type=F32)

    r8 = lax.broadcasted_iota(jnp.int32, (SUBLANES, ts), 0)

    def rows8(vals, fill):
        out = jnp.full((SUBLANES, ts), fill, vals[0].dtype)
        for kk, v in enumerate(vals):
            out = jnp.where(r8 == kk, v, out)
        return out

    lpos_ref[ti] = rows8(lpos, -1)
    wts_ref[ti] = rows8(top_w, 0.0)


def _token_mixing(x, nmw, wall, wgk, bgk, glanw, lnw, lnb, w2, bsb, wout, nfw, wrh, wrl, br):
    B, S, D = x.shape
    ts = SEQ_TILE
    ns = S // ts
    T = B * S

    def const(shape):
        return pl.BlockSpec(shape, lambda b, s: (0,) * len(shape))

    nbr = MIX_BATCH_ROWS
    return pl.pallas_call(
        _mix_kernel,
        grid=(B // nbr, ns),
        in_specs=[
            pl.BlockSpec((nbr, ts, D), lambda b, s: (b, s, 0)),
            const((1, D)), const((D, IN_COLS_PACKED)), const((LANES, GLA_QK)), const((1, GLA_QK)),
            const((1, GLA_WIDTH)), const((1, SG_WIDTH)), const((1, SG_WIDTH)),
            const((SG_GROUPS, ts, ts)), const((SG_GROUPS, ts, SG_CH)),
            const((D, D)), const((1, D)), const((N_EXPERTS, D)), const((N_EXPERTS, D)), const((N_EXPERTS, 1)),
        ],
        out_specs=[
            pl.BlockSpec((nbr, ts, D), lambda b, s: (b, s, 0)),
            pl.BlockSpec((nbr, ts, D), lambda b, s: (b, s, 0)),
            pl.BlockSpec((nbr, SUBLANES, ts), lambda b, s: (b, 0, s)),
            pl.BlockSpec((nbr, SUBLANES, ts), lambda b, s: (b, 0, s)),
            pl.BlockSpec((nbr, 1, SUBLANES, LANES), lambda b, s: (b, s, 0, 0)),
        ],
        out_shape=[
            jax.ShapeDtypeStruct((B, S, D), F32),
            jax.ShapeDtypeStruct((B, S, D), BF16),
            jax.ShapeDtypeStruct((B, SUBLANES, S), jnp.int32),
            jax.ShapeDtypeStruct((B, SUBLANES, S), F32),
            jax.ShapeDtypeStruct((B, ns, SUBLANES, LANES), F32),
        ],
        scratch_shapes=[pltpu.VMEM((nbr, GLA_QK, GLA_DV), F32)],
        compiler_params=pltpu.CompilerParams(
            dimension_semantics=("arbitrary", "arbitrary"), vmem_limit_bytes=VMEM_LIMIT_BYTES),
        name="token_mixing",
    )(x, nmw, wall, wgk, bgk, glanw, lnw, lnb, w2, bsb, wout, nfw, wrh, wrl, br)


def _chunk_rows(start_row, n_chunks=1):
    return pl.ds(pl.multiple_of(start_row * ROW_TILES, ROW_TILES), n_chunks * (RUN_CHUNK * ROW_TILES))


def _wait_chunks(hbm_ref, sem, n_chunks):
    @pl.when(n_chunks > 0)
    def _():
        rows = pl.ds(0, n_chunks * (RUN_CHUNK * ROW_TILES))
        pltpu.make_async_copy(hbm_ref.at[rows, :], hbm_ref.at[rows, :], sem).wait()


def _one_hot_positions(lpos_ref, values):
    ts = SEQ_TILE
    p_iota = lax.broadcasted_iota(jnp.int32, (LOCAL_ROWS, ts), 0)
    out = jnp.zeros((LOCAL_ROWS, ts), F32)
    for kk in range(TOP_K):
        out = jnp.where(p_iota == lpos_ref[kk:kk + 1, :], values[kk], out)
    return out.astype(BF16)


def _dispatch_kernel(goff_ref, nch_ref, loff_ref, ntot_ref, zst_ref, znc_ref,
                     h2_ref, lpos_ref, xbuf_hbm, srt_ref, zero_ref, sem, zsem):
    s = pl.program_id(0)
    n_steps = pl.num_programs(0)
    slot = s % 2

    @pl.when(s == 0)
    def _():
        zero_ref[...] = jnp.zeros_like(zero_ref)

        def per_expert(e, total):
            n_full = znc_ref[e] // ZERO_CHUNKS
            n_rest = znc_ref[e] % ZERO_CHUNKS

            def per_copy(c, carry):
                first = zst_ref[e] + c * (ZERO_CHUNKS * RUN_CHUNK)
                pltpu.make_async_copy(zero_ref, xbuf_hbm.at[_chunk_rows(first, ZERO_CHUNKS), :], zsem).start()
                return carry
            lax.fori_loop(0, n_full, per_copy, 0)

            @pl.when(n_rest > 0)
            def _():
                first = zst_ref[e] + n_full * (ZERO_CHUNKS * RUN_CHUNK)
                pltpu.make_async_copy(zero_ref.at[_chunk_rows(0, n_rest), :],
                                      xbuf_hbm.at[_chunk_rows(first, n_rest), :], zsem).start()
            return total + znc_ref[e]
        n_zero = lax.fori_loop(0, N_EXPERTS + 1, per_expert, 0)
        _wait_chunks(xbuf_hbm, zsem, n_zero)

    ones = [1.0] * TOP_K
    perm = _one_hot_positions(lpos_ref, ones)
    local_sorted = _dot(perm, h2_ref[...])

    _rows_to_tiles(srt_ref, (slot,), local_sorted, LOCAL_ROWS)

    @pl.when(s > 0)
    def _():
        _wait_chunks(xbuf_hbm, sem.at[1 - slot], ntot_ref[s - 1])

    def per_expert(e, carry):
        g0 = goff_ref[s * N_EXPERTS + e]
        l0 = loff_ref[s * N_EXPERTS + e]

        n = nch_ref[s * N_EXPERTS + e]

        @pl.when(n > 0)
        def _():
            pltpu.make_async_copy(srt_ref.at[slot, _chunk_rows(l0, n), :],
                                  xbuf_hbm.at[_chunk_rows(g0, n), :], sem.at[slot]).start()
        return carry
    lax.fori_loop(0, N_EXPERTS, per_expert, 0)

    @pl.when(s == n_steps - 1)
    def _():
        _wait_chunks(xbuf_hbm, sem.at[slot], ntot_ref[s])


def _dispatch(tables, h2b, lpos8, n_rows_buf):
    B, S, D = h2b.shape
    ts = SEQ_TILE
    ns = S // ts
    n_tiles = B * ns
    grid_spec = pltpu.PrefetchScalarGridSpec(
        num_scalar_prefetch=len(tables),
        grid=(n_tiles,),
        in_specs=[
            pl.BlockSpec((None, ts, D), lambda t, *_: (t // ns, t % ns, 0)),
            pl.BlockSpec((None, SUBLANES, ts), lambda t, *_: (t // ns, 0, t % ns)),
        ],
        out_specs=pl.BlockSpec(memory_space=pl.ANY),
        scratch_shapes=[
            pltpu.VMEM((2, LOCAL_ROWS * ROW_TILES, LANES), F32),
            pltpu.VMEM((ZERO_CHUNKS * RUN_CHUNK * ROW_TILES, LANES), F32),
            pltpu.SemaphoreType.DMA((2,)),
            pltpu.SemaphoreType.DMA(()),
        ],
    )
    return pl.pallas_call(
        _dispatch_kernel,
        grid_spec=grid_spec,
        out_shape=jax.ShapeDtypeStruct((n_rows_buf * ROW_TILES, LANES), F32),
        compiler_params=pltpu.CompilerParams(
            dimension_semantics=("arbitrary",), vmem_limit_bytes=VMEM_LIMIT_BYTES),
        name="dispatch",
    )(*tables, h2b, lpos8)


def _expert_kernel(be_ref, nu_ref, nv_ref, x_ref, wgu_hbm, bgu_ref, wd_hbm, bd_ref, y_ref,
                   wgu_f_ref, wd_f_ref, wgu_b_ref, wd_b_ref, wsem):
    tm = EXPERT_ROWS
    i = pl.program_id(0)
    n_used = nu_ref[0]
    e = be_ref[i]

    def weight_copies(expert, slot_):
        return (pltpu.make_async_copy(wgu_hbm.at[expert], wgu_f_ref.at[slot_], wsem.at[0, slot_]),
                pltpu.make_async_copy(wd_hbm.at[expert], wd_f_ref.at[slot_], wsem.at[1, slot_]))

    @pl.when(i == 0)
    def _():
        for cp in weight_copies(e, e % 2):
            cp.start()

    @pl.when(i < n_used)
    def _():
        first_of_expert = jnp.logical_or(i == 0, e != be_ref[jnp.maximum(i - 1, 0)])

        @pl.when(first_of_expert)
        def _():
            slot = e % 2
            for cp in weight_copies(e, slot):
                cp.wait()

            @pl.when(e + 1 < N_EXPERTS)
            def _():
                for cp in weight_copies(e + 1, 1 - slot):
                    cp.start()

            for r0 in range(0, D_MODEL, WEIGHT_CAST_ROWS):
                rows = slice(r0, r0 + WEIGHT_CAST_ROWS)
                wgu_b_ref[rows, :] = wgu_f_ref[slot, rows, :].astype(BF16)
                wd_b_ref[rows, :] = wd_f_ref[slot, rows, :].astype(BF16)

        def mlp(n_rows):
            xb = _rows_from_tiles(x_ref, (), n_rows).astype(BF16)
            gu = _dot(xb, wgu_b_ref[...]) + bgu_ref[...]
            gate = jnp.minimum(gu[:, :D_EXPERT], SWIGLU_LIMIT)
            up = jnp.clip(gu[:, D_EXPERT:], -SWIGLU_LIMIT, SWIGLU_LIMIT)
            act = ((up + 1.0) * (gate * jax.nn.sigmoid(SWIGLU_ALPHA * gate))).astype(BF16)
            y = _dot(act, wd_b_ref[...]) + bd_ref[...]
            _rows_to_tiles(y_ref, (), y, n_rows)

        half = tm // 2

        @pl.when(nv_ref[i] > half)
        def _():
            mlp(tm)

        @pl.when(nv_ref[i] <= half)
        def _():
            mlp(half)
            y_ref[half * ROW_TILES:, :] = jnp.zeros((half * ROW_TILES, LANES), F32)

    @pl.when(i >= n_used)
    def _():
        y_ref[...] = jnp.zeros_like(y_ref)


def _experts(block_tables, nb, x_rows, w_gate_up, b_gate_up, w_down, b_down):
    tm = EXPERT_ROWS
    D = D_MODEL
    grid_spec = pltpu.PrefetchScalarGridSpec(
        num_scalar_prefetch=len(block_tables),
        grid=(nb,),
        in_specs=[
            pl.BlockSpec((tm * ROW_TILES, LANES), lambda i, be, nu, nv:(jnp.minimum(i, nu[0] - 1), 0)),
            pl.BlockSpec(memory_space=pl.ANY),
            pl.BlockSpec((None, 1, 2 * D_EXPERT), lambda i, be, nu, nv:(be[i], 0, 0)),
            pl.BlockSpec(memory_space=pl.ANY),
            pl.BlockSpec((None, 1, D), lambda i, be, nu, nv:(be[i], 0, 0)),
        ],
        out_specs=pl.BlockSpec((tm * ROW_TILES, LANES), lambda i, be, nu, nv:(i, 0)),
        scratch_shapes=[
            pltpu.VMEM((2, D, 2 * D_EXPERT), F32),
            pltpu.VMEM((2, D_EXPERT, D), F32),
            pltpu.VMEM((D, 2 * D_EXPERT), BF16),
            pltpu.VMEM((D_EXPERT, D), BF16),
            pltpu.SemaphoreType.DMA((2, 2)),
        ],
    )
    return pl.pallas_call(
        _expert_kernel,
        grid_spec=grid_spec,
        out_shape=jax.ShapeDtypeStruct((nb * tm * ROW_TILES, LANES), F32),
        compiler_params=pltpu.CompilerParams(
            dimension_semantics=("arbitrary",), vmem_limit_bytes=VMEM_LIMIT_BYTES),
        name="routed_experts",
    )(*block_tables, x_rows, w_gate_up, b_gate_up, w_down, b_down)


def _combine_kernel(goff_ref, nch_ref, loff_ref, ntot_ref,
                    lpos_ref, wts_ref, y_hbm, x1_ref, nw_ref, out_ref, ysrt_ref, sem):
    s = pl.program_id(0)
    n_steps = pl.num_programs(0)
    slot = s % 2

    def issue(tile, slot_):
        def per_expert(e, carry):
            g0 = goff_ref[tile * N_EXPERTS + e]
            l0 = loff_ref[tile * N_EXPERTS + e]

            n = nch_ref[tile * N_EXPERTS + e]

            @pl.when(n > 0)
            def _():
                pltpu.make_async_copy(y_hbm.at[_chunk_rows(g0, n), :],
                                      ysrt_ref.at[slot_, _chunk_rows(l0, n), :], sem.at[slot_]).start()
            return carry
        lax.fori_loop(0, N_EXPERTS, per_expert, 0)

    @pl.when(s == 0)
    def _():
        issue(0, 0)

    @pl.when(s + 1 < n_steps)
    def _():
        issue(s + 1, 1 - slot)

    _wait_chunks(y_hbm, sem.at[slot], ntot_ref[s])

    weights = [wts_ref[kk:kk + 1, :] for kk in range(TOP_K)]
    perm_w = _one_hot_positions(lpos_ref, weights)
    y_local = _rows_from_tiles(ysrt_ref, (slot,), LOCAL_ROWS).astype(BF16)
    acc = x1_ref[...] + lax.dot_general(perm_w, y_local, _TN, preferred_element_type=F32)
    ms = jnp.mean(acc * acc, axis=-1, keepdims=True)
    out_ref[...] = acc * lax.rsqrt(ms + EPS) * nw_ref[...]


def _combine(tables, lpos8, wts8, y_rows, x1, norm_final_w):
    B, S, D = x1.shape
    ts = SEQ_TILE
    ns = S // ts
    n_tiles = B * ns
    meta = pl.BlockSpec((None, SUBLANES, ts), lambda t, *_: (t // ns, 0, t % ns))
    rows = pl.BlockSpec((None, ts, D), lambda t, *_: (t // ns, t % ns, 0))
    grid_spec = pltpu.PrefetchScalarGridSpec(
        num_scalar_prefetch=len(tables),
        grid=(n_tiles,),
        in_specs=[
            meta,
            meta,
            pl.BlockSpec(memory_space=pl.ANY),
            rows,
            pl.BlockSpec((1, D), lambda t, *_: (0, 0)),
        ],
        out_specs=rows,
        scratch_shapes=[
            pltpu.VMEM((2, LOCAL_ROWS * ROW_TILES, LANES), F32),
            pltpu.SemaphoreType.DMA((2,)),
        ],
    )
    return pl.pallas_call(
        _combine_kernel,
        grid_spec=grid_spec,
        out_shape=jax.ShapeDtypeStruct((B, S, D), F32),
        compiler_params=pltpu.CompilerParams(
            dimension_semantics=("arbitrary",), vmem_limit_bytes=VMEM_LIMIT_BYTES),
        name="combine",
    )(*tables, lpos8, wts8, y_rows, x1, norm_final_w)


def _pack_layer(w_in, w_gk_up, w_spatial, b_spatial, gla_norm_w, w_router, b_router):
    ts = SEQ_TILE
    widths = (GLA_QK, GLA_QK, GLA_WIDTH, GLA_WIDTH, GLA_GATE_RANK, SG_WIDTH, SG_WIDTH)
    bounds = [sum(widths[:i + 1]) for i in range(len(widths) - 1)]
    q, k, v, g, gkl, u, vs = jnp.split(w_in, bounds, axis=1)
    gkl = jnp.pad(gkl, ((0, 0), (0, LANES - GLA_GATE_RANK)))
    wall = jnp.concatenate([q, k, v, g, u, vs, gkl], axis=1).astype(BF16)
    wgk = jnp.pad(w_gk_up, ((0, LANES - GLA_GATE_RANK), (0, 0))).astype(BF16)
    reps = ts // SG_CHUNK
    w2 = jnp.tile(w_spatial, (1, reps, reps)).astype(BF16)
    bsb = jnp.broadcast_to(jnp.tile(b_spatial, (1, reps))[:, :, None], (SG_GROUPS, ts, SG_CH))
    glanw = jnp.tile(gla_norm_w, GLA_HEADS)[None, :]
    wr_t = w_router.T
    wr_hi = wr_t.astype(BF16)
    wr_lo = (wr_t - wr_hi.astype(F32)).astype(BF16)
    return wall, wgk, w2, bsb, glanw, wr_hi, wr_lo, b_router[:, None]


def _routing_tables(tile_counts, nb):
    tm = EXPERT_ROWS
    n = tile_counts
    counts = jnp.sum(n, axis=0)
    padded = jnp.maximum((counts + (RUN_CHUNK - 1) + tm - 1) // tm, 1) * tm
    padded_ends = jnp.cumsum(padded)
    padded_starts = padded_ends - padded
    goff = padded_starts[None, :] + jnp.cumsum(n, axis=0) - n
    nch = (n + RUN_CHUNK - 1) // RUN_CHUNK
    loff = RUN_CHUNK * (jnp.cumsum(nch, axis=1) - nch)
    ntot = jnp.sum(nch, axis=1)
    zero_start = jnp.concatenate([(padded_starts + counts) // RUN_CHUNK * RUN_CHUNK, padded_ends[-1:]])
    zero_end = jnp.concatenate([padded_ends, jnp.full((1,), nb * tm, padded_ends.dtype)])
    zero_chunks = (zero_end - zero_start) // RUN_CHUNK
    block_start = jnp.arange(nb, dtype=jnp.int32) * tm
    block_e = jnp.minimum(jnp.sum((block_start[:, None] >= padded_ends[None, :]).astype(jnp.int32), axis=1),
                          N_EXPERTS - 1)
    n_used = padded_ends[-1:] // tm
    real_end = padded_starts + counts
    is_e = block_e[:, None] == jnp.arange(N_EXPERTS, dtype=jnp.int32)[None, :]
    block_rows = jnp.clip(jnp.sum(jnp.where(is_e, real_end[None, :], 0), axis=1) - block_start, 0, tm)
    i32 = lambda a: a.astype(jnp.int32)
    run_tables = (i32(goff.reshape(-1)), i32(nch.reshape(-1)), i32(loff.reshape(-1)), i32(ntot))
    return run_tables, (i32(zero_start), i32(zero_chunks)), (i32(block_e), i32(n_used), i32(block_rows))


def kernel(x, norm_mix_w, w_in, w_gk_up, b_gk, gla_norm_w, sg_ln_w, sg_ln_b, w_spatial, b_spatial, w_out,
           norm_ffn_w, w_router, b_router, w_gate_up, b_gate_up, w_down, b_down, norm_final_w):
    B, S, D = x.shape
    T = B * S
    assert w_in.shape[0] == 1
    assert D == D_MODEL and S % SEQ_TILE == 0 and B % MIX_BATCH_ROWS == 0
    tm = EXPERT_ROWS
    spare_blocks = -(-(N_EXPERTS * (RUN_CHUNK - 1)) // tm)
    nb = -(-(T * TOP_K) // tm) + N_EXPERTS + spare_blocks
    l = 0
    wall, wgk, w2, bsb, glanw, wr_hi, wr_lo, br = _pack_layer(
        w_in[l], w_gk_up[l], w_spatial[l], b_spatial[l], gla_norm_w[l], w_router[l], b_router[l])
    x1, h2b, lpos8, wts8, cnt = _token_mixing(
        x, norm_mix_w[l][None, :], wall, wgk, b_gk[l][None, :], glanw, sg_ln_w[l][None, :],
        sg_ln_b[l][None, :], w2, bsb, w_out[l].astype(BF16), norm_ffn_w[l][None, :], wr_hi, wr_lo, br)

    tile_counts = cnt[:, :, 0, :N_EXPERTS].reshape(T // SEQ_TILE, N_EXPERTS).astype(jnp.int32)
    run_tables, zero_tables, block_tables = _routing_tables(tile_counts, nb)
    x_rows = _dispatch(run_tables + zero_tables, h2b, lpos8, nb * tm)
    y_rows = _experts(block_tables, nb, x_rows, w_gate_up[l], b_gate_up[l][:, None, :],
                      w_down[l], b_down[l][:, None, :])
    return _combine(run_tables, lpos8, wts8, y_rows, x1, norm_final_w[None, :])
```

```python
import jax
import jax.numpy as jnp
from jax import lax
from jax.experimental import pallas as pl
from jax.experimental.pallas import tpu as pltpu

D_MODEL = 1024
GLA_HEADS = 4
GLA_DK = 64
GLA_DV = 128
GLA_QK = GLA_HEADS * GLA_DK
GLA_WIDTH = GLA_HEADS * GLA_DV
GLA_GATE_RANK = 16
GLA_GATE_NORMALIZER = 16.0
GLA_CHUNK = 64
SG_GROUPS = 4
SG_CH = 128
SG_WIDTH = SG_GROUPS * SG_CH
SG_CHUNK = 128
N_EXPERTS = 32
TOP_K = 4
D_EXPERT = D_MODEL
SWIGLU_LIMIT = 7.0
SWIGLU_ALPHA = 1.702
EPS = 1e-6

SUBLANES = 8
LANES = 128
ROW_TILES = D_MODEL // LANES
assert ROW_TILES == SUBLANES

SEQ_TILE = 256
MIX_BATCH_ROWS = 4
EXPERT_ROWS = 512
RUN_CHUNK = 1
COMBINE_BUFFERS = 3
ZERO_CHUNKS = 64
LOCAL_ROWS = SEQ_TILE * TOP_K
assert SEQ_TILE * TOP_K + N_EXPERTS * (RUN_CHUNK - 1) <= LOCAL_ROWS
VMEM_LIMIT_BYTES = 56 * 1024 * 1024

_Q0 = 0
_K0 = _Q0 + GLA_QK
_V0 = _K0 + GLA_QK
_G0 = _V0 + GLA_WIDTH
_U0 = _G0 + GLA_WIDTH
_VS0 = _U0 + SG_WIDTH
_GK0 = _VS0 + SG_WIDTH
IN_COLS_PACKED = _GK0 + LANES
WEIGHT_CAST_ROWS = 128

F32 = jnp.float32
BF16 = jnp.bfloat16
_NT = (((1,), (1,)), ((), ()))
_TN = (((0,), (0,)), ((), ()))


def _dot(a, b):
    return jnp.dot(a, b, preferred_element_type=F32)


def _split_bf16(a):
    hi = a.astype(BF16)
    lo = (a - hi.astype(F32)).astype(BF16)
    return hi, lo


def _gelu_tanh(a):
    return 0.5 * a * (1.0 + jnp.tanh(0.7978845608028654 * (a + 0.044715 * (a * a * a))))


def _rows_from_tiles(ref, lead, n_rows, base=0):
    parts = [ref[(*lead, pl.ds(base + j, n_rows, stride=ROW_TILES), slice(None))] for j in range(ROW_TILES)]
    return jnp.concatenate(parts, axis=1)


def _rows_to_tiles(ref, lead, val, n_rows, base=0):
    for j in range(ROW_TILES):
        ref[(*lead, pl.ds(base + j, n_rows, stride=ROW_TILES), slice(None))] = val[:, j * LANES:(j + 1) * LANES]


_DONE = object()


def _mix_kernel(*refs):
    state_ref = refs[-1]

    @pl.when(pl.program_id(1) == 0)
    def _():
        state_ref[...] = jnp.zeros_like(state_ref)

    tiles = [_mix_tile(ti, *refs) for ti in range(MIX_BATCH_ROWS)]
    while tiles:
        tiles = [t for t in tiles if next(t, _DONE) is not _DONE]


def _mix_tile(ti, x_ref, nmw_ref, wall_ref, wgk_ref, bgk_ref, glanw_ref, lnw_ref, lnb_ref, w2_ref, bsb_ref,
              wout_ref, nfw_ref, wrh_ref, wrl_ref, br_ref,
              x1_ref, h2_ref, lpos_ref, wts_ref, cnt_ref,
              state_ref):
    ts = SEQ_TILE
    n_chunks = ts // GLA_CHUNK

    x = x_ref[ti]
    ms = jnp.mean(x * x, axis=-1, keepdims=True)
    hb = (x * lax.rsqrt(ms + EPS) * nmw_ref[...]).astype(BF16)
    yield

    def proj(lo, hi):
        return _dot(hb, wall_ref[:, lo:hi])

    gkl = proj(_GK0, IN_COLS_PACKED)
    z = _dot(gkl.astype(BF16), wgk_ref[...]) + bgk_ref[...]
    log_a = (jnp.minimum(z, 0.0) - jnp.log(1.0 + jnp.exp(-jnp.abs(z)))) * (1.0 / GLA_GATE_NORMALIZER)
    la_hi, la_lo = _split_bf16(log_a)
    yield

    row = lax.broadcasted_iota(jnp.int32, (ts, ts), 0)
    col = lax.broadcasted_iota(jnp.int32, (ts, ts), 1)
    row_base = (row // GLA_CHUNK) * GLA_CHUNK
    in_chunk_le = jnp.where(col <= row, jnp.where(col >= row_base, 1.0, 0.0), 0.0)
    causal = in_chunk_le > 0.5
    lower = in_chunk_le.astype(BF16)
    b_cum = _dot(lower, la_hi) + _dot(lower, la_lo)
    b_last = jnp.concatenate(
        [jnp.broadcast_to(b_cum[(c + 1) * GLA_CHUNK - 1:(c + 1) * GLA_CHUNK, :], (GLA_CHUNK, GLA_QK))
         for c in range(n_chunks)], axis=0)
    b_rest = b_last - b_cum
    yield
    assert 2 * n_chunks == SUBLANES
    r8 = lax.broadcasted_iota(jnp.int32, (SUBLANES, GLA_QK), 0)
    ends = jnp.zeros((SUBLANES, GLA_QK), F32)
    for c in range(n_chunks):
        end_row = b_cum[(c + 1) * GLA_CHUNK - 1:(c + 1) * GLA_CHUNK, :]
        end_hi = end_row.astype(BF16).astype(F32)
        ends = jnp.where(r8 == c, end_hi, jnp.where(r8 == n_chunks + c, end_row - end_hi, ends))
    chunk_sel = jnp.where(
        lax.broadcasted_iota(jnp.int32, (SUBLANES, n_chunks * LANES), 0) % n_chunks
        == lax.broadcasted_iota(jnp.int32, (SUBLANES, n_chunks * LANES), 1) // LANES, 1.0, 0.0).astype(BF16)
    b_last_t = lax.dot_general(ends.astype(BF16), chunk_sel, _TN, preferred_element_type=F32)
    decay_t = jnp.exp(b_last_t)
    yield

    qk = proj(_Q0, _V0)
    q = qk[:, :GLA_QK]
    k = qk[:, GLA_QK:]
    q_dec = q * (GLA_DK ** -0.5) * jnp.exp(b_cum)
    k_inv = (k * jnp.exp(-b_cum)).astype(BF16)
    k_end_t = (k * jnp.exp(b_rest)).T
    yield
    v = proj(_V0, _G0)
    vb = v.astype(BF16)
    g = proj(_G0, _U0)
    yield

    head_of_lane = lax.broadcasted_iota(jnp.int32, (ts, GLA_QK), 1) // GLA_DK
    o_heads = []
    for h in range(GLA_HEADS):
        q_h = jnp.where(head_of_lane == h, q_dec, 0.0).astype(BF16)
        sc = lax.dot_general(q_h, k_inv, _NT, preferred_element_type=F32)
        sc = jnp.where(causal, sc, 0.0).astype(BF16)
        o_heads.append(_dot(sc, vb[:, h * GLA_DV:(h + 1) * GLA_DV]))
        yield

    same_chunk = (lax.broadcasted_iota(jnp.int32, (ts, n_chunks * GLA_DV), 0) // GLA_CHUNK
                  == lax.broadcasted_iota(jnp.int32, (ts, n_chunks * GLA_DV), 1) // GLA_DV)
    upd_heads = []
    for h in range(GLA_HEADS):
        v_h = v[:, h * GLA_DV:(h + 1) * GLA_DV]
        v_by_chunk = jnp.where(same_chunk, jnp.concatenate([v_h] * n_chunks, axis=1), 0.0).astype(BF16)
        upd_heads.append(_dot(k_end_t[h * GLA_DK:(h + 1) * GLA_DK, :].astype(BF16), v_by_chunk))
        yield

    state = state_ref[ti]
    zero_blk = jnp.zeros((GLA_DK, GLA_DV), BF16)
    o_inter = []
    for c in range(n_chunks):
        rows = slice(c * GLA_CHUNK, (c + 1) * GLA_CHUNK)
        state_b = state.astype(BF16)
        state_bd = jnp.concatenate(
            [jnp.concatenate([state_b[h * GLA_DK:(h + 1) * GLA_DK] if h2 == h else zero_blk
                              for h2 in range(GLA_HEADS)], axis=1) for h in range(GLA_HEADS)], axis=0)
        o_inter.append(_dot(q_dec[rows].astype(BF16), state_bd))
        upd = jnp.concatenate([upd_heads[h][:, c * GLA_DV:(c + 1) * GLA_DV] for h in range(GLA_HEADS)], axis=0)
        state = decay_t[:, c * LANES:(c + 1) * LANES] * state + upd
        yield
    state_ref[ti] = state
    inter = jnp.concatenate(o_inter, axis=0)

    gla_parts = []
    for h in range(GLA_HEADS):
        o_h = o_heads[h] + inter[:, h * GLA_DV:(h + 1) * GLA_DV]
        ms_h = jnp.mean(o_h * o_h, axis=-1, keepdims=True)
        g_h = g[:, h * GLA_DV:(h + 1) * GLA_DV]
        gla_parts.append(o_h * lax.rsqrt(ms_h + EPS) * glanw_ref[:, h * GLA_DV:(h + 1) * GLA_DV]
                         * (g_h * jax.nn.sigmoid(g_h)))
        yield

    u = _gelu_tanh(proj(_U0, _VS0))
    vf = _gelu_tanh(proj(_VS0, _GK0))
    mu = jnp.mean(vf, axis=-1, keepdims=True)
    dv = vf - mu
    var = jnp.mean(dv * dv, axis=-1, keepdims=True)
    vn = (dv * lax.rsqrt(var + EPS) * lnw_ref[...] + lnb_ref[...]).astype(BF16)
    yield
    sg_base = (row // SG_CHUNK) * SG_CHUNK
    sg_mask = jnp.where(col <= row, jnp.where(col >= sg_base, 1.0, 0.0), 0.0) > 0.5
    sg_parts = []
    for gi in range(SG_GROUPS):
        w_c = jnp.where(sg_mask, w2_ref[gi], 0.0).astype(BF16)
        mixed_g = _dot(w_c, vn[:, gi * SG_CH:(gi + 1) * SG_CH]) + bsb_ref[gi]
        sg_parts.append(u[:, gi * SG_CH:(gi + 1) * SG_CH] * mixed_g)
        yield

    mixed = jnp.concatenate(gla_parts + sg_parts, axis=1).astype(BF16)
    x1 = x + _dot(mixed, wout_ref[...])
    x1_ref[ti] = x1
    yield

    ms2 = jnp.mean(x1 * x1, axis=-1, keepdims=True)
    h2 = x1 * lax.rsqrt(ms2 + EPS) * nfw_ref[...]
    h2_hi, h2_lo = _split_bf16(h2)
    h2_ref[ti] = h2_hi
    yield
    wr_hi = wrh_ref[...]
    logits = (lax.dot_general(wr_hi, h2_hi, _NT, preferred_element_type=F32)
              + lax.dot_general(wr_hi, h2_lo, _NT, preferred_element_type=F32)
              + lax.dot_general(wrl_ref[...], h2_hi, _NT, preferred_element_type=F32)
              + br_ref[...])
    yield
    e_iota = lax.broadcasted_iota(jnp.int32, (N_EXPERTS, ts), 0)
    work = logits
    top_val, top_idx = [], []
    for _ in range(TOP_K):
        m = jnp.max(work, axis=0, keepdims=True)
        i = jnp.min(jnp.where(work == m, e_iota, N_EXPERTS), axis=0, keepdims=True)
        top_val.append(m)
        top_idx.append(i)
        work = jnp.where(e_iota == i, -jnp.inf, work)
        yield
    ex = [jnp.exp(v - top_val[0]) for v in top_val]
    den = ex[0] + ex[1] + ex[2] + ex[3]
    top_w = [e / den for e in ex]
    yield

    onehot = jnp.zeros((N_EXPERTS, ts), F32)
    for i in top_idx:
        onehot = onehot + jnp.where(e_iota == i, 1.0, 0.0)
    onehot_b = onehot.astype(BF16)
    incl = jnp.where(row <= col, 1.0, 0.0).astype(BF16)
    csum = _dot(onehot_b, incl)
    total = _dot(onehot_b, jnp.ones((ts, ts), BF16))
    yield
    n_chunks_e = jnp.floor((total + (RUN_CHUNK - 1)) * (1.0 / RUN_CHUNK))
    e_lower = jnp.where(lax.broadcasted_iota(jnp.int32, (N_EXPERTS, N_EXPERTS), 1)
                        < lax.broadcasted_iota(jnp.int32, (N_EXPERTS, N_EXPERTS), 0), 1.0, 0.0).astype(BF16)
    local_start = RUN_CHUNK * _dot(e_lower, n_chunks_e.astype(BF16))
    local_pos = local_start + csum - onehot
    yield
    lpos = [jnp.sum(jnp.where(e_iota == i, local_pos, 0.0), axis=0, keepdims=True).astype(jnp.int32)
            for i in top_idx]
    onehot_lanes = jnp.concatenate([onehot_b, jnp.zeros((LANES - N_EXPERTS, ts), BF16)], axis=0)
    cnt_ref[ti, 0] = lax.dot_general(jnp.ones((SUBLANES, ts), BF16), onehot_lanes, _NT, preferred_element_type=F32)

    r8 = lax.broadcasted_iota(jnp.int32, (SUBLANES, ts), 0)

    def rows8(vals, fill):
        out = jnp.full((SUBLANES, ts), fill, vals[0].dtype)
        for kk, v in enumerate(vals):
            out = jnp.where(r8 == kk, v, out)
        return out

    lpos_ref[ti] = rows8(lpos, -1)
    wts_ref[ti] = rows8(top_w, 0.0)


def _token_mixing(x, nmw, wall, wgk, bgk, glanw, lnw, lnb, w2, bsb, wout, nfw, wrh, wrl, br):
    B, S, D = x.shape
    ts = SEQ_TILE
    ns = S // ts
    T = B * S

    def const(shape):
        return pl.BlockSpec(shape, lambda b, s: (0,) * len(shape))

    nbr = MIX_BATCH_ROWS
    return pl.pallas_call(
        _mix_kernel,
        grid=(B // nbr, ns),
        in_specs=[
            pl.BlockSpec((nbr, ts, D), lambda b, s: (b, s, 0)),
            const((1, D)), const((D, IN_COLS_PACKED)), const((LANES, GLA_QK)), const((1, GLA_QK)),
            const((1, GLA_WIDTH)), const((1, SG_WIDTH)), const((1, SG_WIDTH)),
            const((SG_GROUPS, ts, ts)), const((SG_GROUPS, ts, SG_CH)),
            const((D, D)), const((1, D)), const((N_EXPERTS, D)), const((N_EXPERTS, D)), const((N_EXPERTS, 1)),
        ],
        out_specs=[
            pl.BlockSpec((nbr, ts, D), lambda b, s: (b, s, 0)),
            pl.BlockSpec((nbr, ts, D), lambda b, s: (b, s, 0)),
            pl.BlockSpec((nbr, SUBLANES, ts), lambda b, s: (b, 0, s)),
            pl.BlockSpec((nbr, SUBLANES, ts), lambda b, s: (b, 0, s)),
            pl.BlockSpec((nbr, 1, SUBLANES, LANES), lambda b, s: (b, s, 0, 0)),
        ],
        out_shape=[
            jax.ShapeDtypeStruct((B, S, D), F32),
            jax.ShapeDtypeStruct((B, S, D), BF16),
            jax.ShapeDtypeStruct((B, SUBLANES, S), jnp.int32),
            jax.ShapeDtypeStruct((B, SUBLANES, S), F32),
            jax.ShapeDtypeStruct((B, ns, SUBLANES, LANES), F32),
        ],
        scratch_shapes=[pltpu.VMEM((nbr, GLA_QK, GLA_DV), F32)],
        compiler_params=pltpu.CompilerParams(
            dimension_semantics=("arbitrary", "arbitrary"), vmem_limit_bytes=VMEM_LIMIT_BYTES),
        name="token_mixing",
    )(x, nmw, wall, wgk, bgk, glanw, lnw, lnb, w2, bsb, wout, nfw, wrh, wrl, br)


def _chunk_rows(start_row, n_chunks=1):
    return pl.ds(pl.multiple_of(start_row * ROW_TILES, ROW_TILES), n_chunks * (RUN_CHUNK * ROW_TILES))


def _wait_chunks(hbm_ref, sem, n_chunks):
    @pl.when(n_chunks > 0)
    def _():
        rows = pl.ds(0, n_chunks * (RUN_CHUNK * ROW_TILES))
        pltpu.make_async_copy(hbm_ref.at[rows, :], hbm_ref.at[rows, :], sem).wait()


def _one_hot_positions(lpos_ref, values):
    ts = SEQ_TILE
    p_iota = lax.broadcasted_iota(jnp.int32, (LOCAL_ROWS, ts), 0)
    out = jnp.zeros((LOCAL_ROWS, ts), F32)
    for kk in range(TOP_K):
        out = jnp.where(p_iota == lpos_ref[kk:kk + 1, :], values[kk], out)
    return out.astype(BF16)


def _dispatch_kernel(goff_ref, nch_ref, loff_ref, ntot_ref, zst_ref, znc_ref,
                     h2_ref, lpos_ref, xbuf_hbm, srt_ref, zero_ref, sem, zsem):
    s = pl.program_id(0)
    n_steps = pl.num_programs(0)
    slot = s % 2

    @pl.when(s == 0)
    def _():
        zero_ref[...] = jnp.zeros_like(zero_ref)

        def per_expert(e, total):
            n_full = znc_ref[e] // ZERO_CHUNKS
            n_rest = znc_ref[e] % ZERO_CHUNKS

            def per_copy(c, carry):
                first = zst_ref[e] + c * (ZERO_CHUNKS * RUN_CHUNK)
                pltpu.make_async_copy(zero_ref, xbuf_hbm.at[_chunk_rows(first, ZERO_CHUNKS), :], zsem).start()
                return carry
            lax.fori_loop(0, n_full, per_copy, 0)

            @pl.when(n_rest > 0)
            def _():
                first = zst_ref[e] + n_full * (ZERO_CHUNKS * RUN_CHUNK)
                pltpu.make_async_copy(zero_ref.at[_chunk_rows(0, n_rest), :],
                                      xbuf_hbm.at[_chunk_rows(first, n_rest), :], zsem).start()
            return total + znc_ref[e]
        n_zero = lax.fori_loop(0, N_EXPERTS + 1, per_expert, 0)
        _wait_chunks(xbuf_hbm, zsem, n_zero)

    ones = [1.0] * TOP_K
    perm = _one_hot_positions(lpos_ref, ones)
    local_sorted = _dot(perm, h2_ref[...])

    _rows_to_tiles(srt_ref, (slot,), local_sorted, LOCAL_ROWS)

    @pl.when(s > 0)
    def _():
        _wait_chunks(xbuf_hbm, sem.at[1 - slot], ntot_ref[s - 1])

    def per_expert(e, carry):
        g0 = goff_ref[s * N_EXPERTS + e]
        l0 = loff_ref[s * N_EXPERTS + e]

        n = nch_ref[s * N_EXPERTS + e]

        @pl.when(n > 0)
        def _():
            pltpu.make_async_copy(srt_ref.at[slot, _chunk_rows(l0, n), :],
                                  xbuf_hbm.at[_chunk_rows(g0, n), :], sem.at[slot]).start()
        return carry
    lax.fori_loop(0, N_EXPERTS, per_expert, 0)

    @pl.when(s == n_steps - 1)
    def _():
        _wait_chunks(xbuf_hbm, sem.at[slot], ntot_ref[s])


def _dispatch(tables, h2b, lpos8, n_rows_buf):
    B, S, D = h2b.shape
    ts = SEQ_TILE
    ns = S // ts
    n_tiles = B * ns
    grid_spec = pltpu.PrefetchScalarGridSpec(
        num_scalar_prefetch=len(tables),
        grid=(n_tiles,),
        in_specs=[
            pl.BlockSpec((None, ts, D), lambda t, *_: (t // ns, t % ns, 0)),
            pl.BlockSpec((None, SUBLANES, ts), lambda t, *_: (t // ns, 0, t % ns)),
        ],
        out_specs=pl.BlockSpec(memory_space=pl.ANY),
        scratch_shapes=[
            pltpu.VMEM((2, LOCAL_ROWS * ROW_TILES, LANES), F32),
            pltpu.VMEM((ZERO_CHUNKS * RUN_CHUNK * ROW_TILES, LANES), F32),
            pltpu.SemaphoreType.DMA((2,)),
            pltpu.SemaphoreType.DMA(()),
        ],
    )
    return pl.pallas_call(
        _dispatch_kernel,
        grid_spec=grid_spec,
        out_shape=jax.ShapeDtypeStruct((n_rows_buf * ROW_TILES, LANES), F32),
        compiler_params=pltpu.CompilerParams(
            dimension_semantics=("arbitrary",), vmem_limit_bytes=VMEM_LIMIT_BYTES),
        name="dispatch",
    )(*tables, h2b, lpos8)


def _expert_kernel(be_ref, nu_ref, nv_ref, x_ref, wgu_hbm, bgu_ref, wd_hbm, bd_ref, y_ref,
                   wgu_f_ref, wd_f_ref, wgu_b_ref, wd_b_ref, wsem):
    tm = EXPERT_ROWS
    i = pl.program_id(0)
    n_used = nu_ref[0]
    e = be_ref[i]

    def weight_copies(expert, slot_):
        return (pltpu.make_async_copy(wgu_hbm.at[expert], wgu_f_ref.at[slot_], wsem.at[0, slot_]),
                pltpu.make_async_copy(wd_hbm.at[expert], wd_f_ref.at[slot_], wsem.at[1, slot_]))

    @pl.when(i == 0)
    def _():
        for cp in weight_copies(e, e % 2):
            cp.start()

    @pl.when(i < n_used)
    def _():
        first_of_expert = jnp.logical_or(i == 0, e != be_ref[jnp.maximum(i - 1, 0)])

        @pl.when(first_of_expert)
        def _():
            slot = e % 2
            for cp in weight_copies(e, slot):
                cp.wait()

            @pl.when(e + 1 < N_EXPERTS)
            def _():
                for cp in weight_copies(e + 1, 1 - slot):
                    cp.start()

            for r0 in range(0, D_MODEL, WEIGHT_CAST_ROWS):
                rows = slice(r0, r0 + WEIGHT_CAST_ROWS)
                wgu_b_ref[rows, :] = wgu_f_ref[slot, rows, :].astype(BF16)
                wd_b_ref[rows, :] = wd_f_ref[slot, rows, :].astype(BF16)

        def mlp(n_rows):
            xb = _rows_from_tiles(x_ref, (), n_rows).astype(BF16)
            gu = _dot(xb, wgu_b_ref[...]) + bgu_ref[...]
            gate = jnp.minimum(gu[:, :D_EXPERT], SWIGLU_LIMIT)
            up = jnp.clip(gu[:, D_EXPERT:], -SWIGLU_LIMIT, SWIGLU_LIMIT)
            act = ((up + 1.0) * (gate * jax.nn.sigmoid(SWIGLU_ALPHA * gate))).astype(BF16)
            y = _dot(act, wd_b_ref[...]) + bd_ref[...]
            _rows_to_tiles(y_ref, (), y, n_rows)

        half = tm // 2

        @pl.when(nv_ref[i] > half)
        def _():
            mlp(tm)

        @pl.when(nv_ref[i] <= half)
        def _():
            mlp(half)
            y_ref[half * ROW_TILES:, :] = jnp.zeros((half * ROW_TILES, LANES), F32)

    @pl.when(i >= n_used)
    def _():
        y_ref[...] = jnp.zeros_like(y_ref)


def _experts(block_tables, nb, x_rows, w_gate_up, b_gate_up, w_down, b_down):
    tm = EXPERT_ROWS
    D = D_MODEL
    grid_spec = pltpu.PrefetchScalarGridSpec(
        num_scalar_prefetch=len(block_tables),
        grid=(nb,),
        in_specs=[
            pl.BlockSpec((tm * ROW_TILES, LANES), lambda i, be, nu, nv:(jnp.minimum(i, nu[0] - 1), 0)),
            pl.BlockSpec(memory_space=pl.ANY),
            pl.BlockSpec((None, 1, 2 * D_EXPERT), lambda i, be, nu, nv:(be[i], 0, 0)),
            pl.BlockSpec(memory_space=pl.ANY),
            pl.BlockSpec((None, 1, D), lambda i, be, nu, nv:(be[i], 0, 0)),
        ],
        out_specs=pl.BlockSpec((tm * ROW_TILES, LANES), lambda i, be, nu, nv:(i, 0)),
        scratch_shapes=[
            pltpu.VMEM((2, D, 2 * D_EXPERT), F32),
            pltpu.VMEM((2, D_EXPERT, D), F32),
            pltpu.VMEM((D, 2 * D_EXPERT), BF16),
            pltpu.VMEM((D_EXPERT, D), BF16),
            pltpu.SemaphoreType.DMA((2, 2)),
        ],
    )
    return pl.pallas_call(
        _expert_kernel,
        grid_spec=grid_spec,
        out_shape=jax.ShapeDtypeStruct((nb * tm * ROW_TILES, LANES), F32),
        compiler_params=pltpu.CompilerParams(
            dimension_semantics=("arbitrary",), vmem_limit_bytes=VMEM_LIMIT_BYTES),
        name="routed_experts",
    )(*block_tables, x_rows, w_gate_up, b_gate_up, w_down, b_down)


def _combine_kernel(goff_ref, nch_ref, loff_ref, ntot_ref,
                    lpos_ref, wts_ref, y_hbm, x1_ref, nw_ref, out_ref, ysrt_ref, sem):
    s = pl.program_id(0)
    n_steps = pl.num_programs(0)
    slot = s % COMBINE_BUFFERS

    def issue(tile):
        slot_ = tile % COMBINE_BUFFERS

        def per_expert(e, carry):
            g0 = goff_ref[tile * N_EXPERTS + e]
            l0 = loff_ref[tile * N_EXPERTS + e]

            n = nch_ref[tile * N_EXPERTS + e]

            @pl.when(n > 0)
            def _():
                pltpu.make_async_copy(y_hbm.at[_chunk_rows(g0, n), :],
                                      ysrt_ref.at[slot_, _chunk_rows(l0, n), :], sem.at[slot_]).start()
            return carry
        lax.fori_loop(0, N_EXPERTS, per_expert, 0)

    @pl.when(s == 0)
    def _():
        for t in range(COMBINE_BUFFERS - 1):
            @pl.when(t < n_steps)
            def _():
                issue(t)

    @pl.when(s + COMBINE_BUFFERS - 1 < n_steps)
    def _():
        issue(s + COMBINE_BUFFERS - 1)

    _wait_chunks(y_hbm, sem.at[slot], ntot_ref[s])

    weights = [wts_ref[kk:kk + 1, :] for kk in range(TOP_K)]
    perm_w = _one_hot_positions(lpos_ref, weights)
    y_local = _rows_from_tiles(ysrt_ref, (slot,), LOCAL_ROWS).astype(BF16)
    acc = x1_ref[...] + lax.dot_general(perm_w, y_local, _TN, preferred_element_type=F32)
    ms = jnp.mean(acc * acc, axis=-1, keepdims=True)
    out_ref[...] = acc * lax.rsqrt(ms + EPS) * nw_ref[...]


def _combine(tables, lpos8, wts8, y_rows, x1, norm_final_w):
    B, S, D = x1.shape
    ts = SEQ_TILE
    ns = S // ts
    n_tiles = B * ns
    meta = pl.BlockSpec((None, SUBLANES, ts), lambda t, *_: (t // ns, 0, t % ns))
    rows = pl.BlockSpec((None, ts, D), lambda t, *_: (t // ns, t % ns, 0))
    grid_spec = pltpu.PrefetchScalarGridSpec(
        num_scalar_prefetch=len(tables),
        grid=(n_tiles,),
        in_specs=[
            meta,
            meta,
            pl.BlockSpec(memory_space=pl.ANY),
            rows,
            pl.BlockSpec((1, D), lambda t, *_: (0, 0)),
        ],
        out_specs=rows,
        scratch_shapes=[
            pltpu.VMEM((COMBINE_BUFFERS, LOCAL_ROWS * ROW_TILES, LANES), F32),
            pltpu.SemaphoreType.DMA((COMBINE_BUFFERS,)),
        ],
    )
    return pl.pallas_call(
        _combine_kernel,
        grid_spec=grid_spec,
        out_shape=jax.ShapeDtypeStruct((B, S, D), F32),
        compiler_params=pltpu.CompilerParams(
            dimension_semantics=("arbitrary",), vmem_limit_bytes=VMEM_LIMIT_BYTES),
        name="combine",
    )(*tables, lpos8, wts8, y_rows, x1, norm_final_w)


def _pack_layer(w_in, w_gk_up, w_spatial, b_spatial, gla_norm_w, w_router, b_router):
    ts = SEQ_TILE
    widths = (GLA_QK, GLA_QK, GLA_WIDTH, GLA_WIDTH, GLA_GATE_RANK, SG_WIDTH, SG_WIDTH)
    bounds = [sum(widths[:i + 1]) for i in range(len(widths) - 1)]
    q, k, v, g, gkl, u, vs = jnp.split(w_in, bounds, axis=1)
    gkl = jnp.pad(gkl, ((0, 0), (0, LANES - GLA_GATE_RANK)))
    wall = jnp.concatenate([q, k, v, g, u, vs, gkl], axis=1).astype(BF16)
    wgk = jnp.pad(w_gk_up, ((0, LANES - GLA_GATE_RANK), (0, 0))).astype(BF16)
    reps = ts // SG_CHUNK
    w2 = jnp.tile(w_spatial, (1, reps, reps)).astype(BF16)
    bsb = jnp.broadcast_to(jnp.tile(b_spatial, (1, reps))[:, :, None], (SG_GROUPS, ts, SG_CH))
    glanw = jnp.tile(gla_norm_w, GLA_HEADS)[None, :]
    wr_t = w_router.T
    wr_hi = wr_t.astype(BF16)
    wr_lo = (wr_t - wr_hi.astype(F32)).astype(BF16)
    return wall, wgk, w2, bsb, glanw, wr_hi, wr_lo, b_router[:, None]


def _routing_tables(tile_counts, nb):
    tm = EXPERT_ROWS
    n = tile_counts
    counts = jnp.sum(n, axis=0)
    padded = jnp.maximum((counts + (RUN_CHUNK - 1) + tm - 1) // tm, 1) * tm
    padded_ends = jnp.cumsum(padded)
    padded_starts = padded_ends - padded
    goff = padded_starts[None, :] + jnp.cumsum(n, axis=0) - n
    nch = (n + RUN_CHUNK - 1) // RUN_CHUNK
    loff = RUN_CHUNK * (jnp.cumsum(nch, axis=1) - nch)
    ntot = jnp.sum(nch, axis=1)
    zero_start = jnp.concatenate([(padded_starts + counts) // RUN_CHUNK * RUN_CHUNK, padded_ends[-1:]])
    zero_end = jnp.concatenate([padded_ends, jnp.full((1,), nb * tm, padded_ends.dtype)])
    zero_chunks = (zero_end - zero_start) // RUN_CHUNK
    block_start = jnp.arange(nb, dtype=jnp.int32) * tm
    block_e = jnp.minimum(jnp.sum((block_start[:, None] >= padded_ends[None, :]).astype(jnp.int32), axis=1),
                          N_EXPERTS - 1)
    n_used = padded_ends[-1:] // tm
    real_end = padded_starts + counts
    is_e = block_e[:, None] == jnp.arange(N_EXPERTS, dtype=jnp.int32)[None, :]
    block_rows = jnp.clip(jnp.sum(jnp.where(is_e, real_end[None, :], 0), axis=1) - block_start, 0, tm)
    i32 = lambda a: a.astype(jnp.int32)
    run_tables = (i32(goff.reshape(-1)), i32(nch.reshape(-1)), i32(loff.reshape(-1)), i32(ntot))
    return run_tables, (i32(zero_start), i32(zero_chunks)), (i32(block_e), i32(n_used), i32(block_rows))


def kernel(x, norm_mix_w, w_in, w_gk_up, b_gk, gla_norm_w, sg_ln_w, sg_ln_b, w_spatial, b_spatial, w_out,
           norm_ffn_w, w_router, b_router, w_gate_up, b_gate_up, w_down, b_down, norm_final_w):
    B, S, D = x.shape
    T = B * S
    assert w_in.shape[0] == 1
    assert D == D_MODEL and S % SEQ_TILE == 0 and B % MIX_BATCH_ROWS == 0
    tm = EXPERT_ROWS
    spare_blocks = -(-(N_EXPERTS * (RUN_CHUNK - 1)) // tm)
    nb = -(-(T * TOP_K) // tm) + N_EXPERTS + spare_blocks
    l = 0
    wall, wgk, w2, bsb, glanw, wr_hi, wr_lo, br = _pack_layer(
        w_in[l], w_gk_up[l], w_spatial[l], b_spatial[l], gla_norm_w[l], w_router[l], b_router[l])
    x1, h2b, lpos8, wts8, cnt = _token_mixing(
        x, norm_mix_w[l][None, :], wall, wgk, b_gk[l][None, :], glanw, sg_ln_w[l][None, :],
        sg_ln_b[l][None, :], w2, bsb, w_out[l].astype(BF16), norm_ffn_w[l][None, :], wr_hi, wr_lo, br)

    tile_counts = cnt[:, :, 0, :N_EXPERTS].reshape(T // SEQ_TILE, N_EXPERTS).astype(jnp.int32)
    run_tables, zero_tables, block_tables = _routing_tables(tile_counts, nb)
    x_rows = _dispatch(run_tables + zero_tables, h2b, lpos8, nb * tm)
    y_rows = _experts(block_tables, nb, x_rows, w_gate_up[l], b_gate_up[l][:, None, :],
                      w_down[l], b_down[l][:, None, :])
    return _combine(run_tables, lpos8, wts8, y_rows, x1, norm_final_w[None, :])
```
